```python
import jax
import jax.numpy as jnp
from jax import lax
import numpy as np


D_MODEL = 2048
BATCH = 2
SEQ = 4096
DEPTH = 1

CHUNK = 64
NORM_EPS = 1e-6

RET_HEADS = 4
RET_DK = 256
RET_DV = 256
RET_QK = RET_HEADS * RET_DK
RET_WIDTH = RET_HEADS * RET_DV
ROPE_BASE = 10000.0

RWKV_HEAD = 64
RWKV_WIDTH = D_MODEL // 2
RWKV_HEADS = RWKV_WIDTH // RWKV_HEAD
DECAY_LORA = 64
AAA_LORA = 64
GATE_LORA = 128
RWKV_COLS = 3 * RWKV_WIDTH + DECAY_LORA + AAA_LORA + GATE_LORA
RWKV_GN_EPS = 64e-5

IN_COLS = 2 * RET_QK + 2 * RET_WIDTH + RWKV_COLS + 2 * D_MODEL

PEER_HEADS = 8
PEER_NKEYS = 128
PEER_NEXPERTS = PEER_NKEYS * PEER_NKEYS
PEER_DQ = 256
PEER_TOPK = 16
PEER_BLOCK = 128

kernel_name = 'hybrid_retention_rwkv7_peer_block'


def _split(z, sizes):
    idx = [int(s) for s in np.cumsum(sizes)[:-1]]
    return jnp.split(z, idx, axis=-1)


def rms_norm(x, w):
    xf = x.astype(jnp.float32)
    y = xf * lax.rsqrt(jnp.mean(xf * xf, axis=-1, keepdims=True) + NORM_EPS)
    return (y * w.astype(jnp.float32)).astype(x.dtype)


def head_norm(z, w, b, eps):
    h, d = z.shape[-2], z.shape[-1]
    zf = z.astype(jnp.float32)
    mu = jnp.mean(zf, axis=-1, keepdims=True)
    var = jnp.mean(jnp.square(zf - mu), axis=-1, keepdims=True)
    y = (zf - mu) * lax.rsqrt(var + eps)
    y = y * w.reshape(h, d).astype(jnp.float32) + b.reshape(h, d).astype(jnp.float32)
    return y.astype(z.dtype)


def rotary(z, pos):
    d = z.shape[-1]
    half = d // 2
    inv_freq = ROPE_BASE ** (-jnp.arange(half, dtype=jnp.float32) * 2.0 / d)
    ang = pos.astype(jnp.float32)[:, None] * inv_freq[None, :]
    cos = jnp.cos(ang)[None, :, None, :]
    sin = jnp.sin(ang)[None, :, None, :]
    zf = z.astype(jnp.float32)
    z1, z2 = zf[..., :half], zf[..., half:]
    return jnp.concatenate([z1 * cos - z2 * sin, z2 * cos + z1 * sin], axis=-1).astype(z.dtype)


def retention_chunkwise(q, k, v):
    b, t, h, dk = q.shape
    nc = t // CHUNK
    log_g = jnp.log(1.0 - jnp.exp2(-5.0 - jnp.arange(h, dtype=jnp.float32)))
    n = jnp.arange(CHUNK, dtype=jnp.float32)
    d_intra = jnp.exp(log_g[:, None, None] * jnp.abs(n[:, None] - n[None, :]))
    q_decay = jnp.exp(log_g[:, None] * (n[None, :] + 1.0))
    k_decay = jnp.exp(log_g[:, None] * (CHUNK - 1.0 - n[None, :]))
    chunk_decay = jnp.exp(log_g * CHUNK)

    def to_chunks(z):
        return z.reshape(b, nc, CHUNK, h, z.shape[-1]).transpose(0, 3, 1, 2, 4)

    qc, kc, vc = to_chunks(q), to_chunks(k), to_chunks(v)
    scores = jnp.einsum('bhcnd,bhcmd->bhcnm', qc, kc) * d_intra[None, :, None].astype(q.dtype)
    o_intra = jnp.einsum('bhcnm,bhcme->bhcne', scores, vc)
    kv = jnp.einsum('bhcmd,bhcme->bhcde', kc * k_decay[None, :, None, :, None].astype(k.dtype), vc)
    kv = jnp.moveaxis(kv, 2, 0).astype(jnp.float32)

    def step(s, kv_c):
        return s * chunk_decay[None, :, None, None] + kv_c, s

    s0 = jnp.zeros((b, h, dk, vc.shape[-1]), jnp.float32)
    _, s_prev = lax.scan(step, s0, kv)
    o_inter = jnp.einsum('bhcnd,cbhde->bhcne', qc * q_decay[None, :, None, :, None].astype(q.dtype),
                         s_prev.astype(q.dtype))
    o = o_intra + o_inter
    return o.transpose(0, 2, 3, 1, 4).reshape(b, t, h, vc.shape[-1])


def rwkv7_time_mix(z, mix, w0, w_up, a0, a_up, g_up, k_k, k_a, r_k, ln_w, ln_b):
    b, t, _ = z.shape
    z_prev = jnp.pad(z, ((0, 0), (1, 0), (0, 0)))[:, :-1]
    z = z + (z_prev - z) * mix
    r, k, v, xw, xa, xg = _split(z, [RWKV_WIDTH, RWKV_WIDTH, RWKV_WIDTH, DECAY_LORA, AAA_LORA, GATE_LORA])
    w = -jax.nn.softplus(-(w0 + jnp.tanh(xw) @ w_up)) - 0.5
    a = jax.nn.sigmoid(a0 + xa @ a_up)
    g = jax.nn.sigmoid(xg) @ g_up

    def heads(u):
        return u.reshape(b, t, RWKV_HEADS, RWKV_HEAD)

    kk = heads(k * k_k)
    kkf = kk.astype(jnp.float32)
    kk = (kkf / jnp.maximum(jnp.sqrt(jnp.sum(kkf * kkf, axis=-1, keepdims=True)), 1e-12)).astype(k.dtype)
    k = k * (1.0 + (a - 1.0) * k_a)
    r, k, v, a = heads(r), heads(k), heads(v), heads(a)
    decay = jnp.exp(-jnp.exp(heads(w).astype(jnp.float32)))

    def step(s, inp):
        r_t, w_t, k_t, v_t, kk_t, a_t = inp
        sa = jnp.einsum('bhij,bhj->bhi', s, -kk_t)
        s = s * w_t[:, :, None, :] + sa[..., None] * (kk_t * a_t)[:, :, None, :] + v_t[..., None] * k_t[:, :, None, :]
        return s, jnp.einsum('bhij,bhj->bhi', s, r_t)

    xs = tuple(jnp.moveaxis(u, 1, 0).astype(jnp.float32) for u in (r, decay, k, v, kk, a))
    s0 = jnp.zeros((b, RWKV_HEADS, RWKV_HEAD, RWKV_HEAD), jnp.float32)
    _, y = lax.scan(step, s0, xs)
    y = jnp.moveaxis(y, 0, 1).astype(z.dtype)
    y = head_norm(y, ln_w, ln_b, RWKV_GN_EPS)
    y = y + jnp.sum(r * k * r_k, axis=-1, keepdims=True) * v
    return y.reshape(b, t, RWKV_WIDTH) * g


def peer(u, w_q, keys_1, keys_2, expert_u, expert_v):
    b, t, d = u.shape
    tok = u.reshape(b * t, d)
    ntok = b * t
    q = (tok @ w_q).reshape(ntok, PEER_HEADS, 2, PEER_DQ // 2)
    s1 = jnp.einsum('nhd,kd->nhk', q[:, :, 0], keys_1).astype(jnp.float32)
    s2 = jnp.einsum('nhd,kd->nhk', q[:, :, 1], keys_2).astype(jnp.float32)
    v1, i1 = lax.top_k(s1, PEER_TOPK)
    v2, i2 = lax.top_k(s2, PEER_TOPK)
    cand = (v1[..., :, None] + v2[..., None, :]).reshape(ntok, PEER_HEADS, PEER_TOPK * PEER_TOPK)
    top, idx = lax.top_k(cand, PEER_TOPK)
    e1 = jnp.take_along_axis(i1, idx // PEER_TOPK, axis=-1)
    e2 = jnp.take_along_axis(i2, idx % PEER_TOPK, axis=-1)
    experts = e1 * PEER_NKEYS + e2
    gates = jax.nn.softmax(top, axis=-1).astype(u.dtype)
    nb = ntok // PEER_BLOCK

    def block(args):
        xb, eb, gb = args
        ue = jnp.take(expert_u, eb, axis=0)
        act = jax.nn.gelu(jnp.einsum('nhkd,nd->nhk', ue, xb), approximate=False) * gb
        return jnp.einsum('nhk,nhkd->nd', act, jnp.take(expert_v, eb, axis=0))

    out = lax.map(block, (tok.reshape(nb, PEER_BLOCK, d),
                          experts.reshape(nb, PEER_BLOCK, PEER_HEADS, PEER_TOPK),
                          gates.reshape(nb, PEER_BLOCK, PEER_HEADS, PEER_TOPK)))
    return out.reshape(b, t, d)


def setup_inputs(seed: int = 0) -> dict:
    key = jax.random.key(seed)
    ks = jax.random.split(key, 32)
    f32 = jnp.float32
    L = DEPTH

    def nrm(k, shape, scale):
        return jax.random.normal(k, shape, f32) * scale

    return {
        'x': nrm(ks[0], (BATCH, SEQ, D_MODEL), 1.0),
        'norm1_w': 1.0 + nrm(ks[1], (L, D_MODEL), 0.02),
        'w_in': nrm(ks[2], (L, D_MODEL, IN_COLS), D_MODEL ** -0.5),
        'ret_gn_w': 1.0 + nrm(ks[3], (L, RET_WIDTH), 0.02),
        'ret_gn_b': nrm(ks[4], (L, RET_WIDTH), 0.02),
        'rwkv_mix': jax.random.uniform(ks[5], (L, RWKV_COLS), f32),
        'rwkv_w0': jax.random.uniform(ks[6], (L, RWKV_WIDTH), f32, -6.0, -1.0),
        'rwkv_w_up': nrm(ks[7], (L, DECAY_LORA, RWKV_WIDTH), 0.5 * DECAY_LORA ** -0.5),
        'rwkv_a0': nrm(ks[8], (L, RWKV_WIDTH), 0.1),
        'rwkv_a_up': nrm(ks[9], (L, AAA_LORA, RWKV_WIDTH), 0.5 * AAA_LORA ** -0.5),
        'rwkv_g_up': nrm(ks[10], (L, GATE_LORA, RWKV_WIDTH), GATE_LORA ** -0.5),
        'rwkv_k_k': 0.85 + nrm(ks[11], (L, RWKV_WIDTH), 0.02),
        'rwkv_k_a': 1.0 + nrm(ks[12], (L, RWKV_WIDTH), 0.02),
        'rwkv_r_k': nrm(ks[13], (L, RWKV_HEADS, RWKV_HEAD), 0.1),
        'rwkv_ln_w': 1.0 + nrm(ks[14], (L, RWKV_WIDTH), 0.02),
        'rwkv_ln_b': nrm(ks[15], (L, RWKV_WIDTH), 0.02),
        'w_ret_branch': nrm(ks[16], (L, RET_WIDTH, D_MODEL), RET_WIDTH ** -0.5),
        'w_rwkv_branch': nrm(ks[17], (L, RWKV_WIDTH, D_MODEL), RWKV_WIDTH ** -0.5),
        'w_out': nrm(ks[18], (L, D_MODEL, D_MODEL), D_MODEL ** -0.5),
        'norm2_w': 1.0 + nrm(ks[19], (L, D_MODEL), 0.02),
        'peer_w_q': nrm(ks[20], (L, D_MODEL, PEER_HEADS * PEER_DQ), D_MODEL ** -0.5),
        'peer_keys_1': nrm(ks[21], (L, PEER_NKEYS, PEER_DQ // 2), (PEER_DQ // 2) ** -0.5),
        'peer_keys_2': nrm(ks[22], (L, PEER_NKEYS, PEER_DQ // 2), (PEER_DQ // 2) ** -0.5),
        'peer_u': nrm(ks[23], (L, PEER_NEXPERTS, D_MODEL), D_MODEL ** -0.5),
        'peer_v': nrm(ks[24], (L, PEER_NEXPERTS, D_MODEL), 0.5),
        'final_norm_w': 1.0 + nrm(ks[25], (D_MODEL,), 0.02),
    }


def reference(x, norm1_w, w_in, ret_gn_w, ret_gn_b, rwkv_mix, rwkv_w0, rwkv_w_up, rwkv_a0,
              rwkv_a_up, rwkv_g_up, rwkv_k_k, rwkv_k_a, rwkv_r_k, rwkv_ln_w, rwkv_ln_b,
              w_ret_branch, w_rwkv_branch, w_out, norm2_w, peer_w_q, peer_keys_1, peer_keys_2,
              peer_u, peer_v, final_norm_w):
    b, t, _ = x.shape
    pos = jnp.arange(t, dtype=jnp.int32)
    for layer in range(DEPTH):
        h = rms_norm(x, norm1_w[layer])
        z = h @ w_in[layer]
        q, k, v, g_ret_out, z_rwkv, gate_ret, gate_rwkv = _split(
            z, [RET_QK, RET_QK, RET_WIDTH, RET_WIDTH, RWKV_COLS, D_MODEL, D_MODEL])
        q = rotary(q.reshape(b, t, RET_HEADS, RET_DK), pos)
        k = rotary(k.reshape(b, t, RET_HEADS, RET_DK), pos) * (RET_DK ** -0.5)
        o_ret = retention_chunkwise(q, k, v.reshape(b, t, RET_HEADS, RET_DV))
        o_ret = head_norm(o_ret, ret_gn_w[layer], ret_gn_b[layer], NORM_EPS).reshape(b, t, RET_WIDTH)
        o_ret = o_ret * jax.nn.silu(g_ret_out)
        o_rwkv = rwkv7_time_mix(z_rwkv, rwkv_mix[layer], rwkv_w0[layer], rwkv_w_up[layer], rwkv_a0[layer],
                                rwkv_a_up[layer], rwkv_g_up[layer], rwkv_k_k[layer], rwkv_k_a[layer],
                                rwkv_r_k[layer], rwkv_ln_w[layer], rwkv_ln_b[layer])
        merged = (jax.nn.sigmoid(gate_ret) * (o_ret @ w_ret_branch[layer])
                  + jax.nn.sigmoid(gate_rwkv) * (o_rwkv @ w_rwkv_branch[layer]))
        x = x + merged @ w_out[layer]
        h2 = rms_norm(x, norm2_w[layer])
        x = x + peer(h2, peer_w_q[layer], peer_keys_1[layer], peer_keys_2[layer], peer_u[layer], peer_v[layer])
    return rms_norm(x, final_norm_w)
```

```python
import functools

import numpy as np
import jax
import jax.numpy as jnp
from jax import lax
from jax.experimental import pallas as pl
from jax.experimental.pallas import tpu as pltpu

F32 = jnp.float32
BF16 = jnp.bfloat16

D_MODEL = 2048
CHUNK = 64
NORM_EPS = 1e-6
RET_HEADS = 4
RET_DK = 256
RET_DV = 256
RET_QK = RET_HEADS * RET_DK
RET_WIDTH = RET_HEADS * RET_DV
ROPE_BASE = 10000.0
RWKV_HEAD = 64
RWKV_WIDTH = D_MODEL // 2
RWKV_HEADS = RWKV_WIDTH // RWKV_HEAD
DECAY_LORA = 64
AAA_LORA = 64
GATE_LORA = 128
LORA_COLS = DECAY_LORA + AAA_LORA + GATE_LORA
RWKV_COLS = 3 * RWKV_WIDTH + LORA_COLS
RWKV_GN_EPS = 64e-5
IN_COLS = 2 * RET_QK + 2 * RET_WIDTH + RWKV_COLS + 2 * D_MODEL

ZP_GATE_RET = 4096
ZP_GATE_RWKV = 6144
ZP_RWKV = 8192
ZP_LORA = 11264

VMEM_LIMIT = 56 * 1024 * 1024


def _cparams(sem):
    return pltpu.CompilerParams(dimension_semantics=sem, vmem_limit_bytes=VMEM_LIMIT)


def _in_proj_kernel(x_ref, nw_ref, w_ref, o_ref, h_ref):
    @pl.when(pl.program_id(1) == 0)
    def _():
        x = x_ref[...]
        ms = jnp.mean(x * x, axis=-1, keepdims=True)
        h_ref[...] = (x * lax.rsqrt(ms + NORM_EPS) * nw_ref[...]).astype(BF16)

    o_ref[...] = jnp.dot(h_ref[...], w_ref[...], preferred_element_type=F32)


def in_proj(x, norm_w, w_bf16, tm=512, tn=1280):
    n, d = x.shape
    cols = w_bf16.shape[1]
    return pl.pallas_call(
        _in_proj_kernel,
        grid=(n // tm, cols // tn),
        in_specs=[
            pl.BlockSpec((tm, d), lambda i, j: (i, 0)),
            pl.BlockSpec((1, d), lambda i, j: (0, 0)),
            pl.BlockSpec((d, tn), lambda i, j: (0, j)),
        ],
        out_specs=pl.BlockSpec((tm, tn), lambda i, j: (i, j)),
        out_shape=jax.ShapeDtypeStruct((n, cols), F32),
        scratch_shapes=[pltpu.VMEM((tm, d), BF16)],
        compiler_params=_cparams(("arbitrary", "arbitrary")),
        name="in_proj",
    )(x, norm_w.reshape(1, d), w_bf16)


def _retention_kernel(q_ref, k_ref, v_ref, g_ref, cos_ref, sin_ref, dintra_ref, qdec_ref, kdec_ref,
                      cdec_ref, gnw_ref, gnb_ref, o_ref, s_ref, qr_ref, kr_ref, acc_ref):
    tb = q_ref.shape[0]
    half = RET_DK // 2

    @pl.when(pl.program_id(2) == 0)
    def _():
        s_ref[...] = jnp.zeros_like(s_ref)

    cos = cos_ref[...]
    sin = sin_ref[...]

    def rot(z_ref):
        z1 = z_ref[:, :half]
        z2 = z_ref[:, half:]
        return jnp.concatenate([z1 * cos - z2 * sin, z2 * cos + z1 * sin], axis=-1)

    qr_ref[...] = rot(q_ref)
    kr_ref[...] = rot(k_ref) * (RET_DK ** -0.5)
    d_intra = dintra_ref[0]
    q_decay = qdec_ref[0]
    k_decay = kdec_ref[0]
    c_decay = cdec_ref[0]
    for c in range(tb // CHUNK):
        sl = pl.ds(c * CHUNK, CHUNK)
        qc = qr_ref[sl, :]
        kc = kr_ref[sl, :]
        vc = v_ref[sl, :].astype(BF16)
        scores = lax.dot_general(qc.astype(BF16), kc.astype(BF16), (((1,), (1,)), ((), ())),
                                 preferred_element_type=F32) * d_intra
        s_prev = s_ref[...]
        o = jnp.dot(scores.astype(BF16), vc, preferred_element_type=F32)
        o = o + jnp.dot((qc * q_decay).astype(BF16), s_prev.astype(BF16), preferred_element_type=F32)
        kd_t = (kc * k_decay).T.astype(BF16)
        s_ref[...] = s_prev * c_decay + jnp.dot(kd_t, vc, preferred_element_type=F32)
        acc_ref[sl, :] = o
    o = acc_ref[...]
    mu = jnp.mean(o, axis=-1, keepdims=True)
    oc = o - mu
    var = jnp.mean(oc * oc, axis=-1, keepdims=True)
    y = oc * lax.rsqrt(var + NORM_EPS) * gnw_ref[...] + gnb_ref[...]
    g = g_ref[...]
    o_ref[...] = y * (g * jax.nn.sigmoid(g))


def _retention_tables(t):
    h = RET_HEADS
    log_g = jnp.log(1.0 - jnp.exp2(-5.0 - jnp.arange(h, dtype=F32)))
    n = jnp.arange(CHUNK, dtype=F32)
    d_intra = jnp.exp(log_g[:, None, None] * jnp.abs(n[:, None] - n[None, :]))
    q_decay = jnp.exp(log_g[:, None] * (n[None, :] + 1.0))
    k_decay = jnp.exp(log_g[:, None] * (CHUNK - 1.0 - n[None, :]))
    chunk_decay = jnp.exp(log_g * CHUNK)
    q_decay = jnp.broadcast_to(q_decay[:, :, None], (h, CHUNK, RET_DK))
    k_decay = jnp.broadcast_to(k_decay[:, :, None], (h, CHUNK, RET_DK))
    chunk_decay = jnp.broadcast_to(chunk_decay[:, None, None], (h, 1, RET_DV))
    half = RET_DK // 2
    inv_freq = ROPE_BASE ** (-jnp.arange(half, dtype=F32) * 2.0 / RET_DK)
    ang = jnp.arange(t, dtype=jnp.int32).astype(F32)[:, None] * inv_freq[None, :]
    return d_intra, q_decay, k_decay, chunk_decay, jnp.cos(ang), jnp.sin(ang)


def retention(zp, gn_w, gn_b, batch, t, tb=512):
    n = zp.shape[0]
    h = RET_HEADS
    nt = t // tb
    d_intra, q_decay, k_decay, chunk_decay, cos, sin = _retention_tables(t)

    def zspec(col0):
        return pl.BlockSpec((tb, RET_DK), lambda b, hh, i: (b * nt + i, col0 + hh))

    tab = lambda shape: pl.BlockSpec((1,) + shape, lambda b, hh, i: (hh, 0, 0))
    return pl.pallas_call(
        _retention_kernel,
        grid=(batch, h, nt),
        in_specs=[
            zspec(0), zspec(h), zspec(2 * h), zspec(3 * h),
            pl.BlockSpec((tb, RET_DK // 2), lambda b, hh, i: (i, 0)),
            pl.BlockSpec((tb, RET_DK // 2), lambda b, hh, i: (i, 0)),
            tab((CHUNK, CHUNK)), tab((CHUNK, RET_DK)), tab((CHUNK, RET_DK)), tab((1, RET_DV)),
            pl.BlockSpec((1, RET_DV), lambda b, hh, i: (0, hh)),
            pl.BlockSpec((1, RET_DV), lambda b, hh, i: (0, hh)),
        ],
        out_specs=pl.BlockSpec((tb, RET_DV), lambda b, hh, i: (b * nt + i, hh)),
        out_shape=jax.ShapeDtypeStruct((n, RET_WIDTH), F32),
        scratch_shapes=[
            pltpu.VMEM((RET_DK, RET_DV), F32),
            pltpu.VMEM((tb, RET_DK), F32),
            pltpu.VMEM((tb, RET_DK), F32),
            pltpu.VMEM((tb, RET_DV), F32),
        ],
        compiler_params=_cparams(("arbitrary", "arbitrary", "arbitrary")),
        name="retention",
    )(zp, zp, zp, zp, cos, sin, d_intra, q_decay, k_decay, chunk_decay,
      gn_w.reshape(1, RET_WIDTH), gn_b.reshape(1, RET_WIDTH))


def _split2(x):
    hi = x.astype(BF16)
    lo = (x - hi.astype(F32)).astype(BF16)
    return hi, lo


def _split3(x):
    hi = x.astype(BF16)
    r = x - hi.astype(F32)
    mid = r.astype(BF16)
    lo = (r - mid.astype(F32)).astype(BF16)
    return hi, mid, lo


def _group_sum(x, gmat):
    hi, lo = _split2(x)
    return (jnp.dot(hi, gmat, preferred_element_type=F32) + jnp.dot(lo, gmat, preferred_element_type=F32))


def _bmm(a, b):
    return lax.dot_general(a.astype(BF16), b.astype(BF16), (((2,), (1,)), ((0,), (0,))),
                           preferred_element_type=F32)


def _bmm_nt(a, b):
    return lax.dot_general(a.astype(BF16), b.astype(BF16), (((2,), (2,)), ((0,), (0,))),
                           preferred_element_type=F32)


def _bmm_tn(a, b):
    return lax.dot_general(a.astype(BF16), b.astype(BF16), (((1,), (1,)), ((0,), (0,))),
                           preferred_element_type=F32)


RWKV_HEAD_GROUP = 8


def _rwkv_pre_kernel(r_ref, k_ref, v_ref, l_ref, pr_ref, pk_ref, pv_ref, plr_ref,
                     mixr_ref, mixk_ref, mixv_ref, mixl_ref, w0_ref, a0_ref, kk_ref, ka_ref, rk_ref,
                     wup_ref, aup_ref, gup_ref, gmat_ref, ltri_ref,
                     m_out, n_out, q_out, y_out, g_out, bv_out,
                     at_s, rt_s, bh_s, kh_s, bt_s, kt_s, v_s, gc_s, *, tiles_per_batch):
    tb = r_ref.shape[0]
    nchunk = tb // CHUNK
    first = (pl.program_id(0) % tiles_per_batch) == 0
    rowid = lax.broadcasted_iota(jnp.int32, (tb, 1), 0)

    def shift_lerp(cur_ref, prev_ref, mix_ref):
        cur = cur_ref[...]
        prev_row = jnp.where(first, 0.0, prev_ref[7:8, :])
        prev = jnp.where(rowid == 0, prev_row, pltpu.roll(cur, 1, axis=0))
        return cur + (prev - cur) * mix_ref[...]

    r = shift_lerp(r_ref, pr_ref, mixr_ref)
    k = shift_lerp(k_ref, pk_ref, mixk_ref)
    v = shift_lerp(v_ref, pv_ref, mixv_ref)
    lo = shift_lerp(l_ref, plr_ref, mixl_ref)
    xw = lo[:, :DECAY_LORA]
    xa = lo[:, DECAY_LORA:DECAY_LORA + AAA_LORA]
    xg = lo[:, DECAY_LORA + AAA_LORA:]
    wl = jnp.dot(jnp.tanh(xw).astype(BF16), wup_ref[...], preferred_element_type=F32)
    u = -(w0_ref[...] + wl)
    softplus = jnp.maximum(u, 0.0) + jnp.log(1.0 + jnp.exp(-jnp.abs(u)))
    logw = -jnp.exp(-softplus - 0.5)
    a = jax.nn.sigmoid(a0_ref[...] + jnp.dot(xa.astype(BF16), aup_ref[...], preferred_element_type=F32))
    g_out[...] = jnp.dot(jax.nn.sigmoid(xg).astype(BF16), gup_ref[...], preferred_element_type=F32)
    gmat = gmat_ref[...]
    kk = k * kk_ref[...]
    kk = kk / jnp.maximum(jnp.sqrt(_group_sum(kk * kk, gmat)), 1e-12)
    k2 = k * (1.0 + (a - 1.0) * ka_ref[...])
    bv_out[...] = _group_sum(r * k2 * rk_ref[...], gmat) * v
    beta = kk * a
    ltri = ltri_ref[...]

    def to_heads(dst, c, val):
        for h in range(RWKV_HEADS):
            dst[h, pl.ds(c * CHUNK, CHUNK), :] = val[:, h * RWKV_HEAD:(h + 1) * RWKV_HEAD]

    for c in range(nchunk):
        sl = slice(c * CHUNK, (c + 1) * CHUNK)
        lw = logw[sl]
        h1, h2, h3 = _split3(lw)
        cs = (jnp.dot(ltri, h1, preferred_element_type=F32) + jnp.dot(ltri, h2, preferred_element_type=F32)
              + jnp.dot(ltri, h3, preferred_element_type=F32))
        cs_last = cs[CHUNK - 1:CHUNK, :]
        g_t = jnp.exp(cs)
        g_tm1 = jnp.exp(cs - lw)
        inv_g = jnp.exp(-cs)
        g_end = jnp.exp(cs_last - cs)
        to_heads(at_s, c, -kk[sl] * g_tm1)
        to_heads(rt_s, c, r[sl] * g_t)
        to_heads(bh_s, c, beta[sl] * inv_g)
        to_heads(kh_s, c, k2[sl] * inv_g)
        to_heads(bt_s, c, beta[sl] * g_end)
        to_heads(kt_s, c, k2[sl] * g_end)
        to_heads(v_s, c, v[sl])
        gc = jnp.exp(cs_last)
        for h in range(RWKV_HEADS):
            gc_s[h, pl.ds(c * 8, 8), :] = jnp.broadcast_to(gc[:, h * RWKV_HEAD:(h + 1) * RWKV_HEAD], (8, RWKV_HEAD))

    G = RWKV_HEAD_GROUP
    ti = lax.broadcasted_iota(jnp.int32, (CHUNK, CHUNK), 0)
    si = lax.broadcasted_iota(jnp.int32, (CHUNK, CHUNK), 1)
    strict = (si < ti)[None]
    incl = (si <= ti)[None]
    eye = (si == ti).astype(F32)[None]

    def body(idx, carry):
        c = idx // (RWKV_HEADS // G)
        hg = idx % (RWKV_HEADS // G)
        hs = pl.ds(pl.multiple_of(hg * G, G), G)
        ts = pl.ds(pl.multiple_of(c * CHUNK, CHUNK), CHUNK)
        at = at_s[hs, ts, :]
        rt = rt_s[hs, ts, :]
        bh = bh_s[hs, ts, :]
        kh = kh_s[hs, ts, :]
        bt = bt_s[hs, ts, :]
        kt = kt_s[hs, ts, :]
        vv = v_s[hs, ts, :]
        gc = gc_s[hs, pl.ds(pl.multiple_of(c * 8, 8), 1), :]
        aab = jnp.where(strict, _bmm_nt(at, bh), 0.0)
        aak = jnp.where(strict, _bmm_nt(at, kh), 0.0)
        aqb = jnp.where(incl, _bmm_nt(rt, bh), 0.0)
        aqk = jnp.where(incl, _bmm_nt(rt, kh), 0.0)
        p = aab
        tm = eye + aab
        for _ in range(5):
            p = _bmm(p, p)
            tm = tm + _bmm(p, tm)
        akv = _bmm(aak, vv)
        wt = _bmm(tm, at)
        u0 = _bmm(tm, akv)
        q_out[hs, ts, :] = rt + _bmm(aqb, wt)
        y_out[hs, ts, :] = _bmm(aqb, u0) + _bmm(aqk, vv)
        m_out[hs, ts, :] = _bmm_tn(bt, wt) + eye * gc
        n_out[hs, ts, :] = _bmm_tn(bt, u0) + _bmm_tn(kt, vv)
        return carry

    lax.fori_loop(0, nchunk * (RWKV_HEADS // G), body, 0)


def _rwkv_consts():
    lane = np.arange(RWKV_WIDTH) // RWKV_HEAD
    gmat = (lane[:, None] == lane[None, :]).astype(np.float32)
    ltri = np.tril(np.ones((CHUNK, CHUNK), np.float32))
    return jnp.asarray(gmat, BF16), jnp.asarray(ltri, BF16)


def rwkv_pre(zp, mix, w0, w_up, a0, a_up, g_up, k_k, k_a, r_k, batch, t, tb=256):
    n = zp.shape[0]
    W = RWKV_WIDTH
    nt = t // tb
    gmat, ltri = _rwkv_consts()
    row = lambda p: p.reshape(1, -1)
    mix_r, mix_k, mix_v, mix_l = mix[:W], mix[W:2 * W], mix[2 * W:3 * W], mix[3 * W:]
    cb = ZP_RWKV // W
    cur = lambda j: pl.BlockSpec((tb, W), lambda i: (i, cb + j))
    prev = lambda j: pl.BlockSpec((8, W), lambda i: (jnp.maximum(i * (tb // 8) - 1, 0), cb + j))
    full = lambda shape: pl.BlockSpec(shape, lambda i: (0,) * len(shape))
    hm = pl.BlockSpec((RWKV_HEADS, tb, RWKV_HEAD), lambda i: (0, i, 0))
    nat = pl.BlockSpec((tb, W), lambda i: (i, 0))
    hm_shape = jax.ShapeDtypeStruct((RWKV_HEADS, n, RWKV_HEAD), F32)
    nat_shape = jax.ShapeDtypeStruct((n, W), F32)
    hscr = pltpu.VMEM((RWKV_HEADS, tb, RWKV_HEAD), F32)
    return pl.pallas_call(
        functools.partial(_rwkv_pre_kernel, tiles_per_batch=nt),
        grid=(n // tb,),
        in_specs=[
            cur(0), cur(1), cur(2),
            pl.BlockSpec((tb, LORA_COLS), lambda i: (i, ZP_LORA // LORA_COLS)),
            prev(0), prev(1), prev(2),
            pl.BlockSpec((8, LORA_COLS), lambda i: (jnp.maximum(i * (tb // 8) - 1, 0), ZP_LORA // LORA_COLS)),
            full((1, W)), full((1, W)), full((1, W)), full((1, LORA_COLS)),
            full((1, W)), full((1, W)), full((1, W)), full((1, W)), full((1, W)),
            full((DECAY_LORA, W)), full((AAA_LORA, W)), full((GATE_LORA, W)),
            full((W, W)), full((CHUNK, CHUNK)),
        ],
        out_specs=[hm, hm, hm, hm, nat, nat],
        out_shape=[hm_shape, hm_shape, hm_shape, hm_shape, nat_shape, nat_shape],
        scratch_shapes=[hscr] * 7 + [pltpu.VMEM((RWKV_HEADS, 8 * (tb // CHUNK), RWKV_HEAD), F32)],
        compiler_params=_cparams(("arbitrary",)),
        name="rwkv_pre",
    )(zp, zp, zp, zp, zp, zp, zp, zp,
      row(mix_r), row(mix_k), row(mix_v), row(mix_l), row(w0), row(a0), row(k_k), row(k_a), row(r_k),
      w_up.astype(BF16), a_up.astype(BF16), g_up.astype(BF16), gmat, ltri)


def _rwkv_scan_kernel(m_ref, n_ref, q_ref, y_ref, g_ref, bv_ref, lnw_ref, lnb_ref, gmat_ref, o_ref, s_ref, ys_ref):
    tb = g_ref.shape[0]

    @pl.when(pl.program_id(1) == 0)
    def _():
        s_ref[...] = jnp.zeros_like(s_ref)

    for c in range(tb // CHUNK):
        ts = pl.ds(c * CHUNK, CHUNK)
        s = s_ref[...]
        ys_ref[:, ts, :] = _bmm(q_ref[:, ts, :], s) + y_ref[:, ts, :]
        s_ref[...] = _bmm(m_ref[:, ts, :], s) + n_ref[:, ts, :]
    y = jnp.concatenate([ys_ref[h] for h in range(RWKV_HEADS)], axis=-1)
    gmat = gmat_ref[...]
    mu = _group_sum(y, gmat) * (1.0 / RWKV_HEAD)
    yc = y - mu
    var = _group_sum(yc * yc, gmat) * (1.0 / RWKV_HEAD)
    yn = yc * lax.rsqrt(var + RWKV_GN_EPS) * lnw_ref[...] + lnb_ref[...]
    o_ref[...] = (yn + bv_ref[...]) * g_ref[...]


def rwkv_scan(m, nn, qt, y1, g, bv, ln_w, ln_b, batch, t, tb=256):
    n = g.shape[0]
    W = RWKV_WIDTH
    nt = t // tb
    gmat, _ = _rwkv_consts()
    hm = pl.BlockSpec((RWKV_HEADS, tb, RWKV_HEAD), lambda b, i: (0, b * nt + i, 0))
    nat = pl.BlockSpec((tb, W), lambda b, i: (b * nt + i, 0))
    full = lambda shape: pl.BlockSpec(shape, lambda b, i: (0,) * len(shape))
    return pl.pallas_call(
        _rwkv_scan_kernel,
        grid=(batch, nt),
        in_specs=[hm, hm, hm, hm, nat, nat, full((1, W)), full((1, W)), full((W, W))],
        out_specs=nat,
        out_shape=jax.ShapeDtypeStruct((n, W), F32),
        scratch_shapes=[pltpu.VMEM((RWKV_HEADS, RWKV_HEAD, RWKV_HEAD), F32),
                        pltpu.VMEM((RWKV_HEADS, tb, RWKV_HEAD), F32)],
        compiler_params=_cparams(("arbitrary", "arbitrary")),
        name="rwkv_scan",
    )(m, nn, qt, y1, g, bv, ln_w.reshape(1, W), ln_b.reshape(1, W), gmat)


def _merge_kernel(x_ref, gr_ref, gw_ref, oret_ref, orwkv_ref, wr_ref, ww_ref, wo_ref, o_ref):
    pr = jnp.dot(oret_ref[...].astype(BF16), wr_ref[...], preferred_element_type=F32)
    pw = jnp.dot(orwkv_ref[...].astype(BF16), ww_ref[...], preferred_element_type=F32)
    merged = jax.nn.sigmoid(gr_ref[...]) * pr + jax.nn.sigmoid(gw_ref[...]) * pw
    o_ref[...] = x_ref[...] + jnp.dot(merged.astype(BF16), wo_ref[...], preferred_element_type=F32)


def merge_out(x, zp, o_ret, o_rwkv, w_ret, w_rwkv, w_out, tm=256):
    n, d = x.shape
    const = lambda shape: pl.BlockSpec(shape, lambda i: (0, 0), pipeline_mode=pl.Buffered(1))
    return pl.pallas_call(
        _merge_kernel,
        grid=(n // tm,),
        in_specs=[
            pl.BlockSpec((tm, d), lambda i: (i, 0)),
            pl.BlockSpec((tm, d), lambda i: (i, ZP_GATE_RET // D_MODEL)),
            pl.BlockSpec((tm, d), lambda i: (i, ZP_GATE_RWKV // D_MODEL)),
            pl.BlockSpec((tm, RET_WIDTH), lambda i: (i, 0)),
            pl.BlockSpec((tm, RWKV_WIDTH), lambda i: (i, 0)),
            const((RET_WIDTH, d)), const((RWKV_WIDTH, d)), const((d, d)),
        ],
        out_specs=pl.BlockSpec((tm, d), lambda i: (i, 0)),
        out_shape=jax.ShapeDtypeStruct((n, d), F32),
        compiler_params=_cparams(("arbitrary",)),
        name="merge_out",
    )(x, zp, zp, o_ret, o_rwkv, w_ret.astype(BF16), w_rwkv.astype(BF16), w_out.astype(BF16))


PEER_HEADS = 8
PEER_NKEYS = 128
PEER_NEXPERTS = PEER_NKEYS * PEER_NKEYS
PEER_DQ = 256
PEER_TOPK = 16


def _peer_query_kernel(x_ref, nw_ref, wq_ref, k1_ref, k2_ref, s_ref, ht_ref):
    x = x_ref[...]
    ms = jnp.mean(x * x, axis=-1, keepdims=True)
    h2 = x * lax.rsqrt(ms + NORM_EPS) * nw_ref[...]
    ht_ref[...] = h2.T.astype(BF16)
    q = jnp.dot(h2.astype(BF16), wq_ref[...], preferred_element_type=F32).astype(BF16)
    half = PEER_DQ // 2
    for h in range(PEER_HEADS):
        for p, kref in enumerate((k1_ref, k2_ref)):
            qh = q[:, h * PEER_DQ + p * half: h * PEER_DQ + (p + 1) * half]
            s_ref[p, h] = lax.dot_general(kref[...], qh, (((1,), (1,)), ((), ())), preferred_element_type=F32)


def peer_query(x1, norm_w, w_q, keys_1, keys_2, tm=256):
    n, d = x1.shape
    const = lambda shape: pl.BlockSpec(shape, lambda i: (0, 0), pipeline_mode=pl.Buffered(1))
    return pl.pallas_call(
        _peer_query_kernel,
        grid=(n // tm,),
        in_specs=[
            pl.BlockSpec((tm, d), lambda i: (i, 0)),
            const((1, d)), const((d, PEER_HEADS * PEER_DQ)),
            const((PEER_NKEYS, PEER_DQ // 2)), const((PEER_NKEYS, PEER_DQ // 2)),
        ],
        out_specs=[
            pl.BlockSpec((2, PEER_HEADS, PEER_NKEYS, tm), lambda i: (0, 0, 0, i)),
            pl.BlockSpec((d, tm), lambda i: (0, i)),
        ],
        out_shape=[
            jax.ShapeDtypeStruct((2, PEER_HEADS, PEER_NKEYS, n), F32),
            jax.ShapeDtypeStruct((d, n), BF16),
        ],
        compiler_params=_cparams(("arbitrary",)),
        name="peer_query",
    )(x1, norm_w.reshape(1, d), w_q.astype(BF16), keys_1.astype(BF16), keys_2.astype(BF16))


_CAND_GROUPS = [(0, 0), (0, 8)] + [(a, 0) for a in range(1, 8)]


def _peer_topk_kernel(s_ref, bits_ref, pow2_ref, e1_ref, e2_ref):
    tn = s_ref.shape[-1]
    K = PEER_TOPK
    neg = -jnp.inf
    rowid = lax.broadcasted_iota(jnp.int32, (PEER_NKEYS, tn), 0)
    iota_k = lax.broadcasted_iota(jnp.int32, (K, tn), 0)
    nrow = 8 * (len(_CAND_GROUPS) + 1)
    r = lax.broadcasted_iota(jnp.int32, (nrow, tn), 0)
    grp, sub = r // 8, r % 8
    ca = jnp.where(grp < 2, 0, jnp.where(grp < 9, grp - 1, 8 + sub))
    cb = jnp.where(grp == 1, 8 + sub, jnp.where(grp < 9, sub, 0))
    flat = ca * K + cb
    valid = (ca + 1) * (cb + 1) <= K

    def top16(s):
        rank = jnp.full((PEER_NKEYS, tn), K, jnp.int32)
        vals = []
        for a in range(K):
            m = jnp.max(s, axis=0, keepdims=True)
            idx = jnp.min(jnp.where(s == m, rowid, PEER_NKEYS), axis=0, keepdims=True)
            hit = rowid == idx
            rank = jnp.where(hit, a, rank)
            s = jnp.where(hit, neg, s)
            vals.append(m)
        return jnp.concatenate(vals, axis=0), rank

    def head(h, carry):
        s1 = s_ref[0, h]
        s2 = s_ref[1, h]
        v1, rank1 = top16(s1)
        v2, rank2 = top16(s2)
        pieces = [v1[a:a + 1] + v2[b0:b0 + 8] for a, b0 in _CAND_GROUPS] + [v1[8:16] + v2[0:1]]
        cand = jnp.where(valid, jnp.concatenate(pieces, axis=0), neg)
        rowbits = jnp.zeros((K, tn), jnp.int32)
        z = jnp.zeros((1, tn), F32)
        m0 = v1[0:1] + v2[0:1]
        for _ in range(K):
            m = jnp.max(cand, axis=0, keepdims=True)
            f = jnp.min(jnp.where(cand == m, flat, K * K), axis=0, keepdims=True)
            cand = jnp.where(flat == f, neg, cand)
            rowbits = rowbits | jnp.where(iota_k == (f >> 4), jnp.left_shift(1, f & (K - 1)), 0)
            z = z + jnp.exp(m - m0)
        inv_z = 1.0 / z
        sel1 = rank1 < K
        sel2 = rank2 < K
        e1_ref[h] = jnp.where(sel1, jnp.exp(s1 - v1[0:1]) * inv_z, 0.0)
        e2_ref[h] = jnp.where(sel2, jnp.exp(s2 - v2[0:1]), 0.0)
        pow2_ref[h] = jnp.where(sel2, jnp.left_shift(1, jnp.minimum(rank2, K - 1)), 0)
        bits = jnp.zeros((PEER_NKEYS, tn), jnp.int32)
        for a in range(K):
            bits = jnp.where(rank1 == a, rowbits[a:a + 1], bits)
        bits_ref[h] = bits
        return carry

    lax.fori_loop(0, PEER_HEADS, head, 0)


def peer_topk(s_t, tn=256):
    n = s_t.shape[-1]
    spec = pl.BlockSpec((PEER_HEADS, PEER_NKEYS, tn), lambda i: (0, 0, i))
    shp = lambda dt: jax.ShapeDtypeStruct((PEER_HEADS, PEER_NKEYS, n), dt)
    return pl.pallas_call(
        _peer_topk_kernel,
        grid=(n // tn,),
        in_specs=[pl.BlockSpec((2, PEER_HEADS, PEER_NKEYS, tn), lambda i: (0, 0, 0, i))],
        out_specs=[spec, spec, spec, spec],
        out_shape=[shp(jnp.int32), shp(jnp.int32), shp(F32), shp(F32)],
        compiler_params=_cparams(("arbitrary",)),
        name="peer_topk",
    )(s_t)


def _peer_ffn_kernel(x_ref, ht_ref, bits_ref, pow2_ref, e1_ref, e2_ref, u_ref, vt_ref, fnw_ref, o_ref,
                     acc_ref, act_ref):
    j = pl.program_id(1)
    eb = u_ref.shape[0]
    nsub = eb // PEER_NKEYS

    @pl.when(j == 0)
    def _():
        acc_ref[...] = jnp.zeros_like(acc_ref)

    pre = jnp.dot(u_ref[...], ht_ref[...], preferred_element_type=F32)
    for jj in range(nsub):
        e1_idx = j * nsub + jj
        gate = None
        for h in range(PEER_HEADS):
            brow = bits_ref[h, pl.ds(e1_idx, 1), :]
            erow = e1_ref[h, pl.ds(e1_idx, 1), :]
            term = jnp.where((brow & pow2_ref[h]) != 0, e2_ref[h], 0.0) * erow
            gate = term if gate is None else gate + term
        p = pre[jj * PEER_NKEYS:(jj + 1) * PEER_NKEYS]
        gelu = 0.5 * p * (1.0 + lax.erf(p * (2.0 ** -0.5)))
        act_ref[pl.ds(jj * PEER_NKEYS, PEER_NKEYS), :] = (gelu * gate).astype(BF16)
    acc_ref[...] += jnp.dot(vt_ref[...], act_ref[...], preferred_element_type=F32)

    @pl.when(j == pl.num_programs(1) - 1)
    def _():
        y = x_ref[...] + acc_ref[...].T
        ms = jnp.mean(y * y, axis=-1, keepdims=True)
        o_ref[...] = y * lax.rsqrt(ms + NORM_EPS) * fnw_ref[...]


def peer_ffn(x1, h2t, bits, pow2, e1, e2, expert_u, expert_v, final_norm_w, tn=512, eb=512):
    n, d = x1.shape
    ne = expert_u.shape[0]
    u = expert_u.astype(BF16)
    vt = expert_v.T.astype(BF16)
    tok = pl.BlockSpec((PEER_HEADS, PEER_NKEYS, tn), lambda i, j: (0, 0, i))
    return pl.pallas_call(
        _peer_ffn_kernel,
        grid=(n // tn, ne // eb),
        in_specs=[
            pl.BlockSpec((tn, d), lambda i, j: (i, 0)),
            pl.BlockSpec((d, tn), lambda i, j: (0, i)),
            tok, tok, tok, tok,
            pl.BlockSpec((eb, d), lambda i, j: (j, 0)),
            pl.BlockSpec((d, eb), lambda i, j: (0, j)),
            pl.BlockSpec((1, d), lambda i, j: (0, 0)),
        ],
        out_specs=pl.BlockSpec((tn, d), lambda i, j: (i, 0)),
        out_shape=jax.ShapeDtypeStruct((n, d), F32),
        scratch_shapes=[pltpu.VMEM((d, tn), F32), pltpu.VMEM((eb, tn), BF16)],
        compiler_params=_cparams(("arbitrary", "arbitrary")),
        name="peer_ffn",
    )(x1, h2t, bits, pow2, e1, e2, u, vt, final_norm_w.reshape(1, d))


def _permute_w_in(w_in):
    q_end = 2 * RET_QK + 2 * RET_WIDTH
    rw_end = q_end + 3 * RWKV_WIDTH
    lora_end = q_end + RWKV_COLS
    return jnp.concatenate(
        [w_in[:, :q_end], w_in[:, lora_end:], w_in[:, q_end:rw_end], w_in[:, rw_end:lora_end]], axis=1)


def kernel(x, norm1_w, w_in, ret_gn_w, ret_gn_b, rwkv_mix, rwkv_w0, rwkv_w_up, rwkv_a0, rwkv_a_up, rwkv_g_up, rwkv_k_k, rwkv_k_a, rwkv_r_k, rwkv_ln_w, rwkv_ln_b, w_ret_branch, w_rwkv_branch, w_out, norm2_w, peer_w_q, peer_keys_1, peer_keys_2, peer_u, peer_v, final_norm_w):
    b, t, d = x.shape
    n = b * t
    assert w_in.shape[0] == 1, "single-layer block: the final norm is fused into the PEER kernel"
    l = 0
    xf = x.reshape(n, d)
    zp = in_proj(xf, norm1_w[l], _permute_w_in(w_in[l]).astype(BF16))
    o_ret = retention(zp, ret_gn_w[l], ret_gn_b[l], b, t)
    pre = rwkv_pre(zp, rwkv_mix[l], rwkv_w0[l], rwkv_w_up[l], rwkv_a0[l], rwkv_a_up[l], rwkv_g_up[l],
                   rwkv_k_k[l], rwkv_k_a[l], rwkv_r_k[l], b, t)
    o_rwkv = rwkv_scan(*pre, rwkv_ln_w[l], rwkv_ln_b[l], b, t)
    x1 = merge_out(xf, zp, o_ret, o_rwkv, w_ret_branch[l], w_rwkv_branch[l], w_out[l])
    s_t, h2t = peer_query(x1, norm2_w[l], peer_w_q[l], peer_keys_1[l], peer_keys_2[l])
    bits, pow2, e1, e2 = peer_topk(s_t)
    out = peer_ffn(x1, h2t, bits, pow2, e1, e2, peer_u[l], peer_v[l], final_norm_w)
    return out.reshape(b, t, d)
```

```python
import functools

import numpy as np
import jax
import jax.numpy as jnp
from jax import lax
from jax.experimental import pallas as pl
from jax.experimental.pallas import tpu as pltpu

F32 = jnp.float32
BF16 = jnp.bfloat16

D_MODEL = 2048
CHUNK = 64
NORM_EPS = 1e-6
RET_HEADS = 4
RET_DK = 256
RET_DV = 256
RET_QK = RET_HEADS * RET_DK
RET_WIDTH = RET_HEADS * RET_DV
ROPE_BASE = 10000.0
RWKV_HEAD = 64
RWKV_WIDTH = D_MODEL // 2
RWKV_HEADS = RWKV_WIDTH // RWKV_HEAD
DECAY_LORA = 64
AAA_LORA = 64
GATE_LORA = 128
LORA_COLS = DECAY_LORA + AAA_LORA + GATE_LORA
RWKV_COLS = 3 * RWKV_WIDTH + LORA_COLS
RWKV_GN_EPS = 64e-5
IN_COLS = 2 * RET_QK + 2 * RET_WIDTH + RWKV_COLS + 2 * D_MODEL

ZP_GATE_RET = 4096
ZP_GATE_RWKV = 6144
ZP_RWKV = 8192
ZP_LORA = 11264

VMEM_LIMIT = 56 * 1024 * 1024


def _cparams(sem):
    return pltpu.CompilerParams(dimension_semantics=sem, vmem_limit_bytes=VMEM_LIMIT)


def _in_proj_kernel(x_ref, nw_ref, w_ref, o_ref, h_ref):
    @pl.when(pl.program_id(1) == 0)
    def _():
        x = x_ref[...]
        ms = jnp.mean(x * x, axis=-1, keepdims=True)
        h_ref[...] = (x * lax.rsqrt(ms + NORM_EPS) * nw_ref[...]).astype(BF16)

    o_ref[...] = jnp.dot(h_ref[...], w_ref[...], preferred_element_type=F32)


def in_proj(x, norm_w, w_bf16, tm=512, tn=1280):
    n, d = x.shape
    cols = w_bf16.shape[1]
    return pl.pallas_call(
        _in_proj_kernel,
        grid=(n // tm, cols // tn),
        in_specs=[
            pl.BlockSpec((tm, d), lambda i, j: (i, 0)),
            pl.BlockSpec((1, d), lambda i, j: (0, 0)),
            pl.BlockSpec((d, tn), lambda i, j: (0, j)),
        ],
        out_specs=pl.BlockSpec((tm, tn), lambda i, j: (i, j)),
        out_shape=jax.ShapeDtypeStruct((n, cols), F32),
        scratch_shapes=[pltpu.VMEM((tm, d), BF16)],
        compiler_params=_cparams(("arbitrary", "arbitrary")),
        name="in_proj",
    )(x, norm_w.reshape(1, d), w_bf16)


def _retention_kernel(q_ref, k_ref, v_ref, g_ref, cos_ref, sin_ref, dintra_ref, qdec_ref, kdec_ref,
                      cdec_ref, gnw_ref, gnb_ref, o_ref, s_ref, qr_ref, kr_ref, acc_ref):
    tb = q_ref.shape[0]
    half = RET_DK // 2

    @pl.when(pl.program_id(2) == 0)
    def _():
        s_ref[...] = jnp.zeros_like(s_ref)

    cos = cos_ref[...]
    sin = sin_ref[...]

    def rot(z_ref):
        z1 = z_ref[:, :half]
        z2 = z_ref[:, half:]
        return jnp.concatenate([z1 * cos - z2 * sin, z2 * cos + z1 * sin], axis=-1)

    qr_ref[...] = rot(q_ref)
    kr_ref[...] = rot(k_ref) * (RET_DK ** -0.5)
    d_intra = dintra_ref[0]
    q_decay = qdec_ref[0]
    k_decay = kdec_ref[0]
    c_decay = cdec_ref[0]
    for c in range(tb // CHUNK):
        sl = pl.ds(c * CHUNK, CHUNK)
        qc = qr_ref[sl, :]
        kc = kr_ref[sl, :]
        vc = v_ref[sl, :].astype(BF16)
        scores = lax.dot_general(qc.astype(BF16), kc.astype(BF16), (((1,), (1,)), ((), ())),
                                 preferred_element_type=F32) * d_intra
        s_prev = s_ref[...]
        o = jnp.dot(scores.astype(BF16), vc, preferred_element_type=F32)
        o = o + jnp.dot((qc * q_decay).astype(BF16), s_prev.astype(BF16), preferred_element_type=F32)
        kd_t = (kc * k_decay).T.astype(BF16)
        s_ref[...] = s_prev * c_decay + jnp.dot(kd_t, vc, preferred_element_type=F32)
        acc_ref[sl, :] = o
    o = acc_ref[...]
    mu = jnp.mean(o, axis=-1, keepdims=True)
    oc = o - mu
    var = jnp.mean(oc * oc, axis=-1, keepdims=True)
    y = oc * lax.rsqrt(var + NORM_EPS) * gnw_ref[...] + gnb_ref[...]
    g = g_ref[...]
    o_ref[...] = y * (g * jax.nn.sigmoid(g))


def _retention_tables(t):
    h = RET_HEADS
    log_g = jnp.log(1.0 - jnp.exp2(-5.0 - jnp.arange(h, dtype=F32)))
    n = jnp.arange(CHUNK, dtype=F32)
    d_intra = jnp.exp(log_g[:, None, None] * jnp.abs(n[:, None] - n[None, :]))
    q_decay = jnp.exp(log_g[:, None] * (n[None, :] + 1.0))
    k_decay = jnp.exp(log_g[:, None] * (CHUNK - 1.0 - n[None, :]))
    chunk_decay = jnp.exp(log_g * CHUNK)
    q_decay = jnp.broadcast_to(q_decay[:, :, None], (h, CHUNK, RET_DK))
    k_decay = jnp.broadcast_to(k_decay[:, :, None], (h, CHUNK, RET_DK))
    chunk_decay = jnp.broadcast_to(chunk_decay[:, None, None], (h, 1, RET_DV))
    half = RET_DK // 2
    inv_freq = ROPE_BASE ** (-jnp.arange(half, dtype=F32) * 2.0 / RET_DK)
    ang = jnp.arange(t, dtype=jnp.int32).astype(F32)[:, None] * inv_freq[None, :]
    return d_intra, q_decay, k_decay, chunk_decay, jnp.cos(ang), jnp.sin(ang)


def retention(zp, gn_w, gn_b, batch, t, tb=512):
    n = zp.shape[0]
    h = RET_HEADS
    nt = t // tb
    d_intra, q_decay, k_decay, chunk_decay, cos, sin = _retention_tables(t)

    def zspec(col0):
        return pl.BlockSpec((tb, RET_DK), lambda b, hh, i: (b * nt + i, col0 + hh))

    tab = lambda shape: pl.BlockSpec((1,) + shape, lambda b, hh, i: (hh, 0, 0))
    return pl.pallas_call(
        _retention_kernel,
        grid=(batch, h, nt),
        in_specs=[
            zspec(0), zspec(h), zspec(2 * h), zspec(3 * h),
            pl.BlockSpec((tb, RET_DK // 2), lambda b, hh, i: (i, 0)),
            pl.BlockSpec((tb, RET_DK // 2), lambda b, hh, i: (i, 0)),
            tab((CHUNK, CHUNK)), tab((CHUNK, RET_DK)), tab((CHUNK, RET_DK)), tab((1, RET_DV)),
            pl.BlockSpec((1, RET_DV), lambda b, hh, i: (0, hh)),
            pl.BlockSpec((1, RET_DV), lambda b, hh, i: (0, hh)),
        ],
        out_specs=pl.BlockSpec((tb, RET_DV), lambda b, hh, i: (b * nt + i, hh)),
        out_shape=jax.ShapeDtypeStruct((n, RET_WIDTH), F32),
        scratch_shapes=[
            pltpu.VMEM((RET_DK, RET_DV), F32),
            pltpu.VMEM((tb, RET_DK), F32),
            pltpu.VMEM((tb, RET_DK), F32),
            pltpu.VMEM((tb, RET_DV), F32),
        ],
        compiler_params=_cparams(("arbitrary", "arbitrary", "arbitrary")),
        name="retention",
    )(zp, zp, zp, zp, cos, sin, d_intra, q_decay, k_decay, chunk_decay,
      gn_w.reshape(1, RET_WIDTH), gn_b.reshape(1, RET_WIDTH))


def _split2(x):
    hi = x.astype(BF16)
    lo = (x - hi.astype(F32)).astype(BF16)
    return hi, lo


def _split3(x):
    hi = x.astype(BF16)
    r = x - hi.astype(F32)
    mid = r.astype(BF16)
    lo = (r - mid.astype(F32)).astype(BF16)
    return hi, mid, lo


def _group_sum(x, gmat):
    hi, lo = _split2(x)
    return (jnp.dot(hi, gmat, preferred_element_type=F32) + jnp.dot(lo, gmat, preferred_element_type=F32))


def _bmm(a, b):
    return lax.dot_general(a.astype(BF16), b.astype(BF16), (((2,), (1,)), ((0,), (0,))),
                           preferred_element_type=F32)


def _bmm_nt(a, b):
    return lax.dot_general(a.astype(BF16), b.astype(BF16), (((2,), (2,)), ((0,), (0,))),
                           preferred_element_type=F32)


def _bmm_tn(a, b):
    return lax.dot_general(a.astype(BF16), b.astype(BF16), (((1,), (1,)), ((0,), (0,))),
                           preferred_element_type=F32)


RWKV_HEAD_GROUP = 8


def _rwkv_pre_kernel(r_ref, k_ref, v_ref, l_ref, pr_ref, pk_ref, pv_ref, plr_ref,
                     mixr_ref, mixk_ref, mixv_ref, mixl_ref, w0_ref, a0_ref, kk_ref, ka_ref, rk_ref,
                     wup_ref, aup_ref, gup_ref, gmat_ref, ltri_ref,
                     m_out, n_out, q_out, y_out, g_out, bv_out,
                     at_s, rt_s, bh_s, kh_s, bt_s, kt_s, v_s, gc_s, *, tiles_per_batch):
    tb = r_ref.shape[0]
    nchunk = tb // CHUNK
    first = (pl.program_id(0) % tiles_per_batch) == 0
    rowid = lax.broadcasted_iota(jnp.int32, (tb, 1), 0)

    def shift_lerp(cur_ref, prev_ref, mix_ref):
        cur = cur_ref[...]
        prev_row = jnp.where(first, 0.0, prev_ref[7:8, :])
        prev = jnp.where(rowid == 0, prev_row, pltpu.roll(cur, 1, axis=0))
        return cur + (prev - cur) * mix_ref[...]

    r = shift_lerp(r_ref, pr_ref, mixr_ref)
    k = shift_lerp(k_ref, pk_ref, mixk_ref)
    v = shift_lerp(v_ref, pv_ref, mixv_ref)
    lo = shift_lerp(l_ref, plr_ref, mixl_ref)
    xw = lo[:, :DECAY_LORA]
    xa = lo[:, DECAY_LORA:DECAY_LORA + AAA_LORA]
    xg = lo[:, DECAY_LORA + AAA_LORA:]
    wl = jnp.dot(jnp.tanh(xw).astype(BF16), wup_ref[...], preferred_element_type=F32)
    u = -(w0_ref[...] + wl)
    softplus = jnp.maximum(u, 0.0) + jnp.log(1.0 + jnp.exp(-jnp.abs(u)))
    logw = -jnp.exp(-softplus - 0.5)
    a = jax.nn.sigmoid(a0_ref[...] + jnp.dot(xa.astype(BF16), aup_ref[...], preferred_element_type=F32))
    g_out[...] = jnp.dot(jax.nn.sigmoid(xg).astype(BF16), gup_ref[...], preferred_element_type=F32)
    gmat = gmat_ref[...]
    kk = k * kk_ref[...]
    kk = kk / jnp.maximum(jnp.sqrt(_group_sum(kk * kk, gmat)), 1e-12)
    k2 = k * (1.0 + (a - 1.0) * ka_ref[...])
    bv_out[...] = _group_sum(r * k2 * rk_ref[...], gmat) * v
    beta = kk * a
    ltri = ltri_ref[...]

    def to_heads(dst, c, val):
        for h in range(RWKV_HEADS):
            dst[h, pl.ds(c * CHUNK, CHUNK), :] = val[:, h * RWKV_HEAD:(h + 1) * RWKV_HEAD]

    for c in range(nchunk):
        sl = slice(c * CHUNK, (c + 1) * CHUNK)
        lw = logw[sl]
        h1, h2, h3 = _split3(lw)
        cs = (jnp.dot(ltri, h1, preferred_element_type=F32) + jnp.dot(ltri, h2, preferred_element_type=F32)
              + jnp.dot(ltri, h3, preferred_element_type=F32))
        cs_last = cs[CHUNK - 1:CHUNK, :]
        g_t = jnp.exp(cs)
        g_tm1 = jnp.exp(cs - lw)
        inv_g = jnp.exp(-cs)
        g_end = jnp.exp(cs_last - cs)
        to_heads(at_s, c, -kk[sl] * g_tm1)
        to_heads(rt_s, c, r[sl] * g_t)
        to_heads(bh_s, c, beta[sl] * inv_g)
        to_heads(kh_s, c, k2[sl] * inv_g)
        to_heads(bt_s, c, beta[sl] * g_end)
        to_heads(kt_s, c, k2[sl] * g_end)
        to_heads(v_s, c, v[sl])
        gc = jnp.exp(cs_last)
        for h in range(RWKV_HEADS):
            gc_s[h, pl.ds(c * 8, 8), :] = jnp.broadcast_to(gc[:, h * RWKV_HEAD:(h + 1) * RWKV_HEAD], (8, RWKV_HEAD))

    G = RWKV_HEAD_GROUP
    ti = lax.broadcasted_iota(jnp.int32, (CHUNK, CHUNK), 0)
    si = lax.broadcasted_iota(jnp.int32, (CHUNK, CHUNK), 1)
    strict = (si < ti)[None]
    incl = (si <= ti)[None]
    eye = (si == ti).astype(F32)[None]

    def body(idx, carry):
        c = idx // (RWKV_HEADS // G)
        hg = idx % (RWKV_HEADS // G)
        hs = pl.ds(pl.multiple_of(hg * G, G), G)
        ts = pl.ds(pl.multiple_of(c * CHUNK, CHUNK), CHUNK)
        at = at_s[hs, ts, :]
        rt = rt_s[hs, ts, :]
        bh = bh_s[hs, ts, :]
        kh = kh_s[hs, ts, :]
        bt = bt_s[hs, ts, :]
        kt = kt_s[hs, ts, :]
        vv = v_s[hs, ts, :]
        gc = gc_s[hs, pl.ds(pl.multiple_of(c * 8, 8), 1), :]
        aab = jnp.where(strict, _bmm_nt(at, bh), 0.0)
        aak = jnp.where(strict, _bmm_nt(at, kh), 0.0)
        aqb = jnp.where(incl, _bmm_nt(rt, bh), 0.0)
        aqk = jnp.where(incl, _bmm_nt(rt, kh), 0.0)
        p = aab
        tm = eye + aab
        for _ in range(5):
            p = _bmm(p, p)
            tm = tm + _bmm(p, tm)
        akv = _bmm(aak, vv)
        wt = _bmm(tm, at)
        u0 = _bmm(tm, akv)
        q_out[hs, ts, :] = rt + _bmm(aqb, wt)
        y_out[hs, ts, :] = _bmm(aqb, u0) + _bmm(aqk, vv)
        m_out[hs, ts, :] = _bmm_tn(bt, wt) + eye * gc
        n_out[hs, ts, :] = _bmm_tn(bt, u0) + _bmm_tn(kt, vv)
        return carry

    lax.fori_loop(0, nchunk * (RWKV_HEADS // G), body, 0)


def _rwkv_consts():
    lane = np.arange(RWKV_WIDTH) // RWKV_HEAD
    gmat = (lane[:, None] == lane[None, :]).astype(np.float32)
    ltri = np.tril(np.ones((CHUNK, CHUNK), np.float32))
    return jnp.asarray(gmat, BF16), jnp.asarray(ltri, BF16)


def rwkv_pre(zp, mix, w0, w_up, a0, a_up, g_up, k_k, k_a, r_k, batch, t, tb=256):
    n = zp.shape[0]
    W = RWKV_WIDTH
    nt = t // tb
    gmat, ltri = _rwkv_consts()
    row = lambda p: p.reshape(1, -1)
    mix_r, mix_k, mix_v, mix_l = mix[:W], mix[W:2 * W], mix[2 * W:3 * W], mix[3 * W:]
    cb = ZP_RWKV // W
    cur = lambda j: pl.BlockSpec((tb, W), lambda i: (i, cb + j))
    prev = lambda j: pl.BlockSpec((8, W), lambda i: (jnp.maximum(i * (tb // 8) - 1, 0), cb + j))
    full = lambda shape: pl.BlockSpec(shape, lambda i: (0,) * len(shape))
    hm = pl.BlockSpec((RWKV_HEADS, tb, RWKV_HEAD), lambda i: (0, i, 0))
    nat = pl.BlockSpec((tb, W), lambda i: (i, 0))
    hm_shape = jax.ShapeDtypeStruct((RWKV_HEADS, n, RWKV_HEAD), F32)
    nat_shape = jax.ShapeDtypeStruct((n, W), F32)
    hscr = pltpu.VMEM((RWKV_HEADS, tb, RWKV_HEAD), F32)
    return pl.pallas_call(
        functools.partial(_rwkv_pre_kernel, tiles_per_batch=nt),
        grid=(n // tb,),
        in_specs=[
            cur(0), cur(1), cur(2),
            pl.BlockSpec((tb, LORA_COLS), lambda i: (i, ZP_LORA // LORA_COLS)),
            prev(0), prev(1), prev(2),
            pl.BlockSpec((8, LORA_COLS), lambda i: (jnp.maximum(i * (tb // 8) - 1, 0), ZP_LORA // LORA_COLS)),
            full((1, W)), full((1, W)), full((1, W)), full((1, LORA_COLS)),
            full((1, W)), full((1, W)), full((1, W)), full((1, W)), full((1, W)),
            full((DECAY_LORA, W)), full((AAA_LORA, W)), full((GATE_LORA, W)),
            full((W, W)), full((CHUNK, CHUNK)),
        ],
        out_specs=[hm, hm, hm, hm, nat, nat],
        out_shape=[hm_shape, hm_shape, hm_shape, hm_shape, nat_shape, nat_shape],
        scratch_shapes=[hscr] * 7 + [pltpu.VMEM((RWKV_HEADS, 8 * (tb // CHUNK), RWKV_HEAD), F32)],
        compiler_params=_cparams(("arbitrary",)),
        name="rwkv_pre",
    )(zp, zp, zp, zp, zp, zp, zp, zp,
      row(mix_r), row(mix_k), row(mix_v), row(mix_l), row(w0), row(a0), row(k_k), row(k_a), row(r_k),
      w_up.astype(BF16), a_up.astype(BF16), g_up.astype(BF16), gmat, ltri)


def _rwkv_scan_kernel(m_ref, n_ref, q_ref, y_ref, g_ref, bv_ref, lnw_ref, lnb_ref, gmat_ref, o_ref, s_ref, ys_ref):
    tb = g_ref.shape[0]

    @pl.when(pl.program_id(1) == 0)
    def _():
        s_ref[...] = jnp.zeros_like(s_ref)

    for c in range(tb // CHUNK):
        ts = pl.ds(c * CHUNK, CHUNK)
        s = s_ref[...]
        ys_ref[:, ts, :] = _bmm(q_ref[:, ts, :], s) + y_ref[:, ts, :]
        s_ref[...] = _bmm(m_ref[:, ts, :], s) + n_ref[:, ts, :]
    y = jnp.concatenate([ys_ref[h] for h in range(RWKV_HEADS)], axis=-1)
    gmat = gmat_ref[...]
    mu = _group_sum(y, gmat) * (1.0 / RWKV_HEAD)
    yc = y - mu
    var = _group_sum(yc * yc, gmat) * (1.0 / RWKV_HEAD)
    yn = yc * lax.rsqrt(var + RWKV_GN_EPS) * lnw_ref[...] + lnb_ref[...]
    o_ref[...] = (yn + bv_ref[...]) * g_ref[...]


def rwkv_scan(m, nn, qt, y1, g, bv, ln_w, ln_b, batch, t, tb=256):
    n = g.shape[0]
    W = RWKV_WIDTH
    nt = t // tb
    gmat, _ = _rwkv_consts()
    hm = pl.BlockSpec((RWKV_HEADS, tb, RWKV_HEAD), lambda b, i: (0, b * nt + i, 0))
    nat = pl.BlockSpec((tb, W), lambda b, i: (b * nt + i, 0))
    full = lambda shape: pl.BlockSpec(shape, lambda b, i: (0,) * len(shape))
    return pl.pallas_call(
        _rwkv_scan_kernel,
        grid=(batch, nt),
        in_specs=[hm, hm, hm, hm, nat, nat, full((1, W)), full((1, W)), full((W, W))],
        out_specs=nat,
        out_shape=jax.ShapeDtypeStruct((n, W), F32),
        scratch_shapes=[pltpu.VMEM((RWKV_HEADS, RWKV_HEAD, RWKV_HEAD), F32),
                        pltpu.VMEM((RWKV_HEADS, tb, RWKV_HEAD), F32)],
        compiler_params=_cparams(("arbitrary", "arbitrary")),
        name="rwkv_scan",
    )(m, nn, qt, y1, g, bv, ln_w.reshape(1, W), ln_b.reshape(1, W), gmat)


def _merge_kernel(x_ref, gr_ref, gw_ref, oret_ref, orwkv_ref, wr_ref, ww_ref, wo_ref, o_ref):
    pr = jnp.dot(oret_ref[...].astype(BF16), wr_ref[...], preferred_element_type=F32)
    pw = jnp.dot(orwkv_ref[...].astype(BF16), ww_ref[...], preferred_element_type=F32)
    merged = jax.nn.sigmoid(gr_ref[...]) * pr + jax.nn.sigmoid(gw_ref[...]) * pw
    o_ref[...] = x_ref[...] + jnp.dot(merged.astype(BF16), wo_ref[...], preferred_element_type=F32)


def merge_out(x, zp, o_ret, o_rwkv, w_ret, w_rwkv, w_out, tm=256):
    n, d = x.shape
    const = lambda shape: pl.BlockSpec(shape, lambda i: (0, 0), pipeline_mode=pl.Buffered(1))
    return pl.pallas_call(
        _merge_kernel,
        grid=(n // tm,),
        in_specs=[
            pl.BlockSpec((tm, d), lambda i: (i, 0)),
            pl.BlockSpec((tm, d), lambda i: (i, ZP_GATE_RET // D_MODEL)),
            pl.BlockSpec((tm, d), lambda i: (i, ZP_GATE_RWKV // D_MODEL)),
            pl.BlockSpec((tm, RET_WIDTH), lambda i: (i, 0)),
            pl.BlockSpec((tm, RWKV_WIDTH), lambda i: (i, 0)),
            const((RET_WIDTH, d)), const((RWKV_WIDTH, d)), const((d, d)),
        ],
        out_specs=pl.BlockSpec((tm, d), lambda i: (i, 0)),
        out_shape=jax.ShapeDtypeStruct((n, d), F32),
        compiler_params=_cparams(("arbitrary",)),
        name="merge_out",
    )(x, zp, zp, o_ret, o_rwkv, w_ret.astype(BF16), w_rwkv.astype(BF16), w_out.astype(BF16))


PEER_HEADS = 8
PEER_NKEYS = 128
PEER_NEXPERTS = PEER_NKEYS * PEER_NKEYS
PEER_DQ = 256
PEER_TOPK = 16


def _peer_query_kernel(x_ref, nw_ref, wq_ref, k1_ref, k2_ref, s_ref, ht_ref):
    x = x_ref[...]
    ms = jnp.mean(x * x, axis=-1, keepdims=True)
    h2 = x * lax.rsqrt(ms + NORM_EPS) * nw_ref[...]
    ht_ref[...] = h2.T.astype(BF16)
    q = jnp.dot(h2.astype(BF16), wq_ref[...], preferred_element_type=F32).astype(BF16)
    half = PEER_DQ // 2
    for h in range(PEER_HEADS):
        for p, kref in enumerate((k1_ref, k2_ref)):
            qh = q[:, h * PEER_DQ + p * half: h * PEER_DQ + (p + 1) * half]
            s_ref[p, h] = lax.dot_general(kref[...], qh, (((1,), (1,)), ((), ())), preferred_element_type=F32)


def peer_query(x1, norm_w, w_q, keys_1, keys_2, tm=256):
    n, d = x1.shape
    const = lambda shape: pl.BlockSpec(shape, lambda i: (0, 0), pipeline_mode=pl.Buffered(1))
    return pl.pallas_call(
        _peer_query_kernel,
        grid=(n // tm,),
        in_specs=[
            pl.BlockSpec((tm, d), lambda i: (i, 0)),
            const((1, d)), const((d, PEER_HEADS * PEER_DQ)),
            const((PEER_NKEYS, PEER_DQ // 2)), const((PEER_NKEYS, PEER_DQ // 2)),
        ],
        out_specs=[
            pl.BlockSpec((2, PEER_HEADS, PEER_NKEYS, tm), lambda i: (0, 0, 0, i)),
            pl.BlockSpec((d, tm), lambda i: (0, i)),
        ],
        out_shape=[
            jax.ShapeDtypeStruct((2, PEER_HEADS, PEER_NKEYS, n), F32),
            jax.ShapeDtypeStruct((d, n), BF16),
        ],
        compiler_params=_cparams(("arbitrary",)),
        name="peer_query",
    )(x1, norm_w.reshape(1, d), w_q.astype(BF16), keys_1.astype(BF16), keys_2.astype(BF16))


_CAND_GROUPS = [(0, 0), (0, 8)] + [(a, 0) for a in range(1, 8)]


def _peer_topk_kernel(s_ref, lim_ref, rank2_ref, e1_ref, e2_ref, v1_s, r1_s, v2_s, r2_s):
    tn = s_ref.shape[-1]
    K = PEER_TOPK
    neg = -jnp.inf
    rowid = lax.broadcasted_iota(jnp.int32, (PEER_NKEYS, tn), 0)
    iota_k = lax.broadcasted_iota(jnp.int32, (K, tn), 0)
    nrow = 8 * (len(_CAND_GROUPS) + 1)
    r = lax.broadcasted_iota(jnp.int32, (nrow, tn), 0)
    grp, sub = r // 8, r % 8
    ca = jnp.where(grp < 2, 0, jnp.where(grp < 9, grp - 1, 8 + sub))
    cb = jnp.where(grp == 1, 8 + sub, jnp.where(grp < 9, sub, 0))
    flat = ca * K + cb
    valid = (ca + 1) * (cb + 1) <= K

    def top16(s, break_ties):
        rank = jnp.full((PEER_NKEYS, tn), K, jnp.int32)
        vals = []
        for a in range(K):
            m = jnp.max(s, axis=0, keepdims=True)
            hit = s == m
            if break_ties:
                idx = jnp.min(jnp.where(hit, rowid, PEER_NKEYS), axis=0, keepdims=True)
                hit = rowid == idx
            rank = jnp.where(hit, a, rank)
            s = jnp.where(hit, neg, s)
            vals.append(m)
        return jnp.concatenate(vals, axis=0), rank

    def head(h, carry):
        s1 = s_ref[0, h]
        s2 = s_ref[1, h]
        v1_s[...], r1_s[...] = top16(s1, False)
        v2_s[...], r2_s[...] = top16(s2, False)
        ranked = (jnp.sum((r1_s[...] < K).astype(jnp.int32), axis=0, keepdims=True)
                  + jnp.sum((r2_s[...] < K).astype(jnp.int32), axis=0, keepdims=True))
        tied = jnp.max(jnp.abs(ranked - 2 * K)) > 0

        @pl.when(tied)
        def _():
            v1_s[...], r1_s[...] = top16(s1, True)
            v2_s[...], r2_s[...] = top16(s2, True)

        v1, rank1 = v1_s[...], r1_s[...]
        v2, rank2 = v2_s[...], r2_s[...]
        pieces = [v1[a:a + 1] + v2[b0:b0 + 8] for a, b0 in _CAND_GROUPS] + [v1[8:16] + v2[0:1]]
        cand = jnp.where(valid, jnp.concatenate(pieces, axis=0), neg)
        count = jnp.zeros((K, tn), F32)
        z = jnp.zeros((1, tn), F32)
        m0 = v1[0:1] + v2[0:1]
        for _ in range(K):
            m = jnp.max(cand, axis=0, keepdims=True)
            f = jnp.min(jnp.where(cand == m, flat, K * K), axis=0, keepdims=True)
            cand = jnp.where(flat == f, neg, cand)
            count = count + jnp.where(iota_k == (f >> 4), 1.0, 0.0)
            z = z + jnp.exp(m - m0)
        inv_z = 1.0 / z
        e1_ref[h] = jnp.where(rank1 < K, jnp.exp(s1 - v1[0:1]) * inv_z, 0.0)
        e2_ref[h] = jnp.where(rank2 < K, jnp.exp(s2 - v2[0:1]), 0.0).astype(BF16)
        rank2_ref[h] = rank2.astype(F32).astype(BF16)
        lim = jnp.zeros((PEER_NKEYS, tn), F32)
        for a in range(K):
            lim = jnp.where(rank1 == a, count[a:a + 1], lim)
        lim_ref[h] = lim
        return carry

    lax.fori_loop(0, PEER_HEADS, head, 0)


def peer_topk(s_t, tn=256):
    n = s_t.shape[-1]
    spec = pl.BlockSpec((PEER_HEADS, PEER_NKEYS, tn), lambda i: (0, 0, i))
    shp = lambda dt: jax.ShapeDtypeStruct((PEER_HEADS, PEER_NKEYS, n), dt)
    return pl.pallas_call(
        _peer_topk_kernel,
        grid=(n // tn,),
        in_specs=[pl.BlockSpec((2, PEER_HEADS, PEER_NKEYS, tn), lambda i: (0, 0, 0, i))],
        out_specs=[spec, spec, spec, spec],
        out_shape=[shp(F32), shp(BF16), shp(F32), shp(BF16)],
        scratch_shapes=[pltpu.VMEM((PEER_TOPK, tn), F32), pltpu.VMEM((PEER_NKEYS, tn), jnp.int32),
                        pltpu.VMEM((PEER_TOPK, tn), F32), pltpu.VMEM((PEER_NKEYS, tn), jnp.int32)],
        compiler_params=_cparams(("arbitrary",)),
        name="peer_topk",
    )(s_t)


def _peer_ffn_kernel(x_ref, ht_ref, lim_ref, rank2_ref, e1_ref, e2_ref, u_ref, vt_ref, fnw_ref, o_ref,
                     acc_ref, act_ref, pre_ref, *, nj):
    s = pl.program_id(0)
    j = s % nj
    slot = s % 2
    eb = u_ref.shape[0]
    zero = jnp.zeros((), BF16)

    @pl.when(s == 0)
    def _():
        acc_ref[...] = jnp.zeros_like(acc_ref)
        act_ref[...] = jnp.zeros_like(act_ref)

    acc_ref[...] += jnp.dot(vt_ref[...], act_ref[1 - slot], preferred_element_type=F32)
    pre_ref[...] = jnp.dot(u_ref[...], ht_ref[...], preferred_element_type=F32)
    for jj in range(eb // PEER_NKEYS):
        gate = None
        for h in range(PEER_HEADS):
            lrow = lim_ref[h, jj:jj + 1, :].astype(BF16)
            erow = e1_ref[h, jj:jj + 1, :].astype(BF16)
            term = jnp.where(rank2_ref[h] < lrow, e2_ref[h], zero) * erow
            gate = term if gate is None else gate + term
        p = pre_ref[pl.ds(jj * PEER_NKEYS, PEER_NKEYS), :]
        gelu = 0.5 * p * (1.0 + lax.erf(p * (2.0 ** -0.5)))
        act_ref[slot, pl.ds(jj * PEER_NKEYS, PEER_NKEYS), :] = gelu.astype(BF16) * gate

    @pl.when(jnp.logical_and(j == 0, s > 0))
    def _():
        y = x_ref[...] + acc_ref[...].T
        ms = jnp.mean(y * y, axis=-1, keepdims=True)
        o_ref[...] = y * lax.rsqrt(ms + NORM_EPS) * fnw_ref[...]
        acc_ref[...] = jnp.zeros_like(acc_ref)


def peer_ffn(x1, h2t, lim, rank2, e1, e2, expert_u, expert_v, final_norm_w, tn=512, eb=1024):
    n, d = x1.shape
    ne = expert_u.shape[0]
    u = expert_u.astype(BF16)
    vt = expert_v.T.astype(BF16)
    nsub = eb // PEER_NKEYS
    assert nsub == 8
    nj = ne // eb
    ni = n // tn
    cur_i = lambda s: jnp.minimum(s // nj, ni - 1)
    prev_i = lambda s: jnp.maximum(s - 1, 0) // nj
    prev_j = lambda s: jnp.maximum(s - 1, 0) % nj
    tok = pl.BlockSpec((PEER_HEADS, PEER_NKEYS, tn), lambda s: (0, 0, cur_i(s)))
    row = pl.BlockSpec((PEER_HEADS, nsub, tn), lambda s: (0, s % nj, cur_i(s)))
    return pl.pallas_call(
        functools.partial(_peer_ffn_kernel, nj=nj),
        grid=(ni * nj + 1,),
        in_specs=[
            pl.BlockSpec((tn, d), lambda s: (prev_i(s), 0)),
            pl.BlockSpec((d, tn), lambda s: (0, cur_i(s))),
            row, tok, row, tok,
            pl.BlockSpec((eb, d), lambda s: (s % nj, 0)),
            pl.BlockSpec((d, eb), lambda s: (0, prev_j(s))),
            pl.BlockSpec((1, d), lambda s: (0, 0)),
        ],
        out_specs=pl.BlockSpec((tn, d), lambda s: (prev_i(s), 0)),
        out_shape=jax.ShapeDtypeStruct((n, d), F32),
        scratch_shapes=[pltpu.VMEM((d, tn), F32), pltpu.VMEM((2, eb, tn), BF16), pltpu.VMEM((eb, tn), F32)],
        compiler_params=_cparams(("arbitrary",)),
        name="peer_ffn",
    )(x1, h2t, lim, rank2, e1, e2, u, vt, final_norm_w.reshape(1, d))


def _permute_w_in(w_in):
    q_end = 2 * RET_QK + 2 * RET_WIDTH
    rw_end = q_end + 3 * RWKV_WIDTH
    lora_end = q_end + RWKV_COLS
    return jnp.concatenate(
        [w_in[:, :q_end], w_in[:, lora_end:], w_in[:, q_end:rw_end], w_in[:, rw_end:lora_end]], axis=1)


def kernel(x, norm1_w, w_in, ret_gn_w, ret_gn_b, rwkv_mix, rwkv_w0, rwkv_w_up, rwkv_a0, rwkv_a_up, rwkv_g_up, rwkv_k_k, rwkv_k_a, rwkv_r_k, rwkv_ln_w, rwkv_ln_b, w_ret_branch, w_rwkv_branch, w_out, norm2_w, peer_w_q, peer_keys_1, peer_keys_2, peer_u, peer_v, final_norm_w):
    b, t, d = x.shape
    n = b * t
    assert w_in.shape[0] == 1, "single-layer block: the final norm is fused into the PEER kernel"
    l = 0
    xf = x.reshape(n, d)
    zp = in_proj(xf, norm1_w[l], _permute_w_in(w_in[l]).astype(BF16))
    o_ret = retention(zp, ret_gn_w[l], ret_gn_b[l], b, t)
    pre = rwkv_pre(zp, rwkv_mix[l], rwkv_w0[l], rwkv_w_up[l], rwkv_a0[l], rwkv_a_up[l], rwkv_g_up[l],
                   rwkv_k_k[l], rwkv_k_a[l], rwkv_r_k[l], b, t)
    o_rwkv = rwkv_scan(*pre, rwkv_ln_w[l], rwkv_ln_b[l], b, t)
    x1 = merge_out(xf, zp, o_ret, o_rwkv, w_ret_branch[l], w_rwkv_branch[l], w_out[l])
    s_t, h2t = peer_query(x1, norm2_w[l], peer_w_q[l], peer_keys_1[l], peer_keys_2[l])
    lim, rank2, e1, e2 = peer_topk(s_t)
    out = peer_ffn(x1, h2t, lim, rank2, e1, e2, peer_u[l], peer_v[l], final_norm_w)
    return out.reshape(b, t, d)
```

```python
import functools

import numpy as np
import jax
import jax.numpy as jnp
from jax import lax
from jax.experimental import pallas as pl
from jax.experimental.pallas import tpu as pltpu

F32 = jnp.float32
BF16 = jnp.bfloat16

D_MODEL = 2048
CHUNK = 64
NORM_EPS = 1e-6
RET_HEADS = 4
RET_DK = 256
RET_DV = 256
RET_QK = RET_HEADS * RET_DK
RET_WIDTH = RET_HEADS * RET_DV
ROPE_BASE = 10000.0
RWKV_HEAD = 64
RWKV_WIDTH = D_MODEL // 2
RWKV_HEADS = RWKV_WIDTH // RWKV_HEAD
DECAY_LORA = 64
AAA_LORA = 64
GATE_LORA = 128
LORA_COLS = DECAY_LORA + AAA_LORA + GATE_LORA
RWKV_COLS = 3 * RWKV_WIDTH + LORA_COLS
RWKV_GN_EPS = 64e-5
IN_COLS = 2 * RET_QK + 2 * RET_WIDTH + RWKV_COLS + 2 * D_MODEL

ZP_GATE_RET = 4096
ZP_GATE_RWKV = 6144
ZP_RWKV = 8192
ZP_LORA = 11264

VMEM_LIMIT = 56 * 1024 * 1024


def _cparams(sem):
    return pltpu.CompilerParams(dimension_semantics=sem, vmem_limit_bytes=VMEM_LIMIT)


def _in_proj_kernel(x_ref, nw_ref, w_ref, o_ref, h_ref):
    @pl.when(pl.program_id(1) == 0)
    def _():
        x = x_ref[...]
        ms = jnp.mean(x * x, axis=-1, keepdims=True)
        h_ref[...] = (x * lax.rsqrt(ms + NORM_EPS) * nw_ref[...]).astype(BF16)

    o_ref[...] = jnp.dot(h_ref[...], w_ref[...], preferred_element_type=F32)


def in_proj(x, norm_w, w_bf16, tm=1024, tn=1280):
    n, d = x.shape
    cols = w_bf16.shape[1]
    return pl.pallas_call(
        _in_proj_kernel,
        grid=(n // tm, cols // tn),
        in_specs=[
            pl.BlockSpec((tm, d), lambda i, j: (i, 0)),
            pl.BlockSpec((1, d), lambda i, j: (0, 0)),
            pl.BlockSpec((d, tn), lambda i, j: (0, j)),
        ],
        out_specs=pl.BlockSpec((tm, tn), lambda i, j: (i, j)),
        out_shape=jax.ShapeDtypeStruct((n, cols), F32),
        scratch_shapes=[pltpu.VMEM((tm, d), BF16)],
        compiler_params=_cparams(("arbitrary", "arbitrary")),
        name="in_proj",
    )(x, norm_w.reshape(1, d), w_bf16)


def _retention_kernel(q_ref, k_ref, v_ref, g_ref, cos_ref, sin_ref, dintra_ref, qdec_ref, kdec_ref,
                      cdec_ref, gnw_ref, gnb_ref, o_ref, s_ref, qr_ref, kr_ref, acc_ref):
    tb = q_ref.shape[0]
    half = RET_DK // 2

    @pl.when(pl.program_id(2) == 0)
    def _():
        s_ref[...] = jnp.zeros_like(s_ref)

    cos = cos_ref[...]
    sin = sin_ref[...]

    def rot(z_ref):
        z1 = z_ref[:, :half]
        z2 = z_ref[:, half:]
        return jnp.concatenate([z1 * cos - z2 * sin, z2 * cos + z1 * sin], axis=-1)

    qr_ref[...] = rot(q_ref)
    kr_ref[...] = rot(k_ref) * (RET_DK ** -0.5)
    d_intra = dintra_ref[0]
    q_decay = qdec_ref[0]
    k_decay = kdec_ref[0]
    c_decay = cdec_ref[0]
    for c in range(tb // CHUNK):
        sl = pl.ds(c * CHUNK, CHUNK)
        qc = qr_ref[sl, :]
        kc = kr_ref[sl, :]
        vc = v_ref[sl, :].astype(BF16)
        scores = lax.dot_general(qc.astype(BF16), kc.astype(BF16), (((1,), (1,)), ((), ())),
                                 preferred_element_type=F32) * d_intra
        s_prev = s_ref[...]
        o = jnp.dot(scores.astype(BF16), vc, preferred_element_type=F32)
        o = o + jnp.dot((qc * q_decay).astype(BF16), s_prev.astype(BF16), preferred_element_type=F32)
        kd_t = (kc * k_decay).T.astype(BF16)
        s_ref[...] = s_prev * c_decay + jnp.dot(kd_t, vc, preferred_element_type=F32)
        acc_ref[sl, :] = o
    o = acc_ref[...]
    mu = jnp.mean(o, axis=-1, keepdims=True)
    oc = o - mu
    var = jnp.mean(oc * oc, axis=-1, keepdims=True)
    y = oc * lax.rsqrt(var + NORM_EPS) * gnw_ref[...] + gnb_ref[...]
    g = g_ref[...]
    o_ref[...] = y * (g * jax.nn.sigmoid(g))


def _retention_tables(t):
    h = RET_HEADS
    log_g = jnp.log(1.0 - jnp.exp2(-5.0 - jnp.arange(h, dtype=F32)))
    n = jnp.arange(CHUNK, dtype=F32)
    d_intra = jnp.exp(log_g[:, None, None] * jnp.abs(n[:, None] - n[None, :]))
    q_decay = jnp.exp(log_g[:, None] * (n[None, :] + 1.0))
    k_decay = jnp.exp(log_g[:, None] * (CHUNK - 1.0 - n[None, :]))
    chunk_decay = jnp.exp(log_g * CHUNK)
    q_decay = jnp.broadcast_to(q_decay[:, :, None], (h, CHUNK, RET_DK))
    k_decay = jnp.broadcast_to(k_decay[:, :, None], (h, CHUNK, RET_DK))
    chunk_decay = jnp.broadcast_to(chunk_decay[:, None, None], (h, 1, RET_DV))
    half = RET_DK // 2
    inv_freq = ROPE_BASE ** (-jnp.arange(half, dtype=F32) * 2.0 / RET_DK)
    ang = jnp.arange(t, dtype=jnp.int32).astype(F32)[:, None] * inv_freq[None, :]
    return d_intra, q_decay, k_decay, chunk_decay, jnp.cos(ang), jnp.sin(ang)


def retention(zp, gn_w, gn_b, batch, t, tb=512):
    n = zp.shape[0]
    h = RET_HEADS
    nt = t // tb
    d_intra, q_decay, k_decay, chunk_decay, cos, sin = _retention_tables(t)

    def zspec(col0):
        return pl.BlockSpec((tb, RET_DK), lambda b, hh, i: (b * nt + i, col0 + hh))

    tab = lambda shape: pl.BlockSpec((1,) + shape, lambda b, hh, i: (hh, 0, 0))
    return pl.pallas_call(
        _retention_kernel,
        grid=(batch, h, nt),
        in_specs=[
            zspec(0), zspec(h), zspec(2 * h), zspec(3 * h),
            pl.BlockSpec((tb, RET_DK // 2), lambda b, hh, i: (i, 0)),
            pl.BlockSpec((tb, RET_DK // 2), lambda b, hh, i: (i, 0)),
            tab((CHUNK, CHUNK)), tab((CHUNK, RET_DK)), tab((CHUNK, RET_DK)), tab((1, RET_DV)),
            pl.BlockSpec((1, RET_DV), lambda b, hh, i: (0, hh)),
            pl.BlockSpec((1, RET_DV), lambda b, hh, i: (0, hh)),
        ],
        out_specs=pl.BlockSpec((tb, RET_DV), lambda b, hh, i: (b * nt + i, hh)),
        out_shape=jax.ShapeDtypeStruct((n, RET_WIDTH), F32),
        scratch_shapes=[
            pltpu.VMEM((RET_DK, RET_DV), F32),
            pltpu.VMEM((tb, RET_DK), F32),
            pltpu.VMEM((tb, RET_DK), F32),
            pltpu.VMEM((tb, RET_DV), F32),
        ],
        compiler_params=_cparams(("arbitrary", "arbitrary", "arbitrary")),
        name="retention",
    )(zp, zp, zp, zp, cos, sin, d_intra, q_decay, k_decay, chunk_decay,
      gn_w.reshape(1, RET_WIDTH), gn_b.reshape(1, RET_WIDTH))


def _split2(x):
    hi = x.astype(BF16)
    lo = (x - hi.astype(F32)).astype(BF16)
    return hi, lo


def _split3(x):
    hi = x.astype(BF16)
    r = x - hi.astype(F32)
    mid = r.astype(BF16)
    lo = (r - mid.astype(F32)).astype(BF16)
    return hi, mid, lo


def _group_sum(x, gmat):
    hi, lo = _split2(x)
    return (jnp.dot(hi, gmat, preferred_element_type=F32) + jnp.dot(lo, gmat, preferred_element_type=F32))


def _bmm(a, b):
    return lax.dot_general(a.astype(BF16), b.astype(BF16), (((2,), (1,)), ((0,), (0,))),
                           preferred_element_type=F32)


def _bmm_nt(a, b):
    return lax.dot_general(a.astype(BF16), b.astype(BF16), (((2,), (2,)), ((0,), (0,))),
                           preferred_element_type=F32)


def _bmm_tn(a, b):
    return lax.dot_general(a.astype(BF16), b.astype(BF16), (((1,), (1,)), ((0,), (0,))),
                           preferred_element_type=F32)


RWKV_HEAD_GROUP = 16


def _rwkv_pre_kernel(r_ref, k_ref, v_ref, l_ref, pr_ref, pk_ref, pv_ref, plr_ref,
                     mixr_ref, mixk_ref, mixv_ref, mixl_ref, w0_ref, a0_ref, kk_ref, ka_ref, rk_ref,
                     wup_ref, aup_ref, gup_ref, gmat_ref, ltri_ref,
                     m_out, n_out, q_out, y_out, g_out, bv_out,
                     at_s, rt_s, bh_s, kh_s, bt_s, kt_s, v_s, gc_s, *, tiles_per_batch):
    tb = r_ref.shape[0]
    nchunk = tb // CHUNK
    first = (pl.program_id(0) % tiles_per_batch) == 0
    rowid = lax.broadcasted_iota(jnp.int32, (tb, 1), 0)

    def shift_lerp(cur_ref, prev_ref, mix_ref):
        cur = cur_ref[...]
        prev_row = jnp.where(first, 0.0, prev_ref[7:8, :])
        prev = jnp.where(rowid == 0, prev_row, pltpu.roll(cur, 1, axis=0))
        return cur + (prev - cur) * mix_ref[...]

    r = shift_lerp(r_ref, pr_ref, mixr_ref)
    k = shift_lerp(k_ref, pk_ref, mixk_ref)
    v = shift_lerp(v_ref, pv_ref, mixv_ref)
    lo = shift_lerp(l_ref, plr_ref, mixl_ref)
    xw = lo[:, :DECAY_LORA]
    xa = lo[:, DECAY_LORA:DECAY_LORA + AAA_LORA]
    xg = lo[:, DECAY_LORA + AAA_LORA:]
    wl = jnp.dot(jnp.tanh(xw).astype(BF16), wup_ref[...], preferred_element_type=F32)
    u = -(w0_ref[...] + wl)
    softplus = jnp.maximum(u, 0.0) + jnp.log(1.0 + jnp.exp(-jnp.abs(u)))
    logw = -jnp.exp(-softplus - 0.5)
    a = jax.nn.sigmoid(a0_ref[...] + jnp.dot(xa.astype(BF16), aup_ref[...], preferred_element_type=F32))
    g_out[...] = jnp.dot(jax.nn.sigmoid(xg).astype(BF16), gup_ref[...], preferred_element_type=F32)
    gmat = gmat_ref[...]
    kk = k * kk_ref[...]
    kk = kk / jnp.maximum(jnp.sqrt(_group_sum(kk * kk, gmat)), 1e-12)
    k2 = k * (1.0 + (a - 1.0) * ka_ref[...])
    bv_out[...] = _group_sum(r * k2 * rk_ref[...], gmat) * v
    beta = kk * a
    ltri = ltri_ref[...]

    def to_heads(dst, c, val):
        for h in range(RWKV_HEADS):
            dst[h, pl.ds(c * CHUNK, CHUNK), :] = val[:, h * RWKV_HEAD:(h + 1) * RWKV_HEAD]

    for c in range(nchunk):
        sl = slice(c * CHUNK, (c + 1) * CHUNK)
        lw = logw[sl]
        h1, h2, h3 = _split3(lw)
        cs = (jnp.dot(ltri, h1, preferred_element_type=F32) + jnp.dot(ltri, h2, preferred_element_type=F32)
              + jnp.dot(ltri, h3, preferred_element_type=F32))
        cs_last = cs[CHUNK - 1:CHUNK, :]
        g_t = jnp.exp(cs)
        g_tm1 = jnp.exp(cs - lw)
        inv_g = jnp.exp(-cs)
        g_end = jnp.exp(cs_last - cs)
        to_heads(at_s, c, -kk[sl] * g_tm1)
        to_heads(rt_s, c, r[sl] * g_t)
        to_heads(bh_s, c, beta[sl] * inv_g)
        to_heads(kh_s, c, k2[sl] * inv_g)
        to_heads(bt_s, c, beta[sl] * g_end)
        to_heads(kt_s, c, k2[sl] * g_end)
        to_heads(v_s, c, v[sl])
        gc = jnp.exp(cs_last)
        for h in range(RWKV_HEADS):
            gc_s[h, pl.ds(c * 8, 8), :] = jnp.broadcast_to(gc[:, h * RWKV_HEAD:(h + 1) * RWKV_HEAD], (8, RWKV_HEAD))

    G = RWKV_HEAD_GROUP
    ti = lax.broadcasted_iota(jnp.int32, (CHUNK, CHUNK), 0)
    si = lax.broadcasted_iota(jnp.int32, (CHUNK, CHUNK), 1)
    strict = (si < ti)[None]
    incl = (si <= ti)[None]
    eye = (si == ti).astype(F32)[None]

    def body(idx, carry):
        c = idx // (RWKV_HEADS // G)
        hg = idx % (RWKV_HEADS // G)
        hs = pl.ds(pl.multiple_of(hg * G, G), G)
        ts = pl.ds(pl.multiple_of(c * CHUNK, CHUNK), CHUNK)
        at = at_s[hs, ts, :]
        rt = rt_s[hs, ts, :]
        bh = bh_s[hs, ts, :]
        kh = kh_s[hs, ts, :]
        bt = bt_s[hs, ts, :]
        kt = kt_s[hs, ts, :]
        vv = v_s[hs, ts, :]
        gc = gc_s[hs, pl.ds(pl.multiple_of(c * 8, 8), 1), :]
        aab = jnp.where(strict, _bmm_nt(at, bh), 0.0)
        aak = jnp.where(strict, _bmm_nt(at, kh), 0.0)
        aqb = jnp.where(incl, _bmm_nt(rt, bh), 0.0)
        aqk = jnp.where(incl, _bmm_nt(rt, kh), 0.0)
        p = aab
        tm = eye + aab
        for _ in range(5):
            p = _bmm(p, p)
            tm = tm + _bmm(p, tm)
        akv = _bmm(aak, vv)
        wt = _bmm(tm, at)
        u0 = _bmm(tm, akv)
        q_out[hs, ts, :] = rt + _bmm(aqb, wt)
        y_out[hs, ts, :] = _bmm(aqb, u0) + _bmm(aqk, vv)
        m_out[hs, ts, :] = _bmm_tn(bt, wt) + eye * gc
        n_out[hs, ts, :] = _bmm_tn(bt, u0) + _bmm_tn(kt, vv)
        return carry

    lax.fori_loop(0, nchunk * (RWKV_HEADS // G), body, 0)


def _rwkv_consts():
    lane = np.arange(RWKV_WIDTH) // RWKV_HEAD
    gmat = (lane[:, None] == lane[None, :]).astype(np.float32)
    ltri = np.tril(np.ones((CHUNK, CHUNK), np.float32))
    return jnp.asarray(gmat, BF16), jnp.asarray(ltri, BF16)


def rwkv_pre(zp, mix, w0, w_up, a0, a_up, g_up, k_k, k_a, r_k, batch, t, tb=128):
    n = zp.shape[0]
    W = RWKV_WIDTH
    nt = t // tb
    gmat, ltri = _rwkv_consts()
    row = lambda p: p.reshape(1, -1)
    mix_r, mix_k, mix_v, mix_l = mix[:W], mix[W:2 * W], mix[2 * W:3 * W], mix[3 * W:]
    cb = ZP_RWKV // W
    cur = lambda j: pl.BlockSpec((tb, W), lambda i: (i, cb + j))
    prev = lambda j: pl.BlockSpec((8, W), lambda i: (jnp.maximum(i * (tb // 8) - 1, 0), cb + j))
    full = lambda shape: pl.BlockSpec(shape, lambda i: (0,) * len(shape))
    hm = pl.BlockSpec((RWKV_HEADS, tb, RWKV_HEAD), lambda i: (0, i, 0))
    nat = pl.BlockSpec((tb, W), lambda i: (i, 0))
    hm_shape = jax.ShapeDtypeStruct((RWKV_HEADS, n, RWKV_HEAD), F32)
    nat_shape = jax.ShapeDtypeStruct((n, W), F32)
    hscr = pltpu.VMEM((RWKV_HEADS, tb, RWKV_HEAD), F32)
    return pl.pallas_call(
        functools.partial(_rwkv_pre_kernel, tiles_per_batch=nt),
        grid=(n // tb,),
        in_specs=[
            cur(0), cur(1), cur(2),
            pl.BlockSpec((tb, LORA_COLS), lambda i: (i, ZP_LORA // LORA_COLS)),
            prev(0), prev(1), prev(2),
            pl.BlockSpec((8, LORA_COLS), lambda i: (jnp.maximum(i * (tb // 8) - 1, 0), ZP_LORA // LORA_COLS)),
            full((1, W)), full((1, W)), full((1, W)), full((1, LORA_COLS)),
            full((1, W)), full((1, W)), full((1, W)), full((1, W)), full((1, W)),
            full((DECAY_LORA, W)), full((AAA_LORA, W)), full((GATE_LORA, W)),
            full((W, W)), full((CHUNK, CHUNK)),
        ],
        out_specs=[hm, hm, hm, hm, nat, nat],
        out_shape=[hm_shape, hm_shape, hm_shape, hm_shape, nat_shape, nat_shape],
        scratch_shapes=[hscr] * 7 + [pltpu.VMEM((RWKV_HEADS, 8 * (tb // CHUNK), RWKV_HEAD), F32)],
        compiler_params=_cparams(("arbitrary",)),
        name="rwkv_pre",
    )(zp, zp, zp, zp, zp, zp, zp, zp,
      row(mix_r), row(mix_k), row(mix_v), row(mix_l), row(w0), row(a0), row(k_k), row(k_a), row(r_k),
      w_up.astype(BF16), a_up.astype(BF16), g_up.astype(BF16), gmat, ltri)


def _rwkv_scan_kernel(m_ref, n_ref, q_ref, y_ref, g_ref, bv_ref, lnw_ref, lnb_ref, gmat_ref, o_ref, s_ref, ys_ref):
    tb = g_ref.shape[0]

    @pl.when(pl.program_id(1) == 0)
    def _():
        s_ref[...] = jnp.zeros_like(s_ref)

    for c in range(tb // CHUNK):
        ts = pl.ds(c * CHUNK, CHUNK)
        s = s_ref[...]
        ys_ref[:, ts, :] = _bmm(q_ref[:, ts, :], s) + y_ref[:, ts, :]
        s_ref[...] = _bmm(m_ref[:, ts, :], s) + n_ref[:, ts, :]
    y = jnp.concatenate([ys_ref[h] for h in range(RWKV_HEADS)], axis=-1)
    gmat = gmat_ref[...]
    mu = _group_sum(y, gmat) * (1.0 / RWKV_HEAD)
    yc = y - mu
    var = _group_sum(yc * yc, gmat) * (1.0 / RWKV_HEAD)
    yn = yc * lax.rsqrt(var + RWKV_GN_EPS) * lnw_ref[...] + lnb_ref[...]
    o_ref[...] = (yn + bv_ref[...]) * g_ref[...]


def rwkv_scan(m, nn, qt, y1, g, bv, ln_w, ln_b, batch, t, tb=256):
    n = g.shape[0]
    W = RWKV_WIDTH
    nt = t // tb
    gmat, _ = _rwkv_consts()
    hm = pl.BlockSpec((RWKV_HEADS, tb, RWKV_HEAD), lambda b, i: (0, b * nt + i, 0))
    nat = pl.BlockSpec((tb, W), lambda b, i: (b * nt + i, 0))
    full = lambda shape: pl.BlockSpec(shape, lambda b, i: (0,) * len(shape))
    return pl.pallas_call(
        _rwkv_scan_kernel,
        grid=(batch, nt),
        in_specs=[hm, hm, hm, hm, nat, nat, full((1, W)), full((1, W)), full((W, W))],
        out_specs=nat,
        out_shape=jax.ShapeDtypeStruct((n, W), F32),
        scratch_shapes=[pltpu.VMEM((RWKV_HEADS, RWKV_HEAD, RWKV_HEAD), F32),
                        pltpu.VMEM((RWKV_HEADS, tb, RWKV_HEAD), F32)],
        compiler_params=_cparams(("arbitrary", "arbitrary")),
        name="rwkv_scan",
    )(m, nn, qt, y1, g, bv, ln_w.reshape(1, W), ln_b.reshape(1, W), gmat)


def _merge_kernel(x_ref, gr_ref, gw_ref, oret_ref, orwkv_ref, wr_ref, ww_ref, wo_ref, o_ref):
    pr = jnp.dot(oret_ref[...].astype(BF16), wr_ref[...], preferred_element_type=F32)
    pw = jnp.dot(orwkv_ref[...].astype(BF16), ww_ref[...], preferred_element_type=F32)
    merged = jax.nn.sigmoid(gr_ref[...]) * pr + jax.nn.sigmoid(gw_ref[...]) * pw
    o_ref[...] = x_ref[...] + jnp.dot(merged.astype(BF16), wo_ref[...], preferred_element_type=F32)


def merge_out(x, zp, o_ret, o_rwkv, w_ret, w_rwkv, w_out, tm=256):
    n, d = x.shape
    const = lambda shape: pl.BlockSpec(shape, lambda i: (0, 0), pipeline_mode=pl.Buffered(1))
    return pl.pallas_call(
        _merge_kernel,
        grid=(n // tm,),
        in_specs=[
            pl.BlockSpec((tm, d), lambda i: (i, 0)),
            pl.BlockSpec((tm, d), lambda i: (i, ZP_GATE_RET // D_MODEL)),
            pl.BlockSpec((tm, d), lambda i: (i, ZP_GATE_RWKV // D_MODEL)),
            pl.BlockSpec((tm, RET_WIDTH), lambda i: (i, 0)),
            pl.BlockSpec((tm, RWKV_WIDTH), lambda i: (i, 0)),
            const((RET_WIDTH, d)), const((RWKV_WIDTH, d)), const((d, d)),
        ],
        out_specs=pl.BlockSpec((tm, d), lambda i: (i, 0)),
        out_shape=jax.ShapeDtypeStruct((n, d), F32),
        compiler_params=_cparams(("arbitrary",)),
        name="merge_out",
    )(x, zp, zp, o_ret, o_rwkv, w_ret.astype(BF16), w_rwkv.astype(BF16), w_out.astype(BF16))


PEER_HEADS = 8
PEER_NKEYS = 128
PEER_NEXPERTS = PEER_NKEYS * PEER_NKEYS
PEER_DQ = 256
PEER_TOPK = 16
PEER_ACT_CHUNKS = 4


def _peer_query_kernel(x_ref, nw_ref, wq_ref, k1_ref, k2_ref, s_ref, ht_ref):
    x = x_ref[...]
    ms = jnp.mean(x * x, axis=-1, keepdims=True)
    h2 = x * lax.rsqrt(ms + NORM_EPS) * nw_ref[...]
    ht_ref[...] = h2.T.astype(BF16)
    q = jnp.dot(h2.astype(BF16), wq_ref[...], preferred_element_type=F32).astype(BF16)
    half = PEER_DQ // 2
    for h in range(PEER_HEADS):
        for p, kref in enumerate((k1_ref, k2_ref)):
            qh = q[:, h * PEER_DQ + p * half: h * PEER_DQ + (p + 1) * half]
            s_ref[p, h] = lax.dot_general(kref[...], qh, (((1,), (1,)), ((), ())), preferred_element_type=F32)


def peer_query(x1, norm_w, w_q, keys_1, keys_2, tm=256):
    n, d = x1.shape
    const = lambda shape: pl.BlockSpec(shape, lambda i: (0, 0), pipeline_mode=pl.Buffered(1))
    return pl.pallas_call(
        _peer_query_kernel,
        grid=(n // tm,),
        in_specs=[
            pl.BlockSpec((tm, d), lambda i: (i, 0)),
            const((1, d)), const((d, PEER_HEADS * PEER_DQ)),
            const((PEER_NKEYS, PEER_DQ // 2)), const((PEER_NKEYS, PEER_DQ // 2)),
        ],
        out_specs=[
            pl.BlockSpec((2, PEER_HEADS, PEER_NKEYS, tm), lambda i: (0, 0, 0, i)),
            pl.BlockSpec((d, tm), lambda i: (0, i)),
        ],
        out_shape=[
            jax.ShapeDtypeStruct((2, PEER_HEADS, PEER_NKEYS, n), F32),
            jax.ShapeDtypeStruct((d, n), BF16),
        ],
        compiler_params=_cparams(("arbitrary",)),
        name="peer_query",
    )(x1, norm_w.reshape(1, d), w_q.astype(BF16), keys_1.astype(BF16), keys_2.astype(BF16))


_CAND_GROUPS = [(0, 0), (0, 8)] + [(a, 0) for a in range(1, 8)]


def _peer_topk_kernel(s_ref, lim_ref, rank2_ref, e1_ref, e2_ref, v1_s, r1_s, v2_s, r2_s):
    tn = s_ref.shape[-1]
    K = PEER_TOPK
    neg = -jnp.inf
    rowid = lax.broadcasted_iota(jnp.int32, (PEER_NKEYS, tn), 0)
    iota_k = lax.broadcasted_iota(jnp.int32, (K, tn), 0)
    nrow = 8 * (len(_CAND_GROUPS) + 1)
    r = lax.broadcasted_iota(jnp.int32, (nrow, tn), 0)
    grp, sub = r // 8, r % 8
    ca = jnp.where(grp < 2, 0, jnp.where(grp < 9, grp - 1, 8 + sub))
    cb = jnp.where(grp == 1, 8 + sub, jnp.where(grp < 9, sub, 0))
    flat = ca * K + cb
    valid = (ca + 1) * (cb + 1) <= K

    def top16(s, break_ties):
        rank = jnp.full((PEER_NKEYS, tn), K, jnp.int32)
        vals = []
        for a in range(K):
            m = jnp.max(s, axis=0, keepdims=True)
            hit = s == m
            if break_ties:
                idx = jnp.min(jnp.where(hit, rowid, PEER_NKEYS), axis=0, keepdims=True)
                hit = rowid == idx
            rank = jnp.where(hit, a, rank)
            s = jnp.where(hit, neg, s)
            vals.append(m)
        return jnp.concatenate(vals, axis=0), rank

    def head(h, carry):
        s1 = s_ref[0, h]
        s2 = s_ref[1, h]
        v1_s[...], r1_s[...] = top16(s1, False)
        v2_s[...], r2_s[...] = top16(s2, False)
        ranked = (jnp.sum((r1_s[...] < K).astype(jnp.int32), axis=0, keepdims=True)
                  + jnp.sum((r2_s[...] < K).astype(jnp.int32), axis=0, keepdims=True))
        tied = jnp.max(jnp.abs(ranked - 2 * K)) > 0

        @pl.when(tied)
        def _():
            v1_s[...], r1_s[...] = top16(s1, True)
            v2_s[...], r2_s[...] = top16(s2, True)

        v1, rank1 = v1_s[...], r1_s[...]
        v2, rank2 = v2_s[...], r2_s[...]
        pieces = [v1[a:a + 1] + v2[b0:b0 + 8] for a, b0 in _CAND_GROUPS] + [v1[8:16] + v2[0:1]]
        cand = jnp.where(valid, jnp.concatenate(pieces, axis=0), neg)
        count = jnp.zeros((K, tn), F32)
        z = jnp.zeros((1, tn), F32)
        m0 = v1[0:1] + v2[0:1]
        for _ in range(K):
            m = jnp.max(cand, axis=0, keepdims=True)
            f = jnp.min(jnp.where(cand == m, flat, K * K), axis=0, keepdims=True)
            cand = jnp.where(flat == f, neg, cand)
            count = count + jnp.where(iota_k == (f >> 4), 1.0, 0.0)
            z = z + jnp.exp(m - m0)
        inv_z = 1.0 / z
        e1_ref[h] = jnp.where(rank1 < K, jnp.exp(s1 - v1[0:1]) * inv_z, 0.0)
        e2_ref[h] = jnp.where(rank2 < K, jnp.exp(s2 - v2[0:1]), 0.0).astype(BF16)
        rank2_ref[h] = rank2.astype(F32).astype(BF16)
        lim = jnp.zeros((PEER_NKEYS, tn), F32)
        for a in range(K):
            lim = jnp.where(rank1 == a, count[a:a + 1], lim)
        lim_ref[h] = lim
        return carry

    lax.fori_loop(0, PEER_HEADS, head, 0)


def peer_topk(s_t, tn=256):
    n = s_t.shape[-1]
    spec = pl.BlockSpec((PEER_HEADS, PEER_NKEYS, tn), lambda i: (0, 0, i))
    shp = lambda dt: jax.ShapeDtypeStruct((PEER_HEADS, PEER_NKEYS, n), dt)
    return pl.pallas_call(
        _peer_topk_kernel,
        grid=(n // tn,),
        in_specs=[pl.BlockSpec((2, PEER_HEADS, PEER_NKEYS, tn), lambda i: (0, 0, 0, i))],
        out_specs=[spec, spec, spec, spec],
        out_shape=[shp(F32), shp(BF16), shp(F32), shp(BF16)],
        scratch_shapes=[pltpu.VMEM((PEER_TOPK, tn), F32), pltpu.VMEM((PEER_NKEYS, tn), jnp.int32),
                        pltpu.VMEM((PEER_TOPK, tn), F32), pltpu.VMEM((PEER_NKEYS, tn), jnp.int32)],
        compiler_params=_cparams(("arbitrary",)),
        name="peer_topk",
    )(s_t)


def _peer_ffn_kernel(x_ref, ht_ref, lim_ref, rank2_ref, e1_ref, e2_ref, u_ref, vt_ref, fnw_ref, o_ref,
                     acc_ref, pre_a, pre_b, *act_refs):
    j = pl.program_id(1)
    eb = u_ref.shape[0]
    nsub = eb // PEER_NKEYS
    half = eb // 2
    per_chunk = nsub // len(act_refs)
    ck = per_chunk * PEER_NKEYS
    pre_refs = (pre_a, pre_b)
    zero = jnp.zeros((), BF16)

    @pl.when(j == 0)
    def _():
        acc_ref[...] = jnp.zeros_like(acc_ref)

    for k in range(2):
        pre_refs[k][...] = jnp.dot(u_ref[pl.ds(k * half, half), :], ht_ref[...], preferred_element_type=F32)
    for c, act_c in enumerate(act_refs):
        for jc in range(per_chunk):
            jj = c * per_chunk + jc
            gate = None
            for h in range(PEER_HEADS):
                lrow = lim_ref[h, jj:jj + 1, :].astype(BF16)
                erow = e1_ref[h, jj:jj + 1, :].astype(BF16)
                term = jnp.where(rank2_ref[h] < lrow, e2_ref[h], zero) * erow
                gate = term if gate is None else gate + term
            p = pre_refs[jj * PEER_NKEYS // half][pl.ds(jj * PEER_NKEYS % half, PEER_NKEYS), :]
            gelu = 0.5 * p * (1.0 + lax.erf(p * (2.0 ** -0.5)))
            act_c[pl.ds(jc * PEER_NKEYS, PEER_NKEYS), :] = gelu.astype(BF16) * gate
        acc_ref[...] += jnp.dot(vt_ref[:, pl.ds(c * ck, ck)], act_c[...], preferred_element_type=F32)

    @pl.when(j == pl.num_programs(1) - 1)
    def _():
        y = x_ref[...] + acc_ref[...].T
        ms = jnp.mean(y * y, axis=-1, keepdims=True)
        o_ref[...] = y * lax.rsqrt(ms + NORM_EPS) * fnw_ref[...]


def peer_ffn(x1, h2t, lim, rank2, e1, e2, expert_u, expert_v, final_norm_w, tn=512, eb=1024):
    n, d = x1.shape
    ne = expert_u.shape[0]
    u = expert_u.astype(BF16)
    vt = expert_v.T.astype(BF16)
    nsub = eb // PEER_NKEYS
    assert nsub == 8
    tok = pl.BlockSpec((PEER_HEADS, PEER_NKEYS, tn), lambda i, j: (0, 0, i))
    row = pl.BlockSpec((PEER_HEADS, nsub, tn), lambda i, j: (0, j, i))
    return pl.pallas_call(
        _peer_ffn_kernel,
        grid=(n // tn, ne // eb),
        in_specs=[
            pl.BlockSpec((tn, d), lambda i, j: (i, 0)),
            pl.BlockSpec((d, tn), lambda i, j: (0, i)),
            row, tok, row, tok,
            pl.BlockSpec((eb, d), lambda i, j: (j, 0)),
            pl.BlockSpec((d, eb), lambda i, j: (0, j)),
            pl.BlockSpec((1, d), lambda i, j: (0, 0)),
        ],
        out_specs=pl.BlockSpec((tn, d), lambda i, j: (i, 0)),
        out_shape=jax.ShapeDtypeStruct((n, d), F32),
        scratch_shapes=[pltpu.VMEM((d, tn), F32),
                        pltpu.VMEM((eb // 2, tn), F32), pltpu.VMEM((eb // 2, tn), F32)]
        + [pltpu.VMEM((eb // PEER_ACT_CHUNKS, tn), BF16)] * PEER_ACT_CHUNKS,
        compiler_params=_cparams(("arbitrary", "arbitrary")),
        name="peer_ffn",
    )(x1, h2t, lim, rank2, e1, e2, u, vt, final_norm_w.reshape(1, d))


def _permute_w_in(w_in):
    q_end = 2 * RET_QK + 2 * RET_WIDTH
    rw_end = q_end + 3 * RWKV_WIDTH
    lora_end = q_end + RWKV_COLS
    return jnp.concatenate(
        [w_in[:, :q_end], w_in[:, lora_end:], w_in[:, q_end:rw_end], w_in[:, rw_end:lora_end]], axis=1)


def kernel(x, norm1_w, w_in, ret_gn_w, ret_gn_b, rwkv_mix, rwkv_w0, rwkv_w_up, rwkv_a0, rwkv_a_up, rwkv_g_up, rwkv_k_k, rwkv_k_a, rwkv_r_k, rwkv_ln_w, rwkv_ln_b, w_ret_branch, w_rwkv_branch, w_out, norm2_w, peer_w_q, peer_keys_1, peer_keys_2, peer_u, peer_v, final_norm_w):
    b, t, d = x.shape
    n = b * t
    assert w_in.shape[0] == 1, "single-layer block: the final norm is fused into the PEER kernel"
    l = 0
    xf = x.reshape(n, d)
    zp = in_proj(xf, norm1_w[l], _permute_w_in(w_in[l]).astype(BF16))
    o_ret = retention(zp, ret_gn_w[l], ret_gn_b[l], b, t)
    pre = rwkv_pre(zp, rwkv_mix[l], rwkv_w0[l], rwkv_w_up[l], rwkv_a0[l], rwkv_a_up[l], rwkv_g_up[l],
                   rwkv_k_k[l], rwkv_k_a[l], rwkv_r_k[l], b, t)
    o_rwkv = rwkv_scan(*pre, rwkv_ln_w[l], rwkv_ln_b[l], b, t)
    x1 = merge_out(xf, zp, o_ret, o_rwkv, w_ret_branch[l], w_rwkv_branch[l], w_out[l])
    s_t, h2t = peer_query(x1, norm2_w[l], peer_w_q[l], peer_keys_1[l], peer_keys_2[l])
    lim, rank2, e1, e2 = peer_topk(s_t)
    out = peer_ffn(x1, h2t, lim, rank2, e1, e2, peer_u[l], peer_v[l], final_norm_w)
    return out.reshape(b, t, d)
```

```python
import functools

import numpy as np
import jax
import jax.numpy as jnp
from jax import lax
from jax.experimental import pallas as pl
from jax.experimental.pallas import tpu as pltpu

F32 = jnp.float32
BF16 = jnp.bfloat16

D_MODEL = 2048
CHUNK = 64
NORM_EPS = 1e-6
RET_HEADS = 4
RET_DK = 256
RET_DV = 256
RET_QK = RET_HEADS * RET_DK
RET_WIDTH = RET_HEADS * RET_DV
ROPE_BASE = 10000.0
RWKV_HEAD = 64
RWKV_WIDTH = D_MODEL // 2
RWKV_HEADS = RWKV_WIDTH // RWKV_HEAD
DECAY_LORA = 64
AAA_LORA = 64
GATE_LORA = 128
LORA_COLS = DECAY_LORA + AAA_LORA + GATE_LORA
RWKV_COLS = 3 * RWKV_WIDTH + LORA_COLS
RWKV_GN_EPS = 64e-5
IN_COLS = 2 * RET_QK + 2 * RET_WIDTH + RWKV_COLS + 2 * D_MODEL

ZP_RWKV = 2 * RET_QK + 2 * RET_WIDTH
ZP_LORA = ZP_RWKV + 3 * RWKV_WIDTH
ZP_GATE_RET = ZP_RWKV + RWKV_COLS
ZP_GATE_RWKV = ZP_GATE_RET + D_MODEL
GATE_PIECE = 256
assert ZP_GATE_RET % GATE_PIECE == 0 and ZP_GATE_RWKV % GATE_PIECE == 0 and ZP_LORA % LORA_COLS == 0

VMEM_LIMIT = 56 * 1024 * 1024


def _cparams(sem):
    return pltpu.CompilerParams(dimension_semantics=sem, vmem_limit_bytes=VMEM_LIMIT)


def _in_proj_kernel(x_ref, nw_ref, w_ref, o_ref, h_ref):
    @pl.when(pl.program_id(1) == 0)
    def _():
        x = x_ref[...]
        ms = jnp.mean(x * x, axis=-1, keepdims=True)
        h_ref[...] = (x * lax.rsqrt(ms + NORM_EPS) * nw_ref[...]).astype(BF16)

    o_ref[...] = jnp.dot(h_ref[...], w_ref[...], preferred_element_type=F32)


def in_proj(x, norm_w, w_bf16, tm=1024, tn=1280):
    n, d = x.shape
    cols = w_bf16.shape[1]
    return pl.pallas_call(
        _in_proj_kernel,
        grid=(n // tm, cols // tn),
        in_specs=[
            pl.BlockSpec((tm, d), lambda i, j: (i, 0)),
            pl.BlockSpec((1, d), lambda i, j: (0, 0)),
            pl.BlockSpec((d, tn), lambda i, j: (0, j)),
        ],
        out_specs=pl.BlockSpec((tm, tn), lambda i, j: (i, j)),
        out_shape=jax.ShapeDtypeStruct((n, cols), F32),
        scratch_shapes=[pltpu.VMEM((tm, d), BF16)],
        compiler_params=_cparams(("arbitrary", "arbitrary")),
        name="in_proj",
    )(x, norm_w.reshape(1, d), w_bf16)


def _retention_kernel(q_ref, k_ref, v_ref, g_ref, cos_ref, sin_ref, dintra_ref, qdec_ref, kdec_ref,
                      cdec_ref, gnw_ref, gnb_ref, o_ref, s_ref, qr_ref, kr_ref, acc_ref):
    tb = q_ref.shape[0]
    half = RET_DK // 2

    @pl.when(pl.program_id(2) == 0)
    def _():
        s_ref[...] = jnp.zeros_like(s_ref)

    cos = cos_ref[...]
    sin = sin_ref[...]

    def rot(z_ref):
        z1 = z_ref[:, :half]
        z2 = z_ref[:, half:]
        return jnp.concatenate([z1 * cos - z2 * sin, z2 * cos + z1 * sin], axis=-1)

    qr_ref[...] = rot(q_ref)
    kr_ref[...] = rot(k_ref) * (RET_DK ** -0.5)
    d_intra = dintra_ref[0]
    q_decay = qdec_ref[0]
    k_decay = kdec_ref[0]
    c_decay = cdec_ref[0]
    for c in range(tb // CHUNK):
        sl = pl.ds(c * CHUNK, CHUNK)
        qc = qr_ref[sl, :]
        kc = kr_ref[sl, :]
        vc = v_ref[sl, :].astype(BF16)
        scores = lax.dot_general(qc.astype(BF16), kc.astype(BF16), (((1,), (1,)), ((), ())),
                                 preferred_element_type=F32) * d_intra
        s_prev = s_ref[...]
        o = jnp.dot(scores.astype(BF16), vc, preferred_element_type=F32)
        o = o + jnp.dot((qc * q_decay).astype(BF16), s_prev.astype(BF16), preferred_element_type=F32)
        kd_t = (kc * k_decay).T.astype(BF16)
        s_ref[...] = s_prev * c_decay + jnp.dot(kd_t, vc, preferred_element_type=F32)
        acc_ref[sl, :] = o
    o = acc_ref[...]
    mu = jnp.mean(o, axis=-1, keepdims=True)
    oc = o - mu
    var = jnp.mean(oc * oc, axis=-1, keepdims=True)
    y = oc * lax.rsqrt(var + NORM_EPS) * gnw_ref[...] + gnb_ref[...]
    g = g_ref[...]
    o_ref[...] = y * (g * jax.nn.sigmoid(g))


def _retention_tables(t):
    h = RET_HEADS
    log_g = jnp.log(1.0 - jnp.exp2(-5.0 - jnp.arange(h, dtype=F32)))
    n = jnp.arange(CHUNK, dtype=F32)
    d_intra = jnp.exp(log_g[:, None, None] * jnp.abs(n[:, None] - n[None, :]))
    q_decay = jnp.exp(log_g[:, None] * (n[None, :] + 1.0))
    k_decay = jnp.exp(log_g[:, None] * (CHUNK - 1.0 - n[None, :]))
    chunk_decay = jnp.exp(log_g * CHUNK)
    q_decay = jnp.broadcast_to(q_decay[:, :, None], (h, CHUNK, RET_DK))
    k_decay = jnp.broadcast_to(k_decay[:, :, None], (h, CHUNK, RET_DK))
    chunk_decay = jnp.broadcast_to(chunk_decay[:, None, None], (h, 1, RET_DV))
    half = RET_DK // 2
    inv_freq = ROPE_BASE ** (-jnp.arange(half, dtype=F32) * 2.0 / RET_DK)
    ang = jnp.arange(t, dtype=jnp.int32).astype(F32)[:, None] * inv_freq[None, :]
    return d_intra, q_decay, k_decay, chunk_decay, jnp.cos(ang), jnp.sin(ang)


def retention(zp, gn_w, gn_b, batch, t, tb=512):
    n = zp.shape[0]
    h = RET_HEADS
    nt = t // tb
    d_intra, q_decay, k_decay, chunk_decay, cos, sin = _retention_tables(t)

    def zspec(col0):
        return pl.BlockSpec((tb, RET_DK), lambda b, hh, i: (b * nt + i, col0 + hh))

    tab = lambda shape: pl.BlockSpec((1,) + shape, lambda b, hh, i: (hh, 0, 0))
    return pl.pallas_call(
        _retention_kernel,
        grid=(batch, h, nt),
        in_specs=[
            zspec(0), zspec(h), zspec(2 * h), zspec(3 * h),
            pl.BlockSpec((tb, RET_DK // 2), lambda b, hh, i: (i, 0)),
            pl.BlockSpec((tb, RET_DK // 2), lambda b, hh, i: (i, 0)),
            tab((CHUNK, CHUNK)), tab((CHUNK, RET_DK)), tab((CHUNK, RET_DK)), tab((1, RET_DV)),
            pl.BlockSpec((1, RET_DV), lambda b, hh, i: (0, hh)),
            pl.BlockSpec((1, RET_DV), lambda b, hh, i: (0, hh)),
        ],
        out_specs=pl.BlockSpec((tb, RET_DV), lambda b, hh, i: (b * nt + i, hh)),
        out_shape=jax.ShapeDtypeStruct((n, RET_WIDTH), F32),
        scratch_shapes=[
            pltpu.VMEM((RET_DK, RET_DV), F32),
            pltpu.VMEM((tb, RET_DK), F32),
            pltpu.VMEM((tb, RET_DK), F32),
            pltpu.VMEM((tb, RET_DV), F32),
        ],
        compiler_params=_cparams(("arbitrary", "arbitrary", "arbitrary")),
        name="retention",
    )(zp, zp, zp, zp, cos, sin, d_intra, q_decay, k_decay, chunk_decay,
      gn_w.reshape(1, RET_WIDTH), gn_b.reshape(1, RET_WIDTH))


def _split2(x):
    hi = x.astype(BF16)
    lo = (x - hi.astype(F32)).astype(BF16)
    return hi, lo


def _split3(x):
    hi = x.astype(BF16)
    r = x - hi.astype(F32)
    mid = r.astype(BF16)
    lo = (r - mid.astype(F32)).astype(BF16)
    return hi, mid, lo


def _group_sum(x, gmat):
    hi, lo = _split2(x)
    lanes = gmat.shape[0]
    outs = []
    for b in range(x.shape[1] // lanes):
        sl = slice(b * lanes, (b + 1) * lanes)
        outs.append(jnp.dot(hi[:, sl], gmat, preferred_element_type=F32)
                    + jnp.dot(lo[:, sl], gmat, preferred_element_type=F32))
    return jnp.concatenate(outs, axis=-1)


def _bmm(a, b):
    return lax.dot_general(a.astype(BF16), b.astype(BF16), (((2,), (1,)), ((0,), (0,))),
                           preferred_element_type=F32)


def _bmm_nt(a, b):
    return lax.dot_general(a.astype(BF16), b.astype(BF16), (((2,), (2,)), ((0,), (0,))),
                           preferred_element_type=F32)


def _bmm_tn(a, b):
    return lax.dot_general(a.astype(BF16), b.astype(BF16), (((1,), (1,)), ((0,), (0,))),
                           preferred_element_type=F32)


RWKV_HEAD_GROUP = 16
GROUP_LANES = 128


def _rwkv_pre_kernel(r_ref, k_ref, v_ref, l_ref, pr_ref, pk_ref, pv_ref, plr_ref,
                     mixr_ref, mixk_ref, mixv_ref, mixl_ref, w0_ref, a0_ref, kk_ref, ka_ref, rk_ref,
                     wup_ref, aup_ref, gup_ref, gmat_ref, ltri_ref,
                     m_out, n_out, q_out, y_out, g_out, bv_out,
                     at_s, rt_s, bh_s, kh_s, bt_s, kt_s, v_s, gc_s, *, tiles_per_batch):
    tb = r_ref.shape[0]
    nchunk = tb // CHUNK
    first = (pl.program_id(0) % tiles_per_batch) == 0
    rowid = lax.broadcasted_iota(jnp.int32, (tb, 1), 0)

    def shift_lerp(cur_ref, prev_ref, mix_ref):
        cur = cur_ref[...]
        prev_row = jnp.where(first, 0.0, prev_ref[7:8, :])
        prev = jnp.where(rowid == 0, prev_row, pltpu.roll(cur, 1, axis=0))
        return cur + (prev - cur) * mix_ref[...]

    r = shift_lerp(r_ref, pr_ref, mixr_ref)
    k = shift_lerp(k_ref, pk_ref, mixk_ref)
    v = shift_lerp(v_ref, pv_ref, mixv_ref)
    lo = shift_lerp(l_ref, plr_ref, mixl_ref)
    xw = lo[:, :DECAY_LORA]
    xa = lo[:, DECAY_LORA:DECAY_LORA + AAA_LORA]
    xg = lo[:, DECAY_LORA + AAA_LORA:]
    wl = jnp.dot(jnp.tanh(xw).astype(BF16), wup_ref[...], preferred_element_type=F32)
    u = -(w0_ref[...] + wl)
    softplus = jnp.maximum(u, 0.0) + jnp.log(1.0 + jnp.exp(-jnp.abs(u)))
    logw = -jnp.exp(-softplus - 0.5)
    a = jax.nn.sigmoid(a0_ref[...] + jnp.dot(xa.astype(BF16), aup_ref[...], preferred_element_type=F32))
    g_out[...] = jnp.dot(jax.nn.sigmoid(xg).astype(BF16), gup_ref[...], preferred_element_type=F32)
    gmat = gmat_ref[...]
    kk = k * kk_ref[...]
    kk = kk / jnp.maximum(jnp.sqrt(_group_sum(kk * kk, gmat)), 1e-12)
    k2 = k * (1.0 + (a - 1.0) * ka_ref[...])
    bv_out[...] = _group_sum(r * k2 * rk_ref[...], gmat) * v
    beta = kk * a
    ltri = ltri_ref[...]

    def to_heads(dst, c, val):
        for h in range(RWKV_HEADS):
            dst[h, pl.ds(c * CHUNK, CHUNK), :] = val[:, h * RWKV_HEAD:(h + 1) * RWKV_HEAD]

    for c in range(nchunk):
        sl = slice(c * CHUNK, (c + 1) * CHUNK)
        lw = logw[sl]
        h1, h2, h3 = _split3(lw)
        cs = (jnp.dot(ltri, h1, preferred_element_type=F32) + jnp.dot(ltri, h2, preferred_element_type=F32)
              + jnp.dot(ltri, h3, preferred_element_type=F32))
        cs_last = cs[CHUNK - 1:CHUNK, :]
        g_t = jnp.exp(cs)
        g_tm1 = jnp.exp(cs - lw)
        inv_g = jnp.exp(-cs)
        g_end = jnp.exp(cs_last - cs)
        to_heads(at_s, c, -kk[sl] * g_tm1)
        to_heads(rt_s, c, r[sl] * g_t)
        to_heads(bh_s, c, beta[sl] * inv_g)
        to_heads(kh_s, c, k2[sl] * inv_g)
        to_heads(bt_s, c, beta[sl] * g_end)
        to_heads(kt_s, c, k2[sl] * g_end)
        to_heads(v_s, c, v[sl])
        gc = jnp.exp(cs_last)
        for h in range(RWKV_HEADS):
            gc_s[h, pl.ds(c * 8, 8), :] = jnp.broadcast_to(gc[:, h * RWKV_HEAD:(h + 1) * RWKV_HEAD], (8, RWKV_HEAD))

    G = RWKV_HEAD_GROUP
    ti = lax.broadcasted_iota(jnp.int32, (CHUNK, CHUNK), 0)
    si = lax.broadcasted_iota(jnp.int32, (CHUNK, CHUNK), 1)
    strict = (si < ti)[None]
    incl = (si <= ti)[None]
    eye = (si == ti).astype(F32)[None]

    def body(idx, carry):
        c = idx // (RWKV_HEADS // G)
        hg = idx % (RWKV_HEADS // G)
        hs = pl.ds(pl.multiple_of(hg * G, G), G)
        ts = pl.ds(pl.multiple_of(c * CHUNK, CHUNK), CHUNK)
        at = at_s[hs, ts, :]
        rt = rt_s[hs, ts, :]
        bh = bh_s[hs, ts, :]
        kh = kh_s[hs, ts, :]
        bt = bt_s[hs, ts, :]
        kt = kt_s[hs, ts, :]
        vv = v_s[hs, ts, :]
        gc = gc_s[hs, pl.ds(pl.multiple_of(c * 8, 8), 1), :]
        aab = jnp.where(strict, _bmm_nt(at, bh), 0.0)
        aak = jnp.where(strict, _bmm_nt(at, kh), 0.0)
        aqb = jnp.where(incl, _bmm_nt(rt, bh), 0.0)
        aqk = jnp.where(incl, _bmm_nt(rt, kh), 0.0)
        p = aab
        tm = eye + aab
        for _ in range(5):
            p = _bmm(p, p)
            tm = tm + _bmm(p, tm)
        akv = _bmm(aak, vv)
        wt = _bmm(tm, at)
        u0 = _bmm(tm, akv)
        q_out[hs, ts, :] = rt + _bmm(aqb, wt)
        y_out[hs, ts, :] = _bmm(aqb, u0) + _bmm(aqk, vv)
        m_out[hs, ts, :] = _bmm_tn(bt, wt) + eye * gc
        n_out[hs, ts, :] = _bmm_tn(bt, u0) + _bmm_tn(kt, vv)
        return carry

    lax.fori_loop(0, nchunk * (RWKV_HEADS // G), body, 0)


def _rwkv_consts():
    lane = np.arange(GROUP_LANES) // RWKV_HEAD
    gmat = (lane[:, None] == lane[None, :]).astype(np.float32)
    ltri = np.tril(np.ones((CHUNK, CHUNK), np.float32))
    return jnp.asarray(gmat, BF16), jnp.asarray(ltri, BF16)


def rwkv_pre(zp, mix, w0, w_up, a0, a_up, g_up, k_k, k_a, r_k, batch, t, tb=128):
    n = zp.shape[0]
    W = RWKV_WIDTH
    nt = t // tb
    gmat, ltri = _rwkv_consts()
    row = lambda p: p.reshape(1, -1)
    mix_r, mix_k, mix_v, mix_l = mix[:W], mix[W:2 * W], mix[2 * W:3 * W], mix[3 * W:]
    cb = ZP_RWKV // W
    cur = lambda j: pl.BlockSpec((tb, W), lambda i: (i, cb + j))
    prev = lambda j: pl.BlockSpec((8, W), lambda i: (jnp.maximum(i * (tb // 8) - 1, 0), cb + j))
    full = lambda shape: pl.BlockSpec(shape, lambda i: (0,) * len(shape))
    hm = pl.BlockSpec((RWKV_HEADS, tb, RWKV_HEAD), lambda i: (0, i, 0))
    nat = pl.BlockSpec((tb, W), lambda i: (i, 0))
    hm_shape = jax.ShapeDtypeStruct((RWKV_HEADS, n, RWKV_HEAD), F32)
    nat_shape = jax.ShapeDtypeStruct((n, W), F32)
    hscr = pltpu.VMEM((RWKV_HEADS, tb, RWKV_HEAD), F32)
    return pl.pallas_call(
        functools.partial(_rwkv_pre_kernel, tiles_per_batch=nt),
        grid=(n // tb,),
        in_specs=[
            cur(0), cur(1), cur(2),
            pl.BlockSpec((tb, LORA_COLS), lambda i: (i, ZP_LORA // LORA_COLS)),
            prev(0), prev(1), prev(2),
            pl.BlockSpec((8, LORA_COLS), lambda i: (jnp.maximum(i * (tb // 8) - 1, 0), ZP_LORA // LORA_COLS)),
            full((1, W)), full((1, W)), full((1, W)), full((1, LORA_COLS)),
            full((1, W)), full((1, W)), full((1, W)), full((1, W)), full((1, W)),
            full((DECAY_LORA, W)), full((AAA_LORA, W)), full((GATE_LORA, W)),
            full((GROUP_LANES, GROUP_LANES)), full((CHUNK, CHUNK)),
        ],
        out_specs=[hm, hm, hm, hm, nat, nat],
        out_shape=[hm_shape, hm_shape, hm_shape, hm_shape, nat_shape, nat_shape],
        scratch_shapes=[hscr] * 7 + [pltpu.VMEM((RWKV_HEADS, 8 * (tb // CHUNK), RWKV_HEAD), F32)],
        compiler_params=_cparams(("arbitrary",)),
        name="rwkv_pre",
    )(zp, zp, zp, zp, zp, zp, zp, zp,
      row(mix_r), row(mix_k), row(mix_v), row(mix_l), row(w0), row(a0), row(k_k), row(k_a), row(r_k),
      w_up.astype(BF16), a_up.astype(BF16), g_up.astype(BF16), gmat, ltri)


def _rwkv_scan_kernel(m_ref, n_ref, q_ref, y_ref, g_ref, bv_ref, lnw_ref, lnb_ref, gmat_ref, o_ref, s_ref, ys_ref):
    tb = g_ref.shape[0]

    @pl.when(pl.program_id(1) == 0)
    def _():
        s_ref[...] = jnp.zeros_like(s_ref)

    for c in range(tb // CHUNK):
        ts = pl.ds(c * CHUNK, CHUNK)
        s = s_ref[...]
        ys_ref[:, ts, :] = _bmm(q_ref[:, ts, :], s) + y_ref[:, ts, :]
        s_ref[...] = _bmm(m_ref[:, ts, :], s) + n_ref[:, ts, :]
    y = jnp.concatenate([ys_ref[h] for h in range(RWKV_HEADS)], axis=-1)
    gmat = gmat_ref[...]
    mu = _group_sum(y, gmat) * (1.0 / RWKV_HEAD)
    yc = y - mu
    var = _group_sum(yc * yc, gmat) * (1.0 / RWKV_HEAD)
    yn = yc * lax.rsqrt(var + RWKV_GN_EPS) * lnw_ref[...] + lnb_ref[...]
    o_ref[...] = (yn + bv_ref[...]) * g_ref[...]


def rwkv_scan(m, nn, qt, y1, g, bv, ln_w, ln_b, batch, t, tb=256):
    n = g.shape[0]
    W = RWKV_WIDTH
    nt = t // tb
    gmat, _ = _rwkv_consts()
    hm = pl.BlockSpec((RWKV_HEADS, tb, RWKV_HEAD), lambda b, i: (0, b * nt + i, 0))
    nat = pl.BlockSpec((tb, W), lambda b, i: (b * nt + i, 0))
    full = lambda shape: pl.BlockSpec(shape, lambda b, i: (0,) * len(shape))
    return pl.pallas_call(
        _rwkv_scan_kernel,
        grid=(batch, nt),
        in_specs=[hm, hm, hm, hm, nat, nat, full((1, W)), full((1, W)), full((GROUP_LANES, GROUP_LANES))],
        out_specs=nat,
        out_shape=jax.ShapeDtypeStruct((n, W), F32),
        scratch_shapes=[pltpu.VMEM((RWKV_HEADS, RWKV_HEAD, RWKV_HEAD), F32),
                        pltpu.VMEM((RWKV_HEADS, tb, RWKV_HEAD), F32)],
        compiler_params=_cparams(("arbitrary", "arbitrary")),
        name="rwkv_scan",
    )(m, nn, qt, y1, g, bv, ln_w.reshape(1, W), ln_b.reshape(1, W), gmat)


def _merge_kernel(x_ref, *refs):
    npiece = D_MODEL // GATE_PIECE
    gr_refs, gw_refs = refs[:npiece], refs[npiece:2 * npiece]
    oret_ref, orwkv_ref, wr_ref, ww_ref, wo_ref, o_ref = refs[2 * npiece:]
    gate_ret = jnp.concatenate([r[...] for r in gr_refs], axis=-1)
    gate_rwkv = jnp.concatenate([r[...] for r in gw_refs], axis=-1)
    pr = jnp.dot(oret_ref[...].astype(BF16), wr_ref[...], preferred_element_type=F32)
    pw = jnp.dot(orwkv_ref[...].astype(BF16), ww_ref[...], preferred_element_type=F32)
    merged = jax.nn.sigmoid(gate_ret) * pr + jax.nn.sigmoid(gate_rwkv) * pw
    o_ref[...] = x_ref[...] + jnp.dot(merged.astype(BF16), wo_ref[...], preferred_element_type=F32)


def merge_out(x, zp, o_ret, o_rwkv, w_ret, w_rwkv, w_out, tm=256):
    n, d = x.shape
    const = lambda shape: pl.BlockSpec(shape, lambda i: (0, 0), pipeline_mode=pl.Buffered(1))
    npiece = d // GATE_PIECE

    def gate_pieces(col0):
        return [pl.BlockSpec((tm, GATE_PIECE), functools.partial(lambda i, c: (i, c), c=col0 // GATE_PIECE + p))
                for p in range(npiece)]

    return pl.pallas_call(
        _merge_kernel,
        grid=(n // tm,),
        in_specs=[pl.BlockSpec((tm, d), lambda i: (i, 0))]
        + gate_pieces(ZP_GATE_RET) + gate_pieces(ZP_GATE_RWKV)
        + [pl.BlockSpec((tm, RET_WIDTH), lambda i: (i, 0)),
           pl.BlockSpec((tm, RWKV_WIDTH), lambda i: (i, 0)),
           const((RET_WIDTH, d)), const((RWKV_WIDTH, d)), const((d, d))],
        out_specs=pl.BlockSpec((tm, d), lambda i: (i, 0)),
        out_shape=jax.ShapeDtypeStruct((n, d), F32),
        compiler_params=_cparams(("arbitrary",)),
        name="merge_out",
    )(x, *([zp] * (2 * npiece)), o_ret, o_rwkv, w_ret.astype(BF16), w_rwkv.astype(BF16), w_out.astype(BF16))


PEER_HEADS = 8
PEER_NKEYS = 128
PEER_NEXPERTS = PEER_NKEYS * PEER_NKEYS
PEER_DQ = 256
PEER_TOPK = 16
PEER_ACT_CHUNKS = 4
PEER_PRE_PIECES = 4


def _peer_query_kernel(x_ref, nw_ref, wq_ref, k1_ref, k2_ref, s_ref, ht_ref):
    x = x_ref[...]
    ms = jnp.mean(x * x, axis=-1, keepdims=True)
    h2 = x * lax.rsqrt(ms + NORM_EPS) * nw_ref[...]
    ht_ref[...] = h2.T.astype(BF16)
    q = jnp.dot(h2.astype(BF16), wq_ref[...], preferred_element_type=F32).astype(BF16)
    half = PEER_DQ // 2
    for h in range(PEER_HEADS):
        for p, kref in enumerate((k1_ref, k2_ref)):
            qh = q[:, h * PEER_DQ + p * half: h * PEER_DQ + (p + 1) * half]
            s_ref[p, h] = lax.dot_general(kref[...], qh, (((1,), (1,)), ((), ())), preferred_element_type=F32)


def peer_query(x1, norm_w, w_q, keys_1, keys_2, tm=256):
    n, d = x1.shape
    const = lambda shape: pl.BlockSpec(shape, lambda i: (0, 0), pipeline_mode=pl.Buffered(1))
    return pl.pallas_call(
        _peer_query_kernel,
        grid=(n // tm,),
        in_specs=[
            pl.BlockSpec((tm, d), lambda i: (i, 0)),
            const((1, d)), const((d, PEER_HEADS * PEER_DQ)),
            const((PEER_NKEYS, PEER_DQ // 2)), const((PEER_NKEYS, PEER_DQ // 2)),
        ],
        out_specs=[
            pl.BlockSpec((2, PEER_HEADS, PEER_NKEYS, tm), lambda i: (0, 0, 0, i)),
            pl.BlockSpec((d, tm), lambda i: (0, i)),
        ],
        out_shape=[
            jax.ShapeDtypeStruct((2, PEER_HEADS, PEER_NKEYS, n), F32),
            jax.ShapeDtypeStruct((d, n), BF16),
        ],
        compiler_params=_cparams(("arbitrary",)),
        name="peer_query",
    )(x1, norm_w.reshape(1, d), w_q.astype(BF16), keys_1.astype(BF16), keys_2.astype(BF16))


_CAND_GROUPS = [(0, 0), (0, 8)] + [(a, 0) for a in range(1, 8)]


def _peer_topk_kernel(s_ref, lim_ref, rank2_ref, e1_ref, e2_ref, v1_s, r1_s, v2_s, r2_s):
    tn = s_ref.shape[-1]
    K = PEER_TOPK
    neg = -jnp.inf
    rowid = lax.broadcasted_iota(jnp.int32, (PEER_NKEYS, tn), 0)
    iota_k = lax.broadcasted_iota(jnp.int32, (K, tn), 0)
    nrow = 8 * (len(_CAND_GROUPS) + 1)
    r = lax.broadcasted_iota(jnp.int32, (nrow, tn), 0)
    grp, sub = r // 8, r % 8
    ca = jnp.where(grp < 2, 0, jnp.where(grp < 9, grp - 1, 8 + sub))
    cb = jnp.where(grp == 1, 8 + sub, jnp.where(grp < 9, sub, 0))
    flat = ca * K + cb
    valid = (ca + 1) * (cb + 1) <= K

    def top16(s, break_ties):
        rank = jnp.full((PEER_NKEYS, tn), K, jnp.int32)
        vals = []
        for a in range(K):
            m = jnp.max(s, axis=0, keepdims=True)
            hit = s == m
            if break_ties:
                idx = jnp.min(jnp.where(hit, rowid, PEER_NKEYS), axis=0, keepdims=True)
                hit = rowid == idx
            rank = jnp.where(hit, a, rank)
            s = jnp.where(hit, neg, s)
            vals.append(m)
        return jnp.concatenate(vals, axis=0), rank

    def head(h, carry):
        s1 = s_ref[0, h]
        s2 = s_ref[1, h]
        v1_s[...], r1_s[...] = top16(s1, False)
        v2_s[...], r2_s[...] = top16(s2, False)
        ranked = (jnp.sum((r1_s[...] < K).astype(jnp.int32), axis=0, keepdims=True)
                  + jnp.sum((r2_s[...] < K).astype(jnp.int32), axis=0, keepdims=True))
        tied = jnp.max(jnp.abs(ranked - 2 * K)) > 0

        @pl.when(tied)
        def _():
            v1_s[...], r1_s[...] = top16(s1, True)
            v2_s[...], r2_s[...] = top16(s2, True)

        v1, rank1 = v1_s[...], r1_s[...]
        v2, rank2 = v2_s[...], r2_s[...]
        pieces = [v1[a:a + 1] + v2[b0:b0 + 8] for a, b0 in _CAND_GROUPS] + [v1[8:16] + v2[0:1]]
        cand = jnp.where(valid, jnp.concatenate(pieces, axis=0), neg)
        count = jnp.zeros((K, tn), F32)
        z = jnp.zeros((1, tn), F32)
        m0 = v1[0:1] + v2[0:1]
        for _ in range(K):
            m = jnp.max(cand, axis=0, keepdims=True)
            f = jnp.min(jnp.where(cand == m, flat, K * K), axis=0, keepdims=True)
            cand = jnp.where(flat == f, neg, cand)
            count = count + jnp.where(iota_k == (f >> 4), 1.0, 0.0)
            z = z + jnp.exp(m - m0)
        inv_z = 1.0 / z
        e1_ref[h] = jnp.where(rank1 < K, jnp.exp(s1 - v1[0:1]) * inv_z, 0.0)
        e2_ref[h] = jnp.where(rank2 < K, jnp.exp(s2 - v2[0:1]), 0.0).astype(BF16)
        rank2_ref[h] = rank2.astype(F32).astype(BF16)
        lim = jnp.zeros((PEER_NKEYS, tn), F32)
        for a in range(K):
            lim = jnp.where(rank1 == a, count[a:a + 1], lim)
        lim_ref[h] = lim
        return carry

    lax.fori_loop(0, PEER_HEADS, head, 0)


def peer_topk(s_t, tn=256):
    n = s_t.shape[-1]
    spec = pl.BlockSpec((PEER_HEADS, PEER_NKEYS, tn), lambda i: (0, 0, i))
    shp = lambda dt: jax.ShapeDtypeStruct((PEER_HEADS, PEER_NKEYS, n), dt)
    return pl.pallas_call(
        _peer_topk_kernel,
        grid=(n // tn,),
        in_specs=[pl.BlockSpec((2, PEER_HEADS, PEER_NKEYS, tn), lambda i: (0, 0, 0, i))],
        out_specs=[spec, spec, spec, spec],
        out_shape=[shp(F32), shp(BF16), shp(F32), shp(BF16)],
        scratch_shapes=[pltpu.VMEM((PEER_TOPK, tn), F32), pltpu.VMEM((PEER_NKEYS, tn), jnp.int32),
                        pltpu.VMEM((PEER_TOPK, tn), F32), pltpu.VMEM((PEER_NKEYS, tn), jnp.int32)],
        compiler_params=_cparams(("arbitrary",)),
        name="peer_topk",
    )(s_t)


def _peer_ffn_kernel(x_ref, ht_ref, lim_ref, rank2_ref, e1_ref, e2_ref, u_ref, vt_ref, fnw_ref, o_ref,
                     acc_ref, *scratch):
    j = pl.program_id(1)
    eb = u_ref.shape[0]
    nsub = eb // PEER_NKEYS
    pre_refs, act_refs = scratch[:PEER_PRE_PIECES], scratch[PEER_PRE_PIECES:]
    piece = eb // len(pre_refs)
    per_chunk = nsub // len(act_refs)
    ck = per_chunk * PEER_NKEYS
    zero = jnp.zeros((), BF16)

    @pl.when(j == 0)
    def _():
        acc_ref[...] = jnp.zeros_like(acc_ref)

    for k, pre_k in enumerate(pre_refs):
        pre_k[...] = jnp.dot(u_ref[pl.ds(k * piece, piece), :], ht_ref[...], preferred_element_type=F32)
    for c, act_c in enumerate(act_refs):
        for jc in range(per_chunk):
            jj = c * per_chunk + jc
            gate = None
            for h in range(PEER_HEADS):
                lrow = lim_ref[h, jj:jj + 1, :].astype(BF16)
                erow = e1_ref[h, jj:jj + 1, :].astype(BF16)
                term = jnp.where(rank2_ref[h] < lrow, e2_ref[h], zero) * erow
                gate = term if gate is None else gate + term
            p = pre_refs[jj * PEER_NKEYS // piece][pl.ds(jj * PEER_NKEYS % piece, PEER_NKEYS), :]
            gelu = 0.5 * p * (1.0 + lax.erf(p * (2.0 ** -0.5)))
            act_c[pl.ds(jc * PEER_NKEYS, PEER_NKEYS), :] = gelu.astype(BF16) * gate
        acc_ref[...] += jnp.dot(vt_ref[:, pl.ds(c * ck, ck)], act_c[...], preferred_element_type=F32)

    @pl.when(j == pl.num_programs(1) - 1)
    def _():
        y = x_ref[...] + acc_ref[...].T
        ms = jnp.mean(y * y, axis=-1, keepdims=True)
        o_ref[...] = y * lax.rsqrt(ms + NORM_EPS) * fnw_ref[...]


def peer_ffn(x1, h2t, lim, rank2, e1, e2, expert_u, expert_v, final_norm_w, tn=512, eb=1024):
    n, d = x1.shape
    ne = expert_u.shape[0]
    u = expert_u.astype(BF16)
    vt = expert_v.T.astype(BF16)
    nsub = eb // PEER_NKEYS
    assert nsub == 8
    tok = pl.BlockSpec((PEER_HEADS, PEER_NKEYS, tn), lambda i, j: (0, 0, i))
    row = pl.BlockSpec((PEER_HEADS, nsub, tn), lambda i, j: (0, j, i))
    return pl.pallas_call(
        _peer_ffn_kernel,
        grid=(n // tn, ne // eb),
        in_specs=[
            pl.BlockSpec((tn, d), lambda i, j: (i, 0)),
            pl.BlockSpec((d, tn), lambda i, j: (0, i)),
            row, tok, row, tok,
            pl.BlockSpec((eb, d), lambda i, j: (j, 0)),
            pl.BlockSpec((d, eb), lambda i, j: (0, j)),
            pl.BlockSpec((1, d), lambda i, j: (0, 0)),
        ],
        out_specs=pl.BlockSpec((tn, d), lambda i, j: (i, 0)),
        out_shape=jax.ShapeDtypeStruct((n, d), F32),
        scratch_shapes=[pltpu.VMEM((d, tn), F32)]
        + [pltpu.VMEM((eb // PEER_PRE_PIECES, tn), F32)] * PEER_PRE_PIECES
        + [pltpu.VMEM((eb // PEER_ACT_CHUNKS, tn), BF16)] * PEER_ACT_CHUNKS,
        compiler_params=_cparams(("arbitrary", "arbitrary")),
        name="peer_ffn",
    )(x1, h2t, lim, rank2, e1, e2, u, vt, final_norm_w.reshape(1, d))


def kernel(x, norm1_w, w_in, ret_gn_w, ret_gn_b, rwkv_mix, rwkv_w0, rwkv_w_up, rwkv_a0, rwkv_a_up, rwkv_g_up, rwkv_k_k, rwkv_k_a, rwkv_r_k, rwkv_ln_w, rwkv_ln_b, w_ret_branch, w_rwkv_branch, w_out, norm2_w, peer_w_q, peer_keys_1, peer_keys_2, peer_u, peer_v, final_norm_w):
    b, t, d = x.shape
    n = b * t
    assert w_in.shape[0] == 1, "single-layer block: the final norm is fused into the PEER kernel"
    l = 0
    xf = x.reshape(n, d)
    zp = in_proj(xf, norm1_w[l], w_in[l].astype(BF16))
    o_ret = retention(zp, ret_gn_w[l], ret_gn_b[l], b, t)
    pre = rwkv_pre(zp, rwkv_mix[l], rwkv_w0[l], rwkv_w_up[l], rwkv_a0[l], rwkv_a_up[l], rwkv_g_up[l],
                   rwkv_k_k[l], rwkv_k_a[l], rwkv_r_k[l], b, t)
    o_rwkv = rwkv_scan(*pre, rwkv_ln_w[l], rwkv_ln_b[l], b, t)
    x1 = merge_out(xf, zp, o_ret, o_rwkv, w_ret_branch[l], w_rwkv_branch[l], w_out[l])
    s_t, h2t = peer_query(x1, norm2_w[l], peer_w_q[l], peer_keys_1[l], peer_keys_2[l])
    lim, rank2, e1, e2 = peer_topk(s_t)
    out = peer_ffn(x1, h2t, lim, rank2, e1, e2, peer_u[l], peer_v[l], final_norm_w)
    return out.reshape(b, t, d)
```

```python
import functools

import numpy as np
import jax
import jax.numpy as jnp
from jax import lax
from jax.experimental import pallas as pl
from jax.experimental.pallas import tpu as pltpu

F32 = jnp.float32
BF16 = jnp.bfloat16

D_MODEL = 2048
CHUNK = 64
NORM_EPS = 1e-6
RET_HEADS = 4
RET_DK = 256
RET_DV = 256
RET_QK = RET_HEADS * RET_DK
RET_WIDTH = RET_HEADS * RET_DV
ROPE_BASE = 10000.0
RWKV_HEAD = 64
RWKV_WIDTH = D_MODEL // 2
RWKV_HEADS = RWKV_WIDTH // RWKV_HEAD
DECAY_LORA = 64
AAA_LORA = 64
GATE_LORA = 128
LORA_COLS = DECAY_LORA + AAA_LORA + GATE_LORA
RWKV_COLS = 3 * RWKV_WIDTH + LORA_COLS
RWKV_GN_EPS = 64e-5
IN_COLS = 2 * RET_QK + 2 * RET_WIDTH + RWKV_COLS + 2 * D_MODEL

ZP_RWKV = 2 * RET_QK + 2 * RET_WIDTH
ZP_LORA = ZP_RWKV + 3 * RWKV_WIDTH
ZP_GATE_RET = ZP_RWKV + RWKV_COLS
ZP_GATE_RWKV = ZP_GATE_RET + D_MODEL
GATE_PIECE = 256
assert ZP_GATE_RET % GATE_PIECE == 0 and ZP_GATE_RWKV % GATE_PIECE == 0 and ZP_LORA % LORA_COLS == 0

VMEM_LIMIT = 56 * 1024 * 1024


def _cparams(sem):
    return pltpu.CompilerParams(dimension_semantics=sem, vmem_limit_bytes=VMEM_LIMIT)


def _in_proj_kernel(x_ref, nw_ref, w_ref, o_ref, h_ref):
    @pl.when(pl.program_id(1) == 0)
    def _():
        x = x_ref[...]
        ms = jnp.mean(x * x, axis=-1, keepdims=True)
        h_ref[...] = (x * lax.rsqrt(ms + NORM_EPS) * nw_ref[...]).astype(BF16)

    o_ref[...] = jnp.dot(h_ref[...], w_ref[...], preferred_element_type=F32)


def in_proj(x, norm_w, w_bf16, tm=1024, tn=1280):
    n, d = x.shape
    cols = w_bf16.shape[1]
    return pl.pallas_call(
        _in_proj_kernel,
        grid=(n // tm, cols // tn),
        in_specs=[
            pl.BlockSpec((tm, d), lambda i, j: (i, 0)),
            pl.BlockSpec((1, d), lambda i, j: (0, 0)),
            pl.BlockSpec((d, tn), lambda i, j: (0, j)),
        ],
        out_specs=pl.BlockSpec((tm, tn), lambda i, j: (i, j)),
        out_shape=jax.ShapeDtypeStruct((n, cols), F32),
        scratch_shapes=[pltpu.VMEM((tm, d), BF16)],
        compiler_params=_cparams(("arbitrary", "arbitrary")),
        name="in_proj",
    )(x, norm_w.reshape(1, d), w_bf16)


def _retention_kernel(q_ref, k_ref, v_ref, g_ref, cos_ref, sin_ref, dintra_ref, qdec_ref, kdec_ref,
                      cdec_ref, gnw_ref, gnb_ref, o_ref, s_ref, qr_ref, kr_ref, acc_ref):
    tb = q_ref.shape[0]
    half = RET_DK // 2

    @pl.when(pl.program_id(2) == 0)
    def _():
        s_ref[...] = jnp.zeros_like(s_ref)

    cos = cos_ref[...]
    sin = sin_ref[...]

    def rot(z_ref):
        z1 = z_ref[:, :half]
        z2 = z_ref[:, half:]
        return jnp.concatenate([z1 * cos - z2 * sin, z2 * cos + z1 * sin], axis=-1)

    qr_ref[...] = rot(q_ref)
    kr_ref[...] = rot(k_ref) * (RET_DK ** -0.5)
    d_intra = dintra_ref[0]
    q_decay = qdec_ref[0]
    k_decay = kdec_ref[0]
    c_decay = cdec_ref[0]
    for c in range(tb // CHUNK):
        sl = pl.ds(c * CHUNK, CHUNK)
        qc = qr_ref[sl, :]
        kc = kr_ref[sl, :]
        vc = v_ref[sl, :].astype(BF16)
        scores = lax.dot_general(qc.astype(BF16), kc.astype(BF16), (((1,), (1,)), ((), ())),
                                 preferred_element_type=F32) * d_intra
        s_prev = s_ref[...]
        o = jnp.dot(scores.astype(BF16), vc, preferred_element_type=F32)
        o = o + jnp.dot((qc * q_decay).astype(BF16), s_prev.astype(BF16), preferred_element_type=F32)
        kd_t = (kc * k_decay).T.astype(BF16)
        s_ref[...] = s_prev * c_decay + jnp.dot(kd_t, vc, preferred_element_type=F32)
        acc_ref[sl, :] = o
    o = acc_ref[...]
    mu = jnp.mean(o, axis=-1, keepdims=True)
    oc = o - mu
    var = jnp.mean(oc * oc, axis=-1, keepdims=True)
    y = oc * lax.rsqrt(var + NORM_EPS) * gnw_ref[...] + gnb_ref[...]
    g = g_ref[...]
    o_ref[...] = y * (g * jax.nn.sigmoid(g))


def _retention_tables(t):
    h = RET_HEADS
    log_g = jnp.log(1.0 - jnp.exp2(-5.0 - jnp.arange(h, dtype=F32)))
    n = jnp.arange(CHUNK, dtype=F32)
    d_intra = jnp.exp(log_g[:, None, None] * jnp.abs(n[:, None] - n[None, :]))
    q_decay = jnp.exp(log_g[:, None] * (n[None, :] + 1.0))
    k_decay = jnp.exp(log_g[:, None] * (CHUNK - 1.0 - n[None, :]))
    chunk_decay = jnp.exp(log_g * CHUNK)
    q_decay = jnp.broadcast_to(q_decay[:, :, None], (h, CHUNK, RET_DK))
    k_decay = jnp.broadcast_to(k_decay[:, :, None], (h, CHUNK, RET_DK))
    chunk_decay = jnp.broadcast_to(chunk_decay[:, None, None], (h, 1, RET_DV))
    half = RET_DK // 2
    inv_freq = ROPE_BASE ** (-jnp.arange(half, dtype=F32) * 2.0 / RET_DK)
    ang = jnp.arange(t, dtype=jnp.int32).astype(F32)[:, None] * inv_freq[None, :]
    return d_intra, q_decay, k_decay, chunk_decay, jnp.cos(ang), jnp.sin(ang)


def retention(zp, gn_w, gn_b, batch, t, tb=512):
    n = zp.shape[0]
    h = RET_HEADS
    nt = t // tb
    d_intra, q_decay, k_decay, chunk_decay, cos, sin = _retention_tables(t)

    def zspec(col0):
        return pl.BlockSpec((tb, RET_DK), lambda b, hh, i: (b * nt + i, col0 + hh))

    tab = lambda shape: pl.BlockSpec((1,) + shape, lambda b, hh, i: (hh, 0, 0))
    return pl.pallas_call(
        _retention_kernel,
        grid=(batch, h, nt),
        in_specs=[
            zspec(0), zspec(h), zspec(2 * h), zspec(3 * h),
            pl.BlockSpec((tb, RET_DK // 2), lambda b, hh, i: (i, 0)),
            pl.BlockSpec((tb, RET_DK // 2), lambda b, hh, i: (i, 0)),
            tab((CHUNK, CHUNK)), tab((CHUNK, RET_DK)), tab((CHUNK, RET_DK)), tab((1, RET_DV)),
            pl.BlockSpec((1, RET_DV), lambda b, hh, i: (0, hh)),
            pl.BlockSpec((1, RET_DV), lambda b, hh, i: (0, hh)),
        ],
        out_specs=pl.BlockSpec((tb, RET_DV), lambda b, hh, i: (b * nt + i, hh)),
        out_shape=jax.ShapeDtypeStruct((n, RET_WIDTH), F32),
        scratch_shapes=[
            pltpu.VMEM((RET_DK, RET_DV), F32),
            pltpu.VMEM((tb, RET_DK), F32),
            pltpu.VMEM((tb, RET_DK), F32),
            pltpu.VMEM((tb, RET_DV), F32),
        ],
        compiler_params=_cparams(("arbitrary", "arbitrary", "arbitrary")),
        name="retention",
    )(zp, zp, zp, zp, cos, sin, d_intra, q_decay, k_decay, chunk_decay,
      gn_w.reshape(1, RET_WIDTH), gn_b.reshape(1, RET_WIDTH))


def _split2(x):
    hi = x.astype(BF16)
    lo = (x - hi.astype(F32)).astype(BF16)
    return hi, lo


def _split3(x):
    hi = x.astype(BF16)
    r = x - hi.astype(F32)
    mid = r.astype(BF16)
    lo = (r - mid.astype(F32)).astype(BF16)
    return hi, mid, lo


def _group_sum(x, gmat):
    hi, lo = _split2(x)
    lanes = gmat.shape[0]
    outs = []
    for b in range(x.shape[1] // lanes):
        sl = slice(b * lanes, (b + 1) * lanes)
        outs.append(jnp.dot(hi[:, sl], gmat, preferred_element_type=F32)
                    + jnp.dot(lo[:, sl], gmat, preferred_element_type=F32))
    return jnp.concatenate(outs, axis=-1)


def _bmm(a, b):
    return lax.dot_general(a.astype(BF16), b.astype(BF16), (((2,), (1,)), ((0,), (0,))),
                           preferred_element_type=F32)


def _bmm_nt(a, b):
    return lax.dot_general(a.astype(BF16), b.astype(BF16), (((2,), (2,)), ((0,), (0,))),
                           preferred_element_type=F32)


def _bmm_tn(a, b):
    return lax.dot_general(a.astype(BF16), b.astype(BF16), (((1,), (1,)), ((0,), (0,))),
                           preferred_element_type=F32)


RWKV_HEAD_GROUP = 16
GROUP_LANES = 128


def _rwkv_pre_kernel(r_ref, k_ref, v_ref, l_ref, pr_ref, pk_ref, pv_ref, plr_ref,
                     mixr_ref, mixk_ref, mixv_ref, mixl_ref, w0_ref, a0_ref, kk_ref, ka_ref, rk_ref,
                     wup_ref, aup_ref, gup_ref, gmat_ref, ltri_ref,
                     m_out, n_out, q_out, y_out, g_out, bv_out,
                     at_s, rt_s, bh_s, kh_s, bt_s, kt_s, v_s, gc_s, *, tiles_per_batch):
    tb = r_ref.shape[0]
    nchunk = tb // CHUNK
    first = (pl.program_id(0) % tiles_per_batch) == 0
    rowid = lax.broadcasted_iota(jnp.int32, (tb, 1), 0)

    def shift_lerp(cur_ref, prev_ref, mix_ref):
        cur = cur_ref[...]
        prev_row = jnp.where(first, 0.0, prev_ref[7:8, :])
        prev = jnp.where(rowid == 0, prev_row, pltpu.roll(cur, 1, axis=0))
        return cur + (prev - cur) * mix_ref[...]

    r = shift_lerp(r_ref, pr_ref, mixr_ref)
    k = shift_lerp(k_ref, pk_ref, mixk_ref)
    v = shift_lerp(v_ref, pv_ref, mixv_ref)
    lo = shift_lerp(l_ref, plr_ref, mixl_ref)
    xw = lo[:, :DECAY_LORA]
    xa = lo[:, DECAY_LORA:DECAY_LORA + AAA_LORA]
    xg = lo[:, DECAY_LORA + AAA_LORA:]
    wl = jnp.dot(jnp.tanh(xw).astype(BF16), wup_ref[...], preferred_element_type=F32)
    u = -(w0_ref[...] + wl)
    softplus = jnp.maximum(u, 0.0) + jnp.log(1.0 + jnp.exp(-jnp.abs(u)))
    logw = -jnp.exp(-softplus - 0.5)
    a = jax.nn.sigmoid(a0_ref[...] + jnp.dot(xa.astype(BF16), aup_ref[...], preferred_element_type=F32))
    g_out[...] = jnp.dot(jax.nn.sigmoid(xg).astype(BF16), gup_ref[...], preferred_element_type=F32)
    gmat = gmat_ref[...]
    kk = k * kk_ref[...]
    kk = kk / jnp.maximum(jnp.sqrt(_group_sum(kk * kk, gmat)), 1e-12)
    k2 = k * (1.0 + (a - 1.0) * ka_ref[...])
    bv_out[...] = _group_sum(r * k2 * rk_ref[...], gmat) * v
    beta = kk * a
    ltri = ltri_ref[...]

    def to_heads(dst, c, val):
        for h in range(RWKV_HEADS):
            dst[h, pl.ds(c * CHUNK, CHUNK), :] = val[:, h * RWKV_HEAD:(h + 1) * RWKV_HEAD]

    for c in range(nchunk):
        sl = slice(c * CHUNK, (c + 1) * CHUNK)
        lw = logw[sl]
        h1, h2, h3 = _split3(lw)
        cs = (jnp.dot(ltri, h1, preferred_element_type=F32) + jnp.dot(ltri, h2, preferred_element_type=F32)
              + jnp.dot(ltri, h3, preferred_element_type=F32))
        cs_last = cs[CHUNK - 1:CHUNK, :]
        g_t = jnp.exp(cs)
        g_tm1 = jnp.exp(cs - lw)
        inv_g = jnp.exp(-cs)
        g_end = jnp.exp(cs_last - cs)
        to_heads(at_s, c, -kk[sl] * g_tm1)
        to_heads(rt_s, c, r[sl] * g_t)
        to_heads(bh_s, c, beta[sl] * inv_g)
        to_heads(kh_s, c, k2[sl] * inv_g)
        to_heads(bt_s, c, beta[sl] * g_end)
        to_heads(kt_s, c, k2[sl] * g_end)
        to_heads(v_s, c, v[sl])
        gc = jnp.exp(cs_last)
        for h in range(RWKV_HEADS):
            gc_s[h, pl.ds(c * 8, 8), :] = jnp.broadcast_to(gc[:, h * RWKV_HEAD:(h + 1) * RWKV_HEAD], (8, RWKV_HEAD))

    G = RWKV_HEAD_GROUP
    ti = lax.broadcasted_iota(jnp.int32, (CHUNK, CHUNK), 0)
    si = lax.broadcasted_iota(jnp.int32, (CHUNK, CHUNK), 1)
    strict = (si < ti)[None]
    incl = (si <= ti)[None]
    eye = (si == ti).astype(F32)[None]

    def body(idx, carry):
        c = idx // (RWKV_HEADS // G)
        hg = idx % (RWKV_HEADS // G)
        hs = pl.ds(pl.multiple_of(hg * G, G), G)
        ts = pl.ds(pl.multiple_of(c * CHUNK, CHUNK), CHUNK)
        at = at_s[hs, ts, :]
        rt = rt_s[hs, ts, :]
        bh = bh_s[hs, ts, :]
        kh = kh_s[hs, ts, :]
        bt = bt_s[hs, ts, :]
        kt = kt_s[hs, ts, :]
        vv = v_s[hs, ts, :]
        gc = gc_s[hs, pl.ds(pl.multiple_of(c * 8, 8), 1), :]
        aab = jnp.where(strict, _bmm_nt(at, bh), 0.0)
        aak = jnp.where(strict, _bmm_nt(at, kh), 0.0)
        aqb = jnp.where(incl, _bmm_nt(rt, bh), 0.0)
        aqk = jnp.where(incl, _bmm_nt(rt, kh), 0.0)
        p = aab
        tm = eye + aab
        for _ in range(5):
            p = _bmm(p, p)
            tm = tm + _bmm(p, tm)
        akv = _bmm(aak, vv)
        wt = _bmm(tm, at)
        u0 = _bmm(tm, akv)
        q_out[hs, ts, :] = (rt + _bmm(aqb, wt)).astype(BF16)
        y_out[hs, ts, :] = _bmm(aqb, u0) + _bmm(aqk, vv)
        m_out[hs, ts, :] = (_bmm_tn(bt, wt) + eye * gc).astype(BF16)
        n_out[hs, ts, :] = _bmm_tn(bt, u0) + _bmm_tn(kt, vv)
        return carry

    lax.fori_loop(0, nchunk * (RWKV_HEADS // G), body, 0)


def _rwkv_consts():
    lane = np.arange(GROUP_LANES) // RWKV_HEAD
    gmat = (lane[:, None] == lane[None, :]).astype(np.float32)
    ltri = np.tril(np.ones((CHUNK, CHUNK), np.float32))
    return jnp.asarray(gmat, BF16), jnp.asarray(ltri, BF16)


def rwkv_pre(zp, mix, w0, w_up, a0, a_up, g_up, k_k, k_a, r_k, batch, t, tb=128):
    n = zp.shape[0]
    W = RWKV_WIDTH
    nt = t // tb
    gmat, ltri = _rwkv_consts()
    row = lambda p: p.reshape(1, -1)
    mix_r, mix_k, mix_v, mix_l = mix[:W], mix[W:2 * W], mix[2 * W:3 * W], mix[3 * W:]
    cb = ZP_RWKV // W
    cur = lambda j: pl.BlockSpec((tb, W), lambda i: (i, cb + j))
    prev = lambda j: pl.BlockSpec((8, W), lambda i: (jnp.maximum(i * (tb // 8) - 1, 0), cb + j))
    full = lambda shape: pl.BlockSpec(shape, lambda i: (0,) * len(shape))
    hm = pl.BlockSpec((RWKV_HEADS, tb, RWKV_HEAD), lambda i: (0, i, 0))
    nat = pl.BlockSpec((tb, W), lambda i: (i, 0))
    hm_shape = jax.ShapeDtypeStruct((RWKV_HEADS, n, RWKV_HEAD), F32)
    hm_bf16 = jax.ShapeDtypeStruct((RWKV_HEADS, n, RWKV_HEAD), BF16)
    nat_shape = jax.ShapeDtypeStruct((n, W), F32)
    hscr = pltpu.VMEM((RWKV_HEADS, tb, RWKV_HEAD), F32)
    return pl.pallas_call(
        functools.partial(_rwkv_pre_kernel, tiles_per_batch=nt),
        grid=(n // tb,),
        in_specs=[
            cur(0), cur(1), cur(2),
            pl.BlockSpec((tb, LORA_COLS), lambda i: (i, ZP_LORA // LORA_COLS)),
            prev(0), prev(1), prev(2),
            pl.BlockSpec((8, LORA_COLS), lambda i: (jnp.maximum(i * (tb // 8) - 1, 0), ZP_LORA // LORA_COLS)),
            full((1, W)), full((1, W)), full((1, W)), full((1, LORA_COLS)),
            full((1, W)), full((1, W)), full((1, W)), full((1, W)), full((1, W)),
            full((DECAY_LORA, W)), full((AAA_LORA, W)), full((GATE_LORA, W)),
            full((GROUP_LANES, GROUP_LANES)), full((CHUNK, CHUNK)),
        ],
        out_specs=[hm, hm, hm, hm, nat, nat],
        out_shape=[hm_bf16, hm_shape, hm_bf16, hm_shape, nat_shape, nat_shape],
        scratch_shapes=[hscr] * 7 + [pltpu.VMEM((RWKV_HEADS, 8 * (tb // CHUNK), RWKV_HEAD), F32)],
        compiler_params=_cparams(("arbitrary",)),
        name="rwkv_pre",
    )(zp, zp, zp, zp, zp, zp, zp, zp,
      row(mix_r), row(mix_k), row(mix_v), row(mix_l), row(w0), row(a0), row(k_k), row(k_a), row(r_k),
      w_up.astype(BF16), a_up.astype(BF16), g_up.astype(BF16), gmat, ltri)


def _rwkv_scan_kernel(m_ref, n_ref, q_ref, y_ref, g_ref, bv_ref, lnw_ref, lnb_ref, gmat_ref, o_ref, s_ref, ys_ref):
    tb = g_ref.shape[0]

    @pl.when(pl.program_id(1) == 0)
    def _():
        s_ref[...] = jnp.zeros_like(s_ref)

    for c in range(tb // CHUNK):
        ts = pl.ds(c * CHUNK, CHUNK)
        s = s_ref[...]
        ys_ref[:, ts, :] = _bmm(q_ref[:, ts, :], s) + y_ref[:, ts, :]
        s_ref[...] = _bmm(m_ref[:, ts, :], s) + n_ref[:, ts, :]
    y = jnp.concatenate([ys_ref[h] for h in range(RWKV_HEADS)], axis=-1)
    gmat = gmat_ref[...]
    mu = _group_sum(y, gmat) * (1.0 / RWKV_HEAD)
    yc = y - mu
    var = _group_sum(yc * yc, gmat) * (1.0 / RWKV_HEAD)
    yn = yc * lax.rsqrt(var + RWKV_GN_EPS) * lnw_ref[...] + lnb_ref[...]
    o_ref[...] = (yn + bv_ref[...]) * g_ref[...]


def rwkv_scan(m, nn, qt, y1, g, bv, ln_w, ln_b, batch, t, tb=256):
    n = g.shape[0]
    W = RWKV_WIDTH
    nt = t // tb
    gmat, _ = _rwkv_consts()
    hm = pl.BlockSpec((RWKV_HEADS, tb, RWKV_HEAD), lambda b, i: (0, b * nt + i, 0))
    nat = pl.BlockSpec((tb, W), lambda b, i: (b * nt + i, 0))
    full = lambda shape: pl.BlockSpec(shape, lambda b, i: (0,) * len(shape))
    return pl.pallas_call(
        _rwkv_scan_kernel,
        grid=(batch, nt),
        in_specs=[hm, hm, hm, hm, nat, nat, full((1, W)), full((1, W)), full((GROUP_LANES, GROUP_LANES))],
        out_specs=nat,
        out_shape=jax.ShapeDtypeStruct((n, W), F32),
        scratch_shapes=[pltpu.VMEM((RWKV_HEADS, RWKV_HEAD, RWKV_HEAD), F32),
                        pltpu.VMEM((RWKV_HEADS, tb, RWKV_HEAD), F32)],
        compiler_params=_cparams(("arbitrary", "arbitrary")),
        name="rwkv_scan",
    )(m, nn, qt, y1, g, bv, ln_w.reshape(1, W), ln_b.reshape(1, W), gmat)


def _merge_kernel(x_ref, *refs):
    npiece = D_MODEL // GATE_PIECE
    gr_refs, gw_refs = refs[:npiece], refs[npiece:2 * npiece]
    oret_ref, orwkv_ref, wr_ref, ww_ref, wo_ref, o_ref = refs[2 * npiece:]
    gate_ret = jnp.concatenate([r[...] for r in gr_refs], axis=-1)
    gate_rwkv = jnp.concatenate([r[...] for r in gw_refs], axis=-1)
    pr = jnp.dot(oret_ref[...].astype(BF16), wr_ref[...], preferred_element_type=F32)
    pw = jnp.dot(orwkv_ref[...].astype(BF16), ww_ref[...], preferred_element_type=F32)
    merged = jax.nn.sigmoid(gate_ret) * pr + jax.nn.sigmoid(gate_rwkv) * pw
    o_ref[...] = x_ref[...] + jnp.dot(merged.astype(BF16), wo_ref[...], preferred_element_type=F32)


def merge_out(x, zp, o_ret, o_rwkv, w_ret, w_rwkv, w_out, tm=256):
    n, d = x.shape
    const = lambda shape: pl.BlockSpec(shape, lambda i: (0, 0), pipeline_mode=pl.Buffered(1))
    npiece = d // GATE_PIECE

    def gate_pieces(col0):
        return [pl.BlockSpec((tm, GATE_PIECE), functools.partial(lambda i, c: (i, c), c=col0 // GATE_PIECE + p))
                for p in range(npiece)]

    return pl.pallas_call(
        _merge_kernel,
        grid=(n // tm,),
        in_specs=[pl.BlockSpec((tm, d), lambda i: (i, 0))]
        + gate_pieces(ZP_GATE_RET) + gate_pieces(ZP_GATE_RWKV)
        + [pl.BlockSpec((tm, RET_WIDTH), lambda i: (i, 0)),
           pl.BlockSpec((tm, RWKV_WIDTH), lambda i: (i, 0)),
           const((RET_WIDTH, d)), const((RWKV_WIDTH, d)), const((d, d))],
        out_specs=pl.BlockSpec((tm, d), lambda i: (i, 0)),
        out_shape=jax.ShapeDtypeStruct((n, d), F32),
        compiler_params=_cparams(("arbitrary",)),
        name="merge_out",
    )(x, *([zp] * (2 * npiece)), o_ret, o_rwkv, w_ret.astype(BF16), w_rwkv.astype(BF16), w_out.astype(BF16))


PEER_HEADS = 8
PEER_NKEYS = 128
PEER_NEXPERTS = PEER_NKEYS * PEER_NKEYS
PEER_DQ = 256
PEER_TOPK = 16
PEER_ACT_CHUNKS = 4
PEER_PRE_PIECES = 2


def _peer_query_kernel(x_ref, nw_ref, wq_ref, k1_ref, k2_ref, s_ref, ht_ref):
    x = x_ref[...]
    ms = jnp.mean(x * x, axis=-1, keepdims=True)
    h2 = x * lax.rsqrt(ms + NORM_EPS) * nw_ref[...]
    ht_ref[...] = h2.T.astype(BF16)
    q = jnp.dot(h2.astype(BF16), wq_ref[...], preferred_element_type=F32).astype(BF16)
    half = PEER_DQ // 2
    for h in range(PEER_HEADS):
        for p, kref in enumerate((k1_ref, k2_ref)):
            qh = q[:, h * PEER_DQ + p * half: h * PEER_DQ + (p + 1) * half]
            s_ref[p, h] = lax.dot_general(kref[...], qh, (((1,), (1,)), ((), ())), preferred_element_type=F32)


def peer_query(x1, norm_w, w_q, keys_1, keys_2, tm=256):
    n, d = x1.shape
    const = lambda shape: pl.BlockSpec(shape, lambda i: (0, 0), pipeline_mode=pl.Buffered(1))
    return pl.pallas_call(
        _peer_query_kernel,
        grid=(n // tm,),
        in_specs=[
            pl.BlockSpec((tm, d), lambda i: (i, 0)),
            const((1, d)), const((d, PEER_HEADS * PEER_DQ)),
            const((PEER_NKEYS, PEER_DQ // 2)), const((PEER_NKEYS, PEER_DQ // 2)),
        ],
        out_specs=[
            pl.BlockSpec((2, PEER_HEADS, PEER_NKEYS, tm), lambda i: (0, 0, 0, i)),
            pl.BlockSpec((d, tm), lambda i: (0, i)),
        ],
        out_shape=[
            jax.ShapeDtypeStruct((2, PEER_HEADS, PEER_NKEYS, n), F32),
            jax.ShapeDtypeStruct((d, n), BF16),
        ],
        compiler_params=_cparams(("arbitrary",)),
        name="peer_query",
    )(x1, norm_w.reshape(1, d), w_q.astype(BF16), keys_1.astype(BF16), keys_2.astype(BF16))


_CAND_GROUPS = [(0, 0), (0, 8)] + [(a, 0) for a in range(1, 8)]


def _peer_topk_kernel(s_ref, lim_ref, rank2_ref, e1_ref, e2_ref, v1_s, r1_s, v2_s, r2_s, cnt_s, z_s):
    tn = s_ref.shape[-1]
    K = PEER_TOPK
    neg = -jnp.inf
    rowid = lax.broadcasted_iota(jnp.int32, (PEER_NKEYS, tn), 0)
    nrow = 8 * (len(_CAND_GROUPS) + 1)
    r = lax.broadcasted_iota(jnp.int32, (nrow, tn), 0)
    grp, sub = r // 8, r % 8
    ca = jnp.where(grp < 2, 0, jnp.where(grp < 9, grp - 1, 8 + sub))
    cb = jnp.where(grp == 1, 8 + sub, jnp.where(grp < 9, sub, 0))
    flat = ca * K + cb
    valid = (ca + 1) * (cb + 1) <= K

    def top16(s, break_ties):
        rank = jnp.full((PEER_NKEYS, tn), K, jnp.int32)
        vals = []
        for a in range(K):
            m = jnp.max(s, axis=0, keepdims=True)
            hit = s == m
            if break_ties:
                idx = jnp.min(jnp.where(hit, rowid, PEER_NKEYS), axis=0, keepdims=True)
                hit = rowid == idx
            rank = jnp.where(hit, a, rank)
            s = jnp.where(hit, neg, s)
            vals.append(m)
        return jnp.concatenate(vals, axis=0), rank

    def head(h, carry):
        s1 = s_ref[0, h]
        s2 = s_ref[1, h]
        v1_s[...], r1_s[...] = top16(s1, False)
        v2_s[...], r2_s[...] = top16(s2, False)
        ranked = (jnp.sum((r1_s[...] < K).astype(jnp.int32), axis=0, keepdims=True)
                  + jnp.sum((r2_s[...] < K).astype(jnp.int32), axis=0, keepdims=True))
        tied = jnp.max(jnp.abs(ranked - 2 * K)) > 0

        @pl.when(tied)
        def _():
            v1_s[...], r1_s[...] = top16(s1, True)
            v2_s[...], r2_s[...] = top16(s2, True)

        v1, rank1 = v1_s[...], r1_s[...]
        v2, rank2 = v2_s[...], r2_s[...]
        pieces = [v1[a:a + 1] + v2[b0:b0 + 8] for a, b0 in _CAND_GROUPS] + [v1[8:16] + v2[0:1]]
        cand = jnp.where(valid, jnp.concatenate(pieces, axis=0), neg)
        m0 = v1[0:1] + v2[0:1]

        def best16(cand, break_ties):
            z = jnp.zeros((1, tn), F32)
            for _ in range(K):
                m = jnp.max(cand, axis=0, keepdims=True)
                hit = cand == m
                if break_ties:
                    f = jnp.min(jnp.where(hit, flat, K * K), axis=0, keepdims=True)
                    hit = flat == f
                cand = jnp.where(hit, neg, cand)
                z = z + jnp.exp(m - m0)
            chosen = jnp.where(jnp.logical_and(valid, cand == neg), 1.0, 0.0)
            per_a = [jnp.sum(chosen[0:16], axis=0, keepdims=True)]
            per_a += [jnp.sum(chosen[8 * (a + 1):8 * (a + 2)], axis=0, keepdims=True) for a in range(1, 8)]
            return jnp.concatenate(per_a + [chosen[8 * 9:8 * 10]], axis=0), z

        cnt_s[...], z_s[...] = best16(cand, False)
        tied2 = jnp.max(jnp.abs(jnp.sum(cnt_s[...], axis=0, keepdims=True) - K)) > 0

        @pl.when(tied2)
        def _():
            cnt_s[...], z_s[...] = best16(cand, True)

        count = cnt_s[...]
        inv_z = 1.0 / z_s[...]
        e1_ref[h] = jnp.where(rank1 < K, jnp.exp(s1 - v1[0:1]) * inv_z, 0.0)
        e2_ref[h] = jnp.where(rank2 < K, jnp.exp(s2 - v2[0:1]), 0.0).astype(BF16)
        rank2_ref[h] = rank2.astype(F32).astype(BF16)
        lim = jnp.zeros((PEER_NKEYS, tn), F32)
        for a in range(K):
            lim = jnp.where(rank1 == a, count[a:a + 1], lim)
        lim_ref[h] = lim
        return carry

    lax.fori_loop(0, PEER_HEADS, head, 0)


def peer_topk(s_t, tn=256):
    n = s_t.shape[-1]
    spec = pl.BlockSpec((PEER_HEADS, PEER_NKEYS, tn), lambda i: (0, 0, i))
    shp = lambda dt: jax.ShapeDtypeStruct((PEER_HEADS, PEER_NKEYS, n), dt)
    return pl.pallas_call(
        _peer_topk_kernel,
        grid=(n // tn,),
        in_specs=[pl.BlockSpec((2, PEER_HEADS, PEER_NKEYS, tn), lambda i: (0, 0, 0, i))],
        out_specs=[spec, spec, spec, spec],
        out_shape=[shp(F32), shp(BF16), shp(F32), shp(BF16)],
        scratch_shapes=[pltpu.VMEM((PEER_TOPK, tn), F32), pltpu.VMEM((PEER_NKEYS, tn), jnp.int32),
                        pltpu.VMEM((PEER_TOPK, tn), F32), pltpu.VMEM((PEER_NKEYS, tn), jnp.int32),
                        pltpu.VMEM((PEER_TOPK, tn), F32), pltpu.VMEM((1, tn), F32)],
        compiler_params=_cparams(("arbitrary",)),
        name="peer_topk",
    )(s_t)


def _peer_ffn_kernel(x_ref, ht_ref, lim_ref, rank2_ref, e1_ref, e2_ref, u_ref, vt_ref, fnw_ref, o_ref,
                     acc_ref, *scratch):
    j = pl.program_id(1)
    eb = u_ref.shape[0]
    nsub = eb // PEER_NKEYS
    pre_refs, act_refs = scratch[:PEER_PRE_PIECES], scratch[PEER_PRE_PIECES:]
    piece = eb // len(pre_refs)
    per_chunk = nsub // len(act_refs)
    ck = per_chunk * PEER_NKEYS
    zero = jnp.zeros((), BF16)

    @pl.when(j == 0)
    def _():
        acc_ref[...] = jnp.zeros_like(acc_ref)

    for k, pre_k in enumerate(pre_refs):
        pre_k[...] = jnp.dot(u_ref[pl.ds(k * piece, piece), :], ht_ref[...], preferred_element_type=F32)
    for c, act_c in enumerate(act_refs):
        for jc in range(per_chunk):
            jj = c * per_chunk + jc
            gate = None
            for h in range(PEER_HEADS):
                lrow = lim_ref[h, jj:jj + 1, :].astype(BF16)
                erow = e1_ref[h, jj:jj + 1, :].astype(BF16)
                term = jnp.where(rank2_ref[h] < lrow, e2_ref[h], zero) * erow
                gate = term if gate is None else gate + term
            p = pre_refs[jj * PEER_NKEYS // piece][pl.ds(jj * PEER_NKEYS % piece, PEER_NKEYS), :]
            gelu = 0.5 * p * (1.0 + lax.erf(p * (2.0 ** -0.5)))
            act_c[pl.ds(jc * PEER_NKEYS, PEER_NKEYS), :] = gelu.astype(BF16) * gate
        acc_ref[...] += jnp.dot(vt_ref[:, pl.ds(c * ck, ck)], act_c[...], preferred_element_type=F32)

    @pl.when(j == pl.num_programs(1) - 1)
    def _():
        y = x_ref[...] + acc_ref[...].T
        ms = jnp.mean(y * y, axis=-1, keepdims=True)
        o_ref[...] = y * lax.rsqrt(ms + NORM_EPS) * fnw_ref[...]


def peer_ffn(x1, h2t, lim, rank2, e1, e2, expert_u, expert_v, final_norm_w, tn=512, eb=1024):
    n, d = x1.shape
    ne = expert_u.shape[0]
    u = expert_u.astype(BF16)
    vt = expert_v.T.astype(BF16)
    nsub = eb // PEER_NKEYS
    assert nsub == 8
    tok = pl.BlockSpec((PEER_HEADS, PEER_NKEYS, tn), lambda i, j: (0, 0, i))
    row = pl.BlockSpec((PEER_HEADS, nsub, tn), lambda i, j: (0, j, i))
    return pl.pallas_call(
        _peer_ffn_kernel,
        grid=(n // tn, ne // eb),
        in_specs=[
            pl.BlockSpec((tn, d), lambda i, j: (i, 0)),
            pl.BlockSpec((d, tn), lambda i, j: (0, i)),
            row, tok, row, tok,
            pl.BlockSpec((eb, d), lambda i, j: (j, 0)),
            pl.BlockSpec((d, eb), lambda i, j: (0, j)),
            pl.BlockSpec((1, d), lambda i, j: (0, 0)),
        ],
        out_specs=pl.BlockSpec((tn, d), lambda i, j: (i, 0)),
        out_shape=jax.ShapeDtypeStruct((n, d), F32),
        scratch_shapes=[pltpu.VMEM((d, tn), F32)]
        + [pltpu.VMEM((eb // PEER_PRE_PIECES, tn), F32)] * PEER_PRE_PIECES
        + [pltpu.VMEM((eb // PEER_ACT_CHUNKS, tn), BF16)] * PEER_ACT_CHUNKS,
        compiler_params=_cparams(("arbitrary", "arbitrary")),
        name="peer_ffn",
    )(x1, h2t, lim, rank2, e1, e2, u, vt, final_norm_w.reshape(1, d))


def kernel(x, norm1_w, w_in, ret_gn_w, ret_gn_b, rwkv_mix, rwkv_w0, rwkv_w_up, rwkv_a0, rwkv_a_up, rwkv_g_up, rwkv_k_k, rwkv_k_a, rwkv_r_k, rwkv_ln_w, rwkv_ln_b, w_ret_branch, w_rwkv_branch, w_out, norm2_w, peer_w_q, peer_keys_1, peer_keys_2, peer_u, peer_v, final_norm_w):
    b, t, d = x.shape
    n = b * t
    assert w_in.shape[0] == 1, "single-layer block: the final norm is fused into the PEER kernel"
    l = 0
    xf = x.reshape(n, d)
    zp = in_proj(xf, norm1_w[l], w_in[l].astype(BF16))
    o_ret = retention(zp, ret_gn_w[l], ret_gn_b[l], b, t)
    pre = rwkv_pre(zp, rwkv_mix[l], rwkv_w0[l], rwkv_w_up[l], rwkv_a0[l], rwkv_a_up[l], rwkv_g_up[l],
                   rwkv_k_k[l], rwkv_k_a[l], rwkv_r_k[l], b, t)
    o_rwkv = rwkv_scan(*pre, rwkv_ln_w[l], rwkv_ln_b[l], b, t)
    x1 = merge_out(xf, zp, o_ret, o_rwkv, w_ret_branch[l], w_rwkv_branch[l], w_out[l])
    s_t, h2t = peer_query(x1, norm2_w[l], peer_w_q[l], peer_keys_1[l], peer_keys_2[l])
    lim, rank2, e1, e2 = peer_topk(s_t)
    out = peer_ffn(x1, h2t, lim, rank2, e1, e2, peer_u[l], peer_v[l], final_norm_w)
    return out.reshape(b, t, d)
```

```python
import functools

import numpy as np
import jax
import jax.numpy as jnp
from jax import lax
from jax.experimental import pallas as pl
from jax.experimental.pallas import tpu as pltpu

F32 = jnp.float32
BF16 = jnp.bfloat16

D_MODEL = 2048
CHUNK = 64
NORM_EPS = 1e-6
RET_HEADS = 4
RET_DK = 256
RET_DV = 256
RET_QK = RET_HEADS * RET_DK
RET_WIDTH = RET_HEADS * RET_DV
ROPE_BASE = 10000.0
RWKV_HEAD = 64
RWKV_WIDTH = D_MODEL // 2
RWKV_HEADS = RWKV_WIDTH // RWKV_HEAD
DECAY_LORA = 64
AAA_LORA = 64
GATE_LORA = 128
LORA_COLS = DECAY_LORA + AAA_LORA + GATE_LORA
RWKV_COLS = 3 * RWKV_WIDTH + LORA_COLS
RWKV_GN_EPS = 64e-5
IN_COLS = 2 * RET_QK + 2 * RET_WIDTH + RWKV_COLS + 2 * D_MODEL

ZP_RWKV = 2 * RET_QK + 2 * RET_WIDTH
ZP_LORA = ZP_RWKV + 3 * RWKV_WIDTH
ZP_GATE_RET = ZP_RWKV + RWKV_COLS
ZP_GATE_RWKV = ZP_GATE_RET + D_MODEL
GATE_PIECE = 256
assert ZP_GATE_RET % GATE_PIECE == 0 and ZP_GATE_RWKV % GATE_PIECE == 0 and ZP_LORA % LORA_COLS == 0

VMEM_LIMIT = 56 * 1024 * 1024


def _cparams(sem):
    return pltpu.CompilerParams(dimension_semantics=sem, vmem_limit_bytes=VMEM_LIMIT)


def _in_proj_kernel(x_ref, nw_ref, w_ref, o_ref, h_ref):
    @pl.when(pl.program_id(1) == 0)
    def _():
        x = x_ref[...]
        ms = jnp.mean(x * x, axis=-1, keepdims=True)
        h_ref[...] = (x * lax.rsqrt(ms + NORM_EPS) * nw_ref[...]).astype(BF16)

    o_ref[...] = jnp.dot(h_ref[...], w_ref[...], preferred_element_type=F32)


def in_proj(x, norm_w, w_bf16, tm=1024, tn=1280):
    n, d = x.shape
    cols = w_bf16.shape[1]
    return pl.pallas_call(
        _in_proj_kernel,
        grid=(n // tm, cols // tn),
        in_specs=[
            pl.BlockSpec((tm, d), lambda i, j: (i, 0)),
            pl.BlockSpec((1, d), lambda i, j: (0, 0)),
            pl.BlockSpec((d, tn), lambda i, j: (0, j)),
        ],
        out_specs=pl.BlockSpec((tm, tn), lambda i, j: (i, j)),
        out_shape=jax.ShapeDtypeStruct((n, cols), F32),
        scratch_shapes=[pltpu.VMEM((tm, d), BF16)],
        compiler_params=_cparams(("arbitrary", "arbitrary")),
        name="in_proj",
    )(x, norm_w.reshape(1, d), w_bf16)


def _retention_kernel(q_ref, k_ref, v_ref, g_ref, cos_ref, sin_ref, dintra_ref, qdec_ref, kdec_ref,
                      cdec_ref, gnw_ref, gnb_ref, o_ref, s_ref, qr_ref, kr_ref, acc_ref):
    tb = q_ref.shape[0]
    half = RET_DK // 2

    @pl.when(pl.program_id(2) == 0)
    def _():
        s_ref[...] = jnp.zeros_like(s_ref)

    cos = cos_ref[...]
    sin = sin_ref[...]

    def rot(z_ref):
        z1 = z_ref[:, :half]
        z2 = z_ref[:, half:]
        return jnp.concatenate([z1 * cos - z2 * sin, z2 * cos + z1 * sin], axis=-1)

    qr_ref[...] = rot(q_ref)
    kr_ref[...] = rot(k_ref) * (RET_DK ** -0.5)
    d_intra = dintra_ref[0]
    q_decay = qdec_ref[0]
    k_decay = kdec_ref[0]
    c_decay = cdec_ref[0]
    for c in range(tb // CHUNK):
        sl = pl.ds(c * CHUNK, CHUNK)
        qc = qr_ref[sl, :]
        kc = kr_ref[sl, :]
        vc = v_ref[sl, :].astype(BF16)
        scores = lax.dot_general(qc.astype(BF16), kc.astype(BF16), (((1,), (1,)), ((), ())),
                                 preferred_element_type=F32) * d_intra
        s_prev = s_ref[...]
        o = jnp.dot(scores.astype(BF16), vc, preferred_element_type=F32)
        o = o + jnp.dot((qc * q_decay).astype(BF16), s_prev.astype(BF16), preferred_element_type=F32)
        kd_t = (kc * k_decay).T.astype(BF16)
        s_ref[...] = s_prev * c_decay + jnp.dot(kd_t, vc, preferred_element_type=F32)
        acc_ref[sl, :] = o
    o = acc_ref[...]
    mu = jnp.mean(o, axis=-1, keepdims=True)
    oc = o - mu
    var = jnp.mean(oc * oc, axis=-1, keepdims=True)
    y = oc * lax.rsqrt(var + NORM_EPS) * gnw_ref[...] + gnb_ref[...]
    g = g_ref[...]
    o_ref[...] = y * (g * jax.nn.sigmoid(g))


def _retention_tables(t):
    h = RET_HEADS
    log_g = jnp.log(1.0 - jnp.exp2(-5.0 - jnp.arange(h, dtype=F32)))
    n = jnp.arange(CHUNK, dtype=F32)
    d_intra = jnp.exp(log_g[:, None, None] * jnp.abs(n[:, None] - n[None, :]))
    q_decay = jnp.exp(log_g[:, None] * (n[None, :] + 1.0))
    k_decay = jnp.exp(log_g[:, None] * (CHUNK - 1.0 - n[None, :]))
    chunk_decay = jnp.exp(log_g * CHUNK)
    q_decay = jnp.broadcast_to(q_decay[:, :, None], (h, CHUNK, RET_DK))
    k_decay = jnp.broadcast_to(k_decay[:, :, None], (h, CHUNK, RET_DK))
    chunk_decay = jnp.broadcast_to(chunk_decay[:, None, None], (h, 1, RET_DV))
    half = RET_DK // 2
    inv_freq = ROPE_BASE ** (-jnp.arange(half, dtype=F32) * 2.0 / RET_DK)
    ang = jnp.arange(t, dtype=jnp.int32).astype(F32)[:, None] * inv_freq[None, :]
    return d_intra, q_decay, k_decay, chunk_decay, jnp.cos(ang), jnp.sin(ang)


def retention(zp, gn_w, gn_b, batch, t, tb=512):
    n = zp.shape[0]
    h = RET_HEADS
    nt = t // tb
    d_intra, q_decay, k_decay, chunk_decay, cos, sin = _retention_tables(t)

    def zspec(col0):
        return pl.BlockSpec((tb, RET_DK), lambda b, hh, i: (b * nt + i, col0 + hh))

    tab = lambda shape: pl.BlockSpec((1,) + shape, lambda b, hh, i: (hh, 0, 0))
    return pl.pallas_call(
        _retention_kernel,
        grid=(batch, h, nt),
        in_specs=[
            zspec(0), zspec(h), zspec(2 * h), zspec(3 * h),
            pl.BlockSpec((tb, RET_DK // 2), lambda b, hh, i: (i, 0)),
            pl.BlockSpec((tb, RET_DK // 2), lambda b, hh, i: (i, 0)),
            tab((CHUNK, CHUNK)), tab((CHUNK, RET_DK)), tab((CHUNK, RET_DK)), tab((1, RET_DV)),
            pl.BlockSpec((1, RET_DV), lambda b, hh, i: (0, hh)),
            pl.BlockSpec((1, RET_DV), lambda b, hh, i: (0, hh)),
        ],
        out_specs=pl.BlockSpec((tb, RET_DV), lambda b, hh, i: (b * nt + i, hh)),
        out_shape=jax.ShapeDtypeStruct((n, RET_WIDTH), F32),
        scratch_shapes=[
            pltpu.VMEM((RET_DK, RET_DV), F32),
            pltpu.VMEM((tb, RET_DK), F32),
            pltpu.VMEM((tb, RET_DK), F32),
            pltpu.VMEM((tb, RET_DV), F32),
        ],
        compiler_params=_cparams(("arbitrary", "arbitrary", "arbitrary")),
        name="retention",
    )(zp, zp, zp, zp, cos, sin, d_intra, q_decay, k_decay, chunk_decay,
      gn_w.reshape(1, RET_WIDTH), gn_b.reshape(1, RET_WIDTH))


def _split2(x):
    hi = x.astype(BF16)
    lo = (x - hi.astype(F32)).astype(BF16)
    return hi, lo


def _split3(x):
    hi = x.astype(BF16)
    r = x - hi.astype(F32)
    mid = r.astype(BF16)
    lo = (r - mid.astype(F32)).astype(BF16)
    return hi, mid, lo


def _group_sum(x, gmat):
    hi, lo = _split2(x)
    lanes = gmat.shape[0]
    outs = []
    for b in range(x.shape[1] // lanes):
        sl = slice(b * lanes, (b + 1) * lanes)
        outs.append(jnp.dot(hi[:, sl], gmat, preferred_element_type=F32)
                    + jnp.dot(lo[:, sl], gmat, preferred_element_type=F32))
    return jnp.concatenate(outs, axis=-1)


def _bmm(a, b):
    return lax.dot_general(a.astype(BF16), b.astype(BF16), (((2,), (1,)), ((0,), (0,))),
                           preferred_element_type=F32)


def _bmm_nt(a, b):
    return lax.dot_general(a.astype(BF16), b.astype(BF16), (((2,), (2,)), ((0,), (0,))),
                           preferred_element_type=F32)


def _bmm_tn(a, b):
    return lax.dot_general(a.astype(BF16), b.astype(BF16), (((1,), (1,)), ((0,), (0,))),
                           preferred_element_type=F32)


RWKV_HEAD_GROUP = 16
GROUP_LANES = 128


def _rwkv_pre_kernel(r_ref, k_ref, v_ref, l_ref, pr_ref, pk_ref, pv_ref, plr_ref,
                     mixr_ref, mixk_ref, mixv_ref, mixl_ref, w0_ref, a0_ref, kk_ref, ka_ref, rk_ref,
                     wup_ref, aup_ref, gup_ref, gmat_ref, ltri_ref,
                     m_out, n_out, q_out, y_out, g_out, bv_out,
                     at_s, rt_s, bh_s, kh_s, bt_s, kt_s, v_s, gc_s, *, tiles_per_batch):
    tb = r_ref.shape[0]
    nchunk = tb // CHUNK
    first = (pl.program_id(0) % tiles_per_batch) == 0
    rowid = lax.broadcasted_iota(jnp.int32, (tb, 1), 0)

    def shift_lerp(cur_ref, prev_ref, mix_ref):
        cur = cur_ref[...]
        prev_row = jnp.where(first, 0.0, prev_ref[7:8, :])
        prev = jnp.where(rowid == 0, prev_row, pltpu.roll(cur, 1, axis=0))
        return cur + (prev - cur) * mix_ref[...]

    r = shift_lerp(r_ref, pr_ref, mixr_ref)
    k = shift_lerp(k_ref, pk_ref, mixk_ref)
    v = shift_lerp(v_ref, pv_ref, mixv_ref)
    lo = shift_lerp(l_ref, plr_ref, mixl_ref)
    xw = lo[:, :DECAY_LORA]
    xa = lo[:, DECAY_LORA:DECAY_LORA + AAA_LORA]
    xg = lo[:, DECAY_LORA + AAA_LORA:]
    wl = jnp.dot(jnp.tanh(xw).astype(BF16), wup_ref[...], preferred_element_type=F32)
    u = -(w0_ref[...] + wl)
    softplus = jnp.maximum(u, 0.0) + jnp.log(1.0 + jnp.exp(-jnp.abs(u)))
    logw = -jnp.exp(-softplus - 0.5)
    a = jax.nn.sigmoid(a0_ref[...] + jnp.dot(xa.astype(BF16), aup_ref[...], preferred_element_type=F32))
    g_out[...] = jnp.dot(jax.nn.sigmoid(xg).astype(BF16), gup_ref[...], preferred_element_type=F32)
    gmat = gmat_ref[...]
    kk = k * kk_ref[...]
    kk = kk / jnp.maximum(jnp.sqrt(_group_sum(kk * kk, gmat)), 1e-12)
    k2 = k * (1.0 + (a - 1.0) * ka_ref[...])
    bv_out[...] = _group_sum(r * k2 * rk_ref[...], gmat) * v
    beta = kk * a
    ltri = ltri_ref[...]

    def to_heads(dst, c, val):
        for h in range(RWKV_HEADS):
            dst[h, pl.ds(c * CHUNK, CHUNK), :] = val[:, h * RWKV_HEAD:(h + 1) * RWKV_HEAD]

    for c in range(nchunk):
        sl = slice(c * CHUNK, (c + 1) * CHUNK)
        lw = logw[sl]
        h1, h2, h3 = _split3(lw)
        cs = (jnp.dot(ltri, h1, preferred_element_type=F32) + jnp.dot(ltri, h2, preferred_element_type=F32)
              + jnp.dot(ltri, h3, preferred_element_type=F32))
        cs_last = cs[CHUNK - 1:CHUNK, :]
        g_t = jnp.exp(cs)
        g_tm1 = jnp.exp(cs - lw)
        inv_g = jnp.exp(-cs)
        g_end = jnp.exp(cs_last - cs)
        to_heads(at_s, c, -kk[sl] * g_tm1)
        to_heads(rt_s, c, r[sl] * g_t)
        to_heads(bh_s, c, beta[sl] * inv_g)
        to_heads(kh_s, c, k2[sl] * inv_g)
        to_heads(bt_s, c, beta[sl] * g_end)
        to_heads(kt_s, c, k2[sl] * g_end)
        to_heads(v_s, c, v[sl])
        gc = jnp.exp(cs_last)
        for h in range(RWKV_HEADS):
            gc_s[h, pl.ds(c * 8, 8), :] = jnp.broadcast_to(gc[:, h * RWKV_HEAD:(h + 1) * RWKV_HEAD], (8, RWKV_HEAD))

    G = RWKV_HEAD_GROUP
    ti = lax.broadcasted_iota(jnp.int32, (CHUNK, CHUNK), 0)
    si = lax.broadcasted_iota(jnp.int32, (CHUNK, CHUNK), 1)
    strict = (si < ti)[None]
    incl = (si <= ti)[None]
    eye = (si == ti).astype(F32)[None]

    def body(idx, carry):
        c = idx // (RWKV_HEADS // G)
        hg = idx % (RWKV_HEADS // G)
        hs = pl.ds(pl.multiple_of(hg * G, G), G)
        ts = pl.ds(pl.multiple_of(c * CHUNK, CHUNK), CHUNK)
        at = at_s[hs, ts, :]
        rt = rt_s[hs, ts, :]
        bh = bh_s[hs, ts, :]
        kh = kh_s[hs, ts, :]
        bt = bt_s[hs, ts, :]
        kt = kt_s[hs, ts, :]
        vv = v_s[hs, ts, :]
        gc = gc_s[hs, pl.ds(pl.multiple_of(c * 8, 8), 1), :]
        aab = jnp.where(strict, _bmm_nt(at, bh), 0.0)
        aak = jnp.where(strict, _bmm_nt(at, kh), 0.0)
        aqb = jnp.where(incl, _bmm_nt(rt, bh), 0.0)
        aqk = jnp.where(incl, _bmm_nt(rt, kh), 0.0)
        p = aab
        tm = eye + aab
        for _ in range(5):
            p = _bmm(p, p)
            tm = tm + _bmm(p, tm)
        akv = _bmm(aak, vv)
        wt = _bmm(tm, at)
        u0 = _bmm(tm, akv)
        q_out[hs, ts, :] = (rt + _bmm(aqb, wt)).astype(BF16)
        y_out[hs, ts, :] = _bmm(aqb, u0) + _bmm(aqk, vv)
        m_out[hs, ts, :] = (_bmm_tn(bt, wt) + eye * gc).astype(BF16)
        n_out[hs, ts, :] = _bmm_tn(bt, u0) + _bmm_tn(kt, vv)
        return carry

    lax.fori_loop(0, nchunk * (RWKV_HEADS // G), body, 0)


def _rwkv_consts():
    lane = np.arange(GROUP_LANES) // RWKV_HEAD
    gmat = (lane[:, None] == lane[None, :]).astype(np.float32)
    ltri = np.tril(np.ones((CHUNK, CHUNK), np.float32))
    return jnp.asarray(gmat, BF16), jnp.asarray(ltri, BF16)


def rwkv_pre(zp, mix, w0, w_up, a0, a_up, g_up, k_k, k_a, r_k, batch, t, tb=128):
    n = zp.shape[0]
    W = RWKV_WIDTH
    nt = t // tb
    gmat, ltri = _rwkv_consts()
    row = lambda p: p.reshape(1, -1)
    mix_r, mix_k, mix_v, mix_l = mix[:W], mix[W:2 * W], mix[2 * W:3 * W], mix[3 * W:]
    cb = ZP_RWKV // W
    cur = lambda j: pl.BlockSpec((tb, W), lambda i: (i, cb + j))
    prev = lambda j: pl.BlockSpec((8, W), lambda i: (jnp.maximum(i * (tb // 8) - 1, 0), cb + j))
    full = lambda shape: pl.BlockSpec(shape, lambda i: (0,) * len(shape))
    hm = pl.BlockSpec((RWKV_HEADS, tb, RWKV_HEAD), lambda i: (0, i, 0))
    nat = pl.BlockSpec((tb, W), lambda i: (i, 0))
    hm_shape = jax.ShapeDtypeStruct((RWKV_HEADS, n, RWKV_HEAD), F32)
    hm_bf16 = jax.ShapeDtypeStruct((RWKV_HEADS, n, RWKV_HEAD), BF16)
    nat_shape = jax.ShapeDtypeStruct((n, W), F32)
    hscr = pltpu.VMEM((RWKV_HEADS, tb, RWKV_HEAD), F32)
    return pl.pallas_call(
        functools.partial(_rwkv_pre_kernel, tiles_per_batch=nt),
        grid=(n // tb,),
        in_specs=[
            cur(0), cur(1), cur(2),
            pl.BlockSpec((tb, LORA_COLS), lambda i: (i, ZP_LORA // LORA_COLS)),
            prev(0), prev(1), prev(2),
            pl.BlockSpec((8, LORA_COLS), lambda i: (jnp.maximum(i * (tb // 8) - 1, 0), ZP_LORA // LORA_COLS)),
            full((1, W)), full((1, W)), full((1, W)), full((1, LORA_COLS)),
            full((1, W)), full((1, W)), full((1, W)), full((1, W)), full((1, W)),
            full((DECAY_LORA, W)), full((AAA_LORA, W)), full((GATE_LORA, W)),
            full((GROUP_LANES, GROUP_LANES)), full((CHUNK, CHUNK)),
        ],
        out_specs=[hm, hm, hm, hm, nat, nat],
        out_shape=[hm_bf16, hm_shape, hm_bf16, hm_shape, nat_shape, nat_shape],
        scratch_shapes=[hscr] * 7 + [pltpu.VMEM((RWKV_HEADS, 8 * (tb // CHUNK), RWKV_HEAD), F32)],
        compiler_params=_cparams(("arbitrary",)),
        name="rwkv_pre",
    )(zp, zp, zp, zp, zp, zp, zp, zp,
      row(mix_r), row(mix_k), row(mix_v), row(mix_l), row(w0), row(a0), row(k_k), row(k_a), row(r_k),
      w_up.astype(BF16), a_up.astype(BF16), g_up.astype(BF16), gmat, ltri)


def _rwkv_scan_kernel(m_ref, n_ref, q_ref, y_ref, g_ref, bv_ref, lnw_ref, lnb_ref, gmat_ref, o_ref, s_ref, ys_ref):
    tb = g_ref.shape[0]

    @pl.when(pl.program_id(1) == 0)
    def _():
        s_ref[...] = jnp.zeros_like(s_ref)

    for c in range(tb // CHUNK):
        ts = pl.ds(c * CHUNK, CHUNK)
        s = s_ref[...]
        ys_ref[:, ts, :] = _bmm(q_ref[:, ts, :], s) + y_ref[:, ts, :]
        s_ref[...] = _bmm(m_ref[:, ts, :], s) + n_ref[:, ts, :]
    y = jnp.concatenate([ys_ref[h] for h in range(RWKV_HEADS)], axis=-1)
    gmat = gmat_ref[...]
    mu = _group_sum(y, gmat) * (1.0 / RWKV_HEAD)
    yc = y - mu
    var = _group_sum(yc * yc, gmat) * (1.0 / RWKV_HEAD)
    yn = yc * lax.rsqrt(var + RWKV_GN_EPS) * lnw_ref[...] + lnb_ref[...]
    o_ref[...] = (yn + bv_ref[...]) * g_ref[...]


def rwkv_scan(m, nn, qt, y1, g, bv, ln_w, ln_b, batch, t, tb=256):
    n = g.shape[0]
    W = RWKV_WIDTH
    nt = t // tb
    gmat, _ = _rwkv_consts()
    hm = pl.BlockSpec((RWKV_HEADS, tb, RWKV_HEAD), lambda b, i: (0, b * nt + i, 0))
    nat = pl.BlockSpec((tb, W), lambda b, i: (b * nt + i, 0))
    full = lambda shape: pl.BlockSpec(shape, lambda b, i: (0,) * len(shape))
    return pl.pallas_call(
        _rwkv_scan_kernel,
        grid=(batch, nt),
        in_specs=[hm, hm, hm, hm, nat, nat, full((1, W)), full((1, W)), full((GROUP_LANES, GROUP_LANES))],
        out_specs=nat,
        out_shape=jax.ShapeDtypeStruct((n, W), F32),
        scratch_shapes=[pltpu.VMEM((RWKV_HEADS, RWKV_HEAD, RWKV_HEAD), F32),
                        pltpu.VMEM((RWKV_HEADS, tb, RWKV_HEAD), F32)],
        compiler_params=_cparams(("arbitrary", "arbitrary")),
        name="rwkv_scan",
    )(m, nn, qt, y1, g, bv, ln_w.reshape(1, W), ln_b.reshape(1, W), gmat)


PEER_HEADS = 8
PEER_NKEYS = 128
PEER_NEXPERTS = PEER_NKEYS * PEER_NKEYS
PEER_DQ = 256
PEER_TOPK = 16
PEER_ACT_CHUNKS = 4
PEER_PRE_PIECES = 2


def _merge_query_kernel(x_ref, *refs):
    npiece = D_MODEL // GATE_PIECE
    gr_refs, gw_refs = refs[:npiece], refs[npiece:2 * npiece]
    (oret_ref, orwkv_ref, wr_ref, ww_ref, wo_ref, nw_ref, wq_ref, k1_ref, k2_ref,
     x1_ref, s_ref, ht_ref) = refs[2 * npiece:]
    gate_ret = jnp.concatenate([r[...] for r in gr_refs], axis=-1)
    gate_rwkv = jnp.concatenate([r[...] for r in gw_refs], axis=-1)
    pr = jnp.dot(oret_ref[...].astype(BF16), wr_ref[...], preferred_element_type=F32)
    pw = jnp.dot(orwkv_ref[...].astype(BF16), ww_ref[...], preferred_element_type=F32)
    merged = jax.nn.sigmoid(gate_ret) * pr + jax.nn.sigmoid(gate_rwkv) * pw
    x1 = x_ref[...] + jnp.dot(merged.astype(BF16), wo_ref[...], preferred_element_type=F32)
    x1_ref[...] = x1
    ms = jnp.mean(x1 * x1, axis=-1, keepdims=True)
    h2 = x1 * lax.rsqrt(ms + NORM_EPS) * nw_ref[...]
    ht_ref[...] = h2.T.astype(BF16)
    q = jnp.dot(h2.astype(BF16), wq_ref[...], preferred_element_type=F32).astype(BF16)
    half = PEER_DQ // 2
    for h in range(PEER_HEADS):
        for p, kref in enumerate((k1_ref, k2_ref)):
            qh = q[:, h * PEER_DQ + p * half: h * PEER_DQ + (p + 1) * half]
            s_ref[p, h] = lax.dot_general(kref[...], qh, (((1,), (1,)), ((), ())), preferred_element_type=F32)


def merge_query(x, zp, o_ret, o_rwkv, w_ret, w_rwkv, w_out, norm_w, w_q, keys_1, keys_2, tm=256):
    n, d = x.shape
    const = lambda shape: pl.BlockSpec(shape, lambda i: (0, 0), pipeline_mode=pl.Buffered(1))
    npiece = d // GATE_PIECE

    def gate_pieces(col0):
        return [pl.BlockSpec((tm, GATE_PIECE), functools.partial(lambda i, c: (i, c), c=col0 // GATE_PIECE + p))
                for p in range(npiece)]

    return pl.pallas_call(
        _merge_query_kernel,
        grid=(n // tm,),
        in_specs=[pl.BlockSpec((tm, d), lambda i: (i, 0))]
        + gate_pieces(ZP_GATE_RET) + gate_pieces(ZP_GATE_RWKV)
        + [pl.BlockSpec((tm, RET_WIDTH), lambda i: (i, 0)),
           pl.BlockSpec((tm, RWKV_WIDTH), lambda i: (i, 0)),
           const((RET_WIDTH, d)), const((RWKV_WIDTH, d)), const((d, d)),
           const((1, d)), const((d, PEER_HEADS * PEER_DQ)),
           const((PEER_NKEYS, PEER_DQ // 2)), const((PEER_NKEYS, PEER_DQ // 2))],
        out_specs=[
            pl.BlockSpec((tm, d), lambda i: (i, 0)),
            pl.BlockSpec((2, PEER_HEADS, PEER_NKEYS, tm), lambda i: (0, 0, 0, i)),
            pl.BlockSpec((d, tm), lambda i: (0, i)),
        ],
        out_shape=[
            jax.ShapeDtypeStruct((n, d), F32),
            jax.ShapeDtypeStruct((2, PEER_HEADS, PEER_NKEYS, n), F32),
            jax.ShapeDtypeStruct((d, n), BF16),
        ],
        compiler_params=_cparams(("arbitrary",)),
        name="merge_query",
    )(x, *([zp] * (2 * npiece)), o_ret, o_rwkv, w_ret.astype(BF16), w_rwkv.astype(BF16), w_out.astype(BF16),
      norm_w.reshape(1, d), w_q.astype(BF16), keys_1.astype(BF16), keys_2.astype(BF16))


_CAND_GROUPS = [(0, 0), (0, 8)] + [(a, 0) for a in range(1, 8)]


def _peer_topk_kernel(s_ref, lim_ref, rank2_ref, e1_ref, e2_ref, v1_s, r1_s, v2_s, r2_s, cnt_s, z_s):
    tn = s_ref.shape[-1]
    K = PEER_TOPK
    neg = -jnp.inf
    rowid = lax.broadcasted_iota(jnp.int32, (PEER_NKEYS, tn), 0)
    nrow = 8 * (len(_CAND_GROUPS) + 1)
    r = lax.broadcasted_iota(jnp.int32, (nrow, tn), 0)
    grp, sub = r // 8, r % 8
    ca = jnp.where(grp < 2, 0, jnp.where(grp < 9, grp - 1, 8 + sub))
    cb = jnp.where(grp == 1, 8 + sub, jnp.where(grp < 9, sub, 0))
    flat = ca * K + cb
    valid = (ca + 1) * (cb + 1) <= K

    def top16(s, break_ties):
        rank = jnp.full((PEER_NKEYS, tn), K, jnp.int32)
        vals = []
        for a in range(K):
            m = jnp.max(s, axis=0, keepdims=True)
            hit = s == m
            if break_ties:
                idx = jnp.min(jnp.where(hit, rowid, PEER_NKEYS), axis=0, keepdims=True)
                hit = rowid == idx
            rank = jnp.where(hit, a, rank)
            s = jnp.where(hit, neg, s)
            vals.append(m)
        return jnp.concatenate(vals, axis=0), rank

    def head(h, carry):
        s1 = s_ref[0, h]
        s2 = s_ref[1, h]
        v1_s[...], r1_s[...] = top16(s1, False)
        v2_s[...], r2_s[...] = top16(s2, False)
        ranked = (jnp.sum((r1_s[...] < K).astype(jnp.int32), axis=0, keepdims=True)
                  + jnp.sum((r2_s[...] < K).astype(jnp.int32), axis=0, keepdims=True))
        tied = jnp.max(jnp.abs(ranked - 2 * K)) > 0

        @pl.when(tied)
        def _():
            v1_s[...], r1_s[...] = top16(s1, True)
            v2_s[...], r2_s[...] = top16(s2, True)

        v1, rank1 = v1_s[...], r1_s[...]
        v2, rank2 = v2_s[...], r2_s[...]
        pieces = [v1[a:a + 1] + v2[b0:b0 + 8] for a, b0 in _CAND_GROUPS] + [v1[8:16] + v2[0:1]]
        cand = jnp.where(valid, jnp.concatenate(pieces, axis=0), neg)
        m0 = v1[0:1] + v2[0:1]

        def best16(cand, break_ties):
            z = jnp.zeros((1, tn), F32)
            for _ in range(K):
                m = jnp.max(cand, axis=0, keepdims=True)
                hit = cand == m
                if break_ties:
                    f = jnp.min(jnp.where(hit, flat, K * K), axis=0, keepdims=True)
                    hit = flat == f
                cand = jnp.where(hit, neg, cand)
                z = z + jnp.exp(m - m0)
            chosen = jnp.where(jnp.logical_and(valid, cand == neg), 1.0, 0.0)
            per_a = [jnp.sum(chosen[0:16], axis=0, keepdims=True)]
            per_a += [jnp.sum(chosen[8 * (a + 1):8 * (a + 2)], axis=0, keepdims=True) for a in range(1, 8)]
            return jnp.concatenate(per_a + [chosen[8 * 9:8 * 10]], axis=0), z

        cnt_s[...], z_s[...] = best16(cand, False)
        tied2 = jnp.max(jnp.abs(jnp.sum(cnt_s[...], axis=0, keepdims=True) - K)) > 0

        @pl.when(tied2)
        def _():
            cnt_s[...], z_s[...] = best16(cand, True)

        count = cnt_s[...]
        inv_z = 1.0 / z_s[...]
        e1_ref[h] = jnp.where(rank1 < K, jnp.exp(s1 - v1[0:1]) * inv_z, 0.0)
        e2_ref[h] = jnp.where(rank2 < K, jnp.exp(s2 - v2[0:1]), 0.0).astype(BF16)
        rank2_ref[h] = rank2.astype(F32).astype(BF16)
        lim = jnp.zeros((PEER_NKEYS, tn), F32)
        for a in range(K):
            lim = jnp.where(rank1 == a, count[a:a + 1], lim)
        lim_ref[h] = lim
        return carry

    lax.fori_loop(0, PEER_HEADS, head, 0)


def peer_topk(s_t, tn=256):
    n = s_t.shape[-1]
    spec = pl.BlockSpec((PEER_HEADS, PEER_NKEYS, tn), lambda i: (0, 0, i))
    shp = lambda dt: jax.ShapeDtypeStruct((PEER_HEADS, PEER_NKEYS, n), dt)
    return pl.pallas_call(
        _peer_topk_kernel,
        grid=(n // tn,),
        in_specs=[pl.BlockSpec((2, PEER_HEADS, PEER_NKEYS, tn), lambda i: (0, 0, 0, i))],
        out_specs=[spec, spec, spec, spec],
        out_shape=[shp(F32), shp(BF16), shp(F32), shp(BF16)],
        scratch_shapes=[pltpu.VMEM((PEER_TOPK, tn), F32), pltpu.VMEM((PEER_NKEYS, tn), jnp.int32),
                        pltpu.VMEM((PEER_TOPK, tn), F32), pltpu.VMEM((PEER_NKEYS, tn), jnp.int32),
                        pltpu.VMEM((PEER_TOPK, tn), F32), pltpu.VMEM((1, tn), F32)],
        compiler_params=_cparams(("arbitrary",)),
        name="peer_topk",
    )(s_t)


def _peer_ffn_kernel(x_ref, ht_ref, lim_ref, rank2_ref, e1_ref, e2_ref, u_ref, vt_ref, fnw_ref, o_ref,
                     acc_ref, *scratch):
    j = pl.program_id(1)
    eb = u_ref.shape[0]
    nsub = eb // PEER_NKEYS
    pre_refs, act_refs = scratch[:PEER_PRE_PIECES], scratch[PEER_PRE_PIECES:]
    piece = eb // len(pre_refs)
    per_chunk = nsub // len(act_refs)
    ck = per_chunk * PEER_NKEYS
    zero = jnp.zeros((), BF16)

    @pl.when(j == 0)
    def _():
        acc_ref[...] = jnp.zeros_like(acc_ref)

    for k, pre_k in enumerate(pre_refs):
        pre_k[...] = jnp.dot(u_ref[pl.ds(k * piece, piece), :], ht_ref[...], preferred_element_type=F32)
    for c, act_c in enumerate(act_refs):
        for jc in range(per_chunk):
            jj = c * per_chunk + jc
            gate = None
            for h in range(PEER_HEADS):
                lrow = lim_ref[h, jj:jj + 1, :].astype(BF16)
                erow = e1_ref[h, jj:jj + 1, :].astype(BF16)
                term = jnp.where(rank2_ref[h] < lrow, e2_ref[h], zero) * erow
                gate = term if gate is None else gate + term
            p = pre_refs[jj * PEER_NKEYS // piece][pl.ds(jj * PEER_NKEYS % piece, PEER_NKEYS), :]
            gelu = 0.5 * p * (1.0 + lax.erf(p * (2.0 ** -0.5)))
            act_c[pl.ds(jc * PEER_NKEYS, PEER_NKEYS), :] = gelu.astype(BF16) * gate
        acc_ref[...] += jnp.dot(vt_ref[:, pl.ds(c * ck, ck)], act_c[...], preferred_element_type=F32)

    @pl.when(j == pl.num_programs(1) - 1)
    def _():
        y = x_ref[...] + acc_ref[...].T
        ms = jnp.mean(y * y, axis=-1, keepdims=True)
        o_ref[...] = y * lax.rsqrt(ms + NORM_EPS) * fnw_ref[...]


def peer_ffn(x1, h2t, lim, rank2, e1, e2, expert_u, expert_v, final_norm_w, tn=512, eb=1024):
    n, d = x1.shape
    ne = expert_u.shape[0]
    u = expert_u.astype(BF16)
    vt = expert_v.T.astype(BF16)
    nsub = eb // PEER_NKEYS
    assert nsub == 8
    tok = pl.BlockSpec((PEER_HEADS, PEER_NKEYS, tn), lambda i, j: (0, 0, i))
    row = pl.BlockSpec((PEER_HEADS, nsub, tn), lambda i, j: (0, j, i))
    return pl.pallas_call(
        _peer_ffn_kernel,
        grid=(n // tn, ne // eb),
        in_specs=[
            pl.BlockSpec((tn, d), lambda i, j: (i, 0)),
            pl.BlockSpec((d, tn), lambda i, j: (0, i)),
            row, tok, row, tok,
            pl.BlockSpec((eb, d), lambda i, j: (j, 0)),
            pl.BlockSpec((d, eb), lambda i, j: (0, j)),
            pl.BlockSpec((1, d), lambda i, j: (0, 0)),
        ],
        out_specs=pl.BlockSpec((tn, d), lambda i, j: (i, 0)),
        out_shape=jax.ShapeDtypeStruct((n, d), F32),
        scratch_shapes=[pltpu.VMEM((d, tn), F32)]
        + [pltpu.VMEM((eb // PEER_PRE_PIECES, tn), F32)] * PEER_PRE_PIECES
        + [pltpu.VMEM((eb // PEER_ACT_CHUNKS, tn), BF16)] * PEER_ACT_CHUNKS,
        compiler_params=_cparams(("arbitrary", "arbitrary")),
        name="peer_ffn",
    )(x1, h2t, lim, rank2, e1, e2, u, vt, final_norm_w.reshape(1, d))


def kernel(x, norm1_w, w_in, ret_gn_w, ret_gn_b, rwkv_mix, rwkv_w0, rwkv_w_up, rwkv_a0, rwkv_a_up, rwkv_g_up, rwkv_k_k, rwkv_k_a, rwkv_r_k, rwkv_ln_w, rwkv_ln_b, w_ret_branch, w_rwkv_branch, w_out, norm2_w, peer_w_q, peer_keys_1, peer_keys_2, peer_u, peer_v, final_norm_w):
    b, t, d = x.shape
    n = b * t
    assert w_in.shape[0] == 1, "single-layer block: the final norm is fused into the PEER kernel"
    l = 0
    xf = x.reshape(n, d)
    zp = in_proj(xf, norm1_w[l], w_in[l].astype(BF16))
    o_ret = retention(zp, ret_gn_w[l], ret_gn_b[l], b, t)
    pre = rwkv_pre(zp, rwkv_mix[l], rwkv_w0[l], rwkv_w_up[l], rwkv_a0[l], rwkv_a_up[l], rwkv_g_up[l],
                   rwkv_k_k[l], rwkv_k_a[l], rwkv_r_k[l], b, t)
    o_rwkv = rwkv_scan(*pre, rwkv_ln_w[l], rwkv_ln_b[l], b, t)
    x1, s_t, h2t = merge_query(xf, zp, o_ret, o_rwkv, w_ret_branch[l], w_rwkv_branch[l], w_out[l],
                               norm2_w[l], peer_w_q[l], peer_keys_1[l], peer_keys_2[l])
    lim, rank2, e1, e2 = peer_topk(s_t)
    out = peer_ffn(x1, h2t, lim, rank2, e1, e2, peer_u[l], peer_v[l], final_norm_w)
    return out.reshape(b, t, d)
```

```python
import functools

import numpy as np
import jax
import jax.numpy as jnp
from jax import lax
from jax.experimental import pallas as pl
from jax.experimental.pallas import tpu as pltpu

F32 = jnp.float32
BF16 = jnp.bfloat16

D_MODEL = 2048
CHUNK = 64
NORM_EPS = 1e-6
RET_HEADS = 4
RET_DK = 256
RET_DV = 256
RET_QK = RET_HEADS * RET_DK
RET_WIDTH = RET_HEADS * RET_DV
ROPE_BASE = 10000.0
RWKV_HEAD = 64
RWKV_WIDTH = D_MODEL // 2
RWKV_HEADS = RWKV_WIDTH // RWKV_HEAD
DECAY_LORA = 64
AAA_LORA = 64
GATE_LORA = 128
LORA_COLS = DECAY_LORA + AAA_LORA + GATE_LORA
RWKV_COLS = 3 * RWKV_WIDTH + LORA_COLS
RWKV_GN_EPS = 64e-5
IN_COLS = 2 * RET_QK + 2 * RET_WIDTH + RWKV_COLS + 2 * D_MODEL

ZP_RWKV = 2 * RET_QK + 2 * RET_WIDTH
ZP_LORA = ZP_RWKV + 3 * RWKV_WIDTH
ZP_GATE_RET = ZP_RWKV + RWKV_COLS
ZP_GATE_RWKV = ZP_GATE_RET + D_MODEL
assert ZP_LORA % LORA_COLS == 0

VMEM_LIMIT = 56 * 1024 * 1024


def _cparams(sem):
    return pltpu.CompilerParams(dimension_semantics=sem, vmem_limit_bytes=VMEM_LIMIT)


def _in_proj_kernel(x_ref, nw_ref, w_ref, o_ref, h_ref):
    @pl.when(pl.program_id(1) == 0)
    def _():
        x = x_ref[...]
        ms = jnp.mean(x * x, axis=-1, keepdims=True)
        h_ref[...] = (x * lax.rsqrt(ms + NORM_EPS) * nw_ref[...]).astype(BF16)

    o_ref[...] = jnp.dot(h_ref[...], w_ref[...], preferred_element_type=F32)


def in_proj(x, norm_w, w_bf16, tm=1024, tn=1280):
    n, d = x.shape
    cols = w_bf16.shape[1]
    return pl.pallas_call(
        _in_proj_kernel,
        grid=(n // tm, cols // tn),
        in_specs=[
            pl.BlockSpec((tm, d), lambda i, j: (i, 0)),
            pl.BlockSpec((1, d), lambda i, j: (0, 0)),
            pl.BlockSpec((d, tn), lambda i, j: (0, j)),
        ],
        out_specs=pl.BlockSpec((tm, tn), lambda i, j: (i, j)),
        out_shape=jax.ShapeDtypeStruct((n, cols), F32),
        scratch_shapes=[pltpu.VMEM((tm, d), BF16)],
        compiler_params=_cparams(("arbitrary", "arbitrary")),
        name="in_proj",
    )(x, norm_w.reshape(1, d), w_bf16)


def _retention_kernel(q_ref, k_ref, v_ref, g_ref, cos_ref, sin_ref, dintra_ref, qdec_ref, kdec_ref,
                      cdec_ref, gnw_ref, gnb_ref, o_ref, s_ref, qr_ref, kr_ref, acc_ref):
    tb = q_ref.shape[0]
    half = RET_DK // 2

    @pl.when(pl.program_id(1) == 0)
    def _():
        s_ref[...] = jnp.zeros_like(s_ref)

    cos = cos_ref[...]
    sin = sin_ref[...]

    def rot(z_ref, hd):
        z1 = z_ref[:, hd * RET_DK:hd * RET_DK + half]
        z2 = z_ref[:, hd * RET_DK + half:(hd + 1) * RET_DK]
        return jnp.concatenate([z1 * cos - z2 * sin, z2 * cos + z1 * sin], axis=-1)

    for hd in range(RET_HEADS):
        cols = pl.ds(hd * RET_DK, RET_DK)
        qr_ref[:, cols] = rot(q_ref, hd)
        kr_ref[:, cols] = rot(k_ref, hd) * (RET_DK ** -0.5)
    for c in range(tb // CHUNK):
        sl = pl.ds(c * CHUNK, CHUNK)
        for hd in range(RET_HEADS):
            cols = pl.ds(hd * RET_DK, RET_DK)
            qc = qr_ref[sl, cols]
            kc = kr_ref[sl, cols]
            vc = v_ref[sl, cols].astype(BF16)
            scores = lax.dot_general(qc.astype(BF16), kc.astype(BF16), (((1,), (1,)), ((), ())),
                                     preferred_element_type=F32) * dintra_ref[hd]
            s_prev = s_ref[hd]
            o = jnp.dot(scores.astype(BF16), vc, preferred_element_type=F32)
            o = o + jnp.dot((qc * qdec_ref[hd]).astype(BF16), s_prev.astype(BF16), preferred_element_type=F32)
            kd_t = (kc * kdec_ref[hd]).T.astype(BF16)
            s_ref[hd] = s_prev * cdec_ref[hd] + jnp.dot(kd_t, vc, preferred_element_type=F32)
            acc_ref[sl, cols] = o
    for hd in range(RET_HEADS):
        cols = pl.ds(hd * RET_DK, RET_DK)
        o = acc_ref[:, cols]
        mu = jnp.mean(o, axis=-1, keepdims=True)
        oc = o - mu
        var = jnp.mean(oc * oc, axis=-1, keepdims=True)
        y = oc * lax.rsqrt(var + NORM_EPS) * gnw_ref[:, cols] + gnb_ref[:, cols]
        g = g_ref[:, cols]
        o_ref[:, cols] = y * (g * jax.nn.sigmoid(g))


def _retention_tables(t):
    h = RET_HEADS
    log_g = jnp.log(1.0 - jnp.exp2(-5.0 - jnp.arange(h, dtype=F32)))
    n = jnp.arange(CHUNK, dtype=F32)
    d_intra = jnp.exp(log_g[:, None, None] * jnp.abs(n[:, None] - n[None, :]))
    q_decay = jnp.exp(log_g[:, None] * (n[None, :] + 1.0))
    k_decay = jnp.exp(log_g[:, None] * (CHUNK - 1.0 - n[None, :]))
    chunk_decay = jnp.exp(log_g * CHUNK)
    q_decay = jnp.broadcast_to(q_decay[:, :, None], (h, CHUNK, RET_DK))
    k_decay = jnp.broadcast_to(k_decay[:, :, None], (h, CHUNK, RET_DK))
    chunk_decay = jnp.broadcast_to(chunk_decay[:, None, None], (h, 1, RET_DV))
    half = RET_DK // 2
    inv_freq = ROPE_BASE ** (-jnp.arange(half, dtype=F32) * 2.0 / RET_DK)
    ang = jnp.arange(t, dtype=jnp.int32).astype(F32)[:, None] * inv_freq[None, :]
    return d_intra, q_decay, k_decay, chunk_decay, jnp.cos(ang), jnp.sin(ang)


def retention(zp, gn_w, gn_b, batch, t, tb=512):
    n = zp.shape[0]
    h = RET_HEADS
    nt = t // tb
    d_intra, q_decay, k_decay, chunk_decay, cos, sin = _retention_tables(t)

    def zspec(col_block):
        return pl.BlockSpec((tb, RET_WIDTH), lambda b, i: (b * nt + i, col_block))

    tab = lambda shape: pl.BlockSpec((h,) + shape, lambda b, i: (0, 0, 0))
    return pl.pallas_call(
        _retention_kernel,
        grid=(batch, nt),
        in_specs=[
            zspec(0), zspec(1), zspec(2), zspec(3),
            pl.BlockSpec((tb, RET_DK // 2), lambda b, i: (i, 0)),
            pl.BlockSpec((tb, RET_DK // 2), lambda b, i: (i, 0)),
            tab((CHUNK, CHUNK)), tab((CHUNK, RET_DK)), tab((CHUNK, RET_DK)), tab((1, RET_DV)),
            pl.BlockSpec((1, RET_WIDTH), lambda b, i: (0, 0)),
            pl.BlockSpec((1, RET_WIDTH), lambda b, i: (0, 0)),
        ],
        out_specs=pl.BlockSpec((tb, RET_WIDTH), lambda b, i: (b * nt + i, 0)),
        out_shape=jax.ShapeDtypeStruct((n, RET_WIDTH), F32),
        scratch_shapes=[
            pltpu.VMEM((h, RET_DK, RET_DV), F32),
            pltpu.VMEM((tb, RET_QK), F32),
            pltpu.VMEM((tb, RET_QK), F32),
            pltpu.VMEM((tb, RET_WIDTH), F32),
        ],
        compiler_params=_cparams(("arbitrary", "arbitrary")),
        name="retention",
    )(zp, zp, zp, zp, cos, sin, d_intra, q_decay, k_decay, chunk_decay,
      gn_w.reshape(1, RET_WIDTH), gn_b.reshape(1, RET_WIDTH))


def _split2(x):
    hi = x.astype(BF16)
    lo = (x - hi.astype(F32)).astype(BF16)
    return hi, lo


def _split3(x):
    hi = x.astype(BF16)
    r = x - hi.astype(F32)
    mid = r.astype(BF16)
    lo = (r - mid.astype(F32)).astype(BF16)
    return hi, mid, lo


def _group_sum(x, gmat):
    hi, lo = _split2(x)
    lanes = gmat.shape[0]
    outs = []
    for b in range(x.shape[1] // lanes):
        sl = slice(b * lanes, (b + 1) * lanes)
        outs.append(jnp.dot(hi[:, sl], gmat, preferred_element_type=F32)
                    + jnp.dot(lo[:, sl], gmat, preferred_element_type=F32))
    return jnp.concatenate(outs, axis=-1)


def _bmm(a, b):
    return lax.dot_general(a.astype(BF16), b.astype(BF16), (((2,), (1,)), ((0,), (0,))),
                           preferred_element_type=F32)


def _bmm_nt(a, b):
    return lax.dot_general(a.astype(BF16), b.astype(BF16), (((2,), (2,)), ((0,), (0,))),
                           preferred_element_type=F32)


def _bmm_tn(a, b):
    return lax.dot_general(a.astype(BF16), b.astype(BF16), (((1,), (1,)), ((0,), (0,))),
                           preferred_element_type=F32)


RWKV_HEAD_GROUP = 16
GROUP_LANES = 128


def _rwkv_pre_kernel(r_ref, k_ref, v_ref, l_ref, pr_ref, pk_ref, pv_ref, plr_ref,
                     mixr_ref, mixk_ref, mixv_ref, mixl_ref, w0_ref, a0_ref, kk_ref, ka_ref, rk_ref,
                     wup_ref, aup_ref, gup_ref, gmat_ref, ltri_ref,
                     m_out, n_out, q_out, y_out, g_out, bv_out,
                     at_s, rt_s, bh_s, kh_s, bt_s, kt_s, v_s, gc_s, *, tiles_per_batch):
    tb = r_ref.shape[0]
    nchunk = tb // CHUNK
    first = (pl.program_id(0) % tiles_per_batch) == 0
    rowid = lax.broadcasted_iota(jnp.int32, (tb, 1), 0)

    def shift_lerp(cur_ref, prev_ref, mix_ref):
        cur = cur_ref[...]
        prev_row = jnp.where(first, 0.0, prev_ref[7:8, :])
        prev = jnp.where(rowid == 0, prev_row, pltpu.roll(cur, 1, axis=0))
        return cur + (prev - cur) * mix_ref[...]

    r = shift_lerp(r_ref, pr_ref, mixr_ref)
    k = shift_lerp(k_ref, pk_ref, mixk_ref)
    v = shift_lerp(v_ref, pv_ref, mixv_ref)
    lo = shift_lerp(l_ref, plr_ref, mixl_ref)
    xw = lo[:, :DECAY_LORA]
    xa = lo[:, DECAY_LORA:DECAY_LORA + AAA_LORA]
    xg = lo[:, DECAY_LORA + AAA_LORA:]
    wl = jnp.dot(jnp.tanh(xw).astype(BF16), wup_ref[...], preferred_element_type=F32)
    u = -(w0_ref[...] + wl)
    softplus = jnp.maximum(u, 0.0) + jnp.log(1.0 + jnp.exp(-jnp.abs(u)))
    logw = -jnp.exp(-softplus - 0.5)
    a = jax.nn.sigmoid(a0_ref[...] + jnp.dot(xa.astype(BF16), aup_ref[...], preferred_element_type=F32))
    g_out[...] = jnp.dot(jax.nn.sigmoid(xg).astype(BF16), gup_ref[...], preferred_element_type=F32)
    gmat = gmat_ref[...]
    kk = k * kk_ref[...]
    kk = kk / jnp.maximum(jnp.sqrt(_group_sum(kk * kk, gmat)), 1e-12)
    k2 = k * (1.0 + (a - 1.0) * ka_ref[...])
    bv_out[...] = _group_sum(r * k2 * rk_ref[...], gmat) * v
    beta = kk * a
    ltri = ltri_ref[...]

    def to_heads(dst, c, val):
        for h in range(RWKV_HEADS):
            dst[h, pl.ds(c * CHUNK, CHUNK), :] = val[:, h * RWKV_HEAD:(h + 1) * RWKV_HEAD]

    for c in range(nchunk):
        sl = slice(c * CHUNK, (c + 1) * CHUNK)
        lw = logw[sl]
        h1, h2, h3 = _split3(lw)
        cs = (jnp.dot(ltri, h1, preferred_element_type=F32) + jnp.dot(ltri, h2, preferred_element_type=F32)
              + jnp.dot(ltri, h3, preferred_element_type=F32))
        cs_last = cs[CHUNK - 1:CHUNK, :]
        g_t = jnp.exp(cs)
        g_tm1 = jnp.exp(cs - lw)
        inv_g = jnp.exp(-cs)
        g_end = jnp.exp(cs_last - cs)
        to_heads(at_s, c, -kk[sl] * g_tm1)
        to_heads(rt_s, c, r[sl] * g_t)
        to_heads(bh_s, c, beta[sl] * inv_g)
        to_heads(kh_s, c, k2[sl] * inv_g)
        to_heads(bt_s, c, beta[sl] * g_end)
        to_heads(kt_s, c, k2[sl] * g_end)
        to_heads(v_s, c, v[sl])
        gc = jnp.exp(cs_last)
        for h in range(RWKV_HEADS):
            gc_s[h, pl.ds(c * 8, 8), :] = jnp.broadcast_to(gc[:, h * RWKV_HEAD:(h + 1) * RWKV_HEAD], (8, RWKV_HEAD))

    G = RWKV_HEAD_GROUP
    ti = lax.broadcasted_iota(jnp.int32, (CHUNK, CHUNK), 0)
    si = lax.broadcasted_iota(jnp.int32, (CHUNK, CHUNK), 1)
    strict = (si < ti)[None]
    incl = (si <= ti)[None]
    eye = (si == ti).astype(F32)[None]

    def body(idx, carry):
        c = idx // (RWKV_HEADS // G)
        hg = idx % (RWKV_HEADS // G)
        hs = pl.ds(pl.multiple_of(hg * G, G), G)
        ts = pl.ds(pl.multiple_of(c * CHUNK, CHUNK), CHUNK)
        at = at_s[hs, ts, :]
        rt = rt_s[hs, ts, :]
        bh = bh_s[hs, ts, :]
        kh = kh_s[hs, ts, :]
        bt = bt_s[hs, ts, :]
        kt = kt_s[hs, ts, :]
        vv = v_s[hs, ts, :]
        gc = gc_s[hs, pl.ds(pl.multiple_of(c * 8, 8), 1), :]
        aab = jnp.where(strict, _bmm_nt(at, bh), 0.0)
        aak = jnp.where(strict, _bmm_nt(at, kh), 0.0)
        aqb = jnp.where(incl, _bmm_nt(rt, bh), 0.0)
        aqk = jnp.where(incl, _bmm_nt(rt, kh), 0.0)
        p = aab
        tm = eye + aab
        for _ in range(5):
            p = _bmm(p, p)
            tm = tm + _bmm(p, tm)
        akv = _bmm(aak, vv)
        wt = _bmm(tm, at)
        u0 = _bmm(tm, akv)
        q_out[hs, ts, :] = (rt + _bmm(aqb, wt)).astype(BF16)
        y_out[hs, ts, :] = _bmm(aqb, u0) + _bmm(aqk, vv)
        m_out[hs, ts, :] = (_bmm_tn(bt, wt) + eye * gc).astype(BF16)
        n_out[hs, ts, :] = _bmm_tn(bt, u0) + _bmm_tn(kt, vv)
        return carry

    lax.fori_loop(0, nchunk * (RWKV_HEADS // G), body, 0)


def _rwkv_consts():
    lane = np.arange(GROUP_LANES) // RWKV_HEAD
    gmat = (lane[:, None] == lane[None, :]).astype(np.float32)
    ltri = np.tril(np.ones((CHUNK, CHUNK), np.float32))
    return jnp.asarray(gmat, BF16), jnp.asarray(ltri, BF16)


def rwkv_pre(zp, mix, w0, w_up, a0, a_up, g_up, k_k, k_a, r_k, batch, t, tb=128):
    n = zp.shape[0]
    W = RWKV_WIDTH
    nt = t // tb
    gmat, ltri = _rwkv_consts()
    row = lambda p: p.reshape(1, -1)
    mix_r, mix_k, mix_v, mix_l = mix[:W], mix[W:2 * W], mix[2 * W:3 * W], mix[3 * W:]
    cb = ZP_RWKV // W
    cur = lambda j: pl.BlockSpec((tb, W), lambda i: (i, cb + j))
    prev = lambda j: pl.BlockSpec((8, W), lambda i: (jnp.maximum(i * (tb // 8) - 1, 0), cb + j))
    full = lambda shape: pl.BlockSpec(shape, lambda i: (0,) * len(shape))
    hm = pl.BlockSpec((RWKV_HEADS, tb, RWKV_HEAD), lambda i: (0, i, 0))
    nat = pl.BlockSpec((tb, W), lambda i: (i, 0))
    hm_shape = jax.ShapeDtypeStruct((RWKV_HEADS, n, RWKV_HEAD), F32)
    hm_bf16 = jax.ShapeDtypeStruct((RWKV_HEADS, n, RWKV_HEAD), BF16)
    nat_shape = jax.ShapeDtypeStruct((n, W), F32)
    hscr = pltpu.VMEM((RWKV_HEADS, tb, RWKV_HEAD), F32)
    return pl.pallas_call(
        functools.partial(_rwkv_pre_kernel, tiles_per_batch=nt),
        grid=(n // tb,),
        in_specs=[
            cur(0), cur(1), cur(2),
            pl.BlockSpec((tb, LORA_COLS), lambda i: (i, ZP_LORA // LORA_COLS)),
            prev(0), prev(1), prev(2),
            pl.BlockSpec((8, LORA_COLS), lambda i: (jnp.maximum(i * (tb // 8) - 1, 0), ZP_LORA // LORA_COLS)),
            full((1, W)), full((1, W)), full((1, W)), full((1, LORA_COLS)),
            full((1, W)), full((1, W)), full((1, W)), full((1, W)), full((1, W)),
            full((DECAY_LORA, W)), full((AAA_LORA, W)), full((GATE_LORA, W)),
            full((GROUP_LANES, GROUP_LANES)), full((CHUNK, CHUNK)),
        ],
        out_specs=[hm, hm, hm, hm, nat, nat],
        out_shape=[hm_bf16, hm_shape, hm_bf16, hm_shape, nat_shape, nat_shape],
        scratch_shapes=[hscr] * 7 + [pltpu.VMEM((RWKV_HEADS, 8 * (tb // CHUNK), RWKV_HEAD), F32)],
        compiler_params=_cparams(("arbitrary",)),
        name="rwkv_pre",
    )(zp, zp, zp, zp, zp, zp, zp, zp,
      row(mix_r), row(mix_k), row(mix_v), row(mix_l), row(w0), row(a0), row(k_k), row(k_a), row(r_k),
      w_up.astype(BF16), a_up.astype(BF16), g_up.astype(BF16), gmat, ltri)


def _rwkv_scan_kernel(m_ref, n_ref, q_ref, y_ref, g_ref, bv_ref, lnw_ref, lnb_ref, gmat_ref, o_ref, s_ref, ys_ref):
    tb = g_ref.shape[0]

    @pl.when(pl.program_id(1) == 0)
    def _():
        s_ref[...] = jnp.zeros_like(s_ref)

    for c in range(tb // CHUNK):
        ts = pl.ds(c * CHUNK, CHUNK)
        s = s_ref[...]
        ys_ref[:, ts, :] = _bmm(q_ref[:, ts, :], s) + y_ref[:, ts, :]
        s_ref[...] = _bmm(m_ref[:, ts, :], s) + n_ref[:, ts, :]
    y = jnp.concatenate([ys_ref[h] for h in range(RWKV_HEADS)], axis=-1)
    gmat = gmat_ref[...]
    mu = _group_sum(y, gmat) * (1.0 / RWKV_HEAD)
    yc = y - mu
    var = _group_sum(yc * yc, gmat) * (1.0 / RWKV_HEAD)
    yn = yc * lax.rsqrt(var + RWKV_GN_EPS) * lnw_ref[...] + lnb_ref[...]
    o_ref[...] = (yn + bv_ref[...]) * g_ref[...]


def rwkv_scan(m, nn, qt, y1, g, bv, ln_w, ln_b, batch, t, tb=256):
    n = g.shape[0]
    W = RWKV_WIDTH
    nt = t // tb
    gmat, _ = _rwkv_consts()
    hm = pl.BlockSpec((RWKV_HEADS, tb, RWKV_HEAD), lambda b, i: (0, b * nt + i, 0))
    nat = pl.BlockSpec((tb, W), lambda b, i: (b * nt + i, 0))
    full = lambda shape: pl.BlockSpec(shape, lambda b, i: (0,) * len(shape))
    return pl.pallas_call(
        _rwkv_scan_kernel,
        grid=(batch, nt),
        in_specs=[hm, hm, hm, hm, nat, nat, full((1, W)), full((1, W)), full((GROUP_LANES, GROUP_LANES))],
        out_specs=nat,
        out_shape=jax.ShapeDtypeStruct((n, W), F32),
        scratch_shapes=[pltpu.VMEM((RWKV_HEADS, RWKV_HEAD, RWKV_HEAD), F32),
                        pltpu.VMEM((RWKV_HEADS, tb, RWKV_HEAD), F32)],
        compiler_params=_cparams(("arbitrary", "arbitrary")),
        name="rwkv_scan",
    )(m, nn, qt, y1, g, bv, ln_w.reshape(1, W), ln_b.reshape(1, W), gmat)


PEER_HEADS = 8
PEER_NKEYS = 128
PEER_NEXPERTS = PEER_NKEYS * PEER_NKEYS
PEER_DQ = 256
PEER_TOPK = 16
PEER_ACT_CHUNKS = 4
PEER_PRE_PIECES = 2


def _merge_query_kernel(x_ref, gr_ref, gw_ref, oret_ref, orwkv_ref, wr_ref, ww_ref, wo_ref, nw_ref, wq_ref,
                        k1_ref, k2_ref, x1_ref, s_ref, ht_ref):
    pr = jnp.dot(oret_ref[...].astype(BF16), wr_ref[...], preferred_element_type=F32)
    pw = jnp.dot(orwkv_ref[...].astype(BF16), ww_ref[...], preferred_element_type=F32)
    merged = jax.nn.sigmoid(gr_ref[...]) * pr + jax.nn.sigmoid(gw_ref[...]) * pw
    x1 = x_ref[...] + jnp.dot(merged.astype(BF16), wo_ref[...], preferred_element_type=F32)
    x1_ref[...] = x1
    ms = jnp.mean(x1 * x1, axis=-1, keepdims=True)
    h2 = x1 * lax.rsqrt(ms + NORM_EPS) * nw_ref[...]
    ht_ref[...] = h2.T.astype(BF16)
    q = jnp.dot(h2.astype(BF16), wq_ref[...], preferred_element_type=F32).astype(BF16)
    half = PEER_DQ // 2
    for h in range(PEER_HEADS):
        for p, kref in enumerate((k1_ref, k2_ref)):
            qh = q[:, h * PEER_DQ + p * half: h * PEER_DQ + (p + 1) * half]
            s_ref[p, h] = lax.dot_general(kref[...], qh, (((1,), (1,)), ((), ())), preferred_element_type=F32)


def merge_query(x, zp, o_ret, o_rwkv, w_ret, w_rwkv, w_out, norm_w, w_q, keys_1, keys_2, tm=256):
    n, d = x.shape
    const = lambda shape: pl.BlockSpec(shape, lambda i: (0, 0), pipeline_mode=pl.Buffered(1))

    def gate(col0):
        return pl.BlockSpec((pl.Element(tm), pl.Element(d)), lambda i: (i * tm, col0))

    return pl.pallas_call(
        _merge_query_kernel,
        grid=(n // tm,),
        in_specs=[pl.BlockSpec((tm, d), lambda i: (i, 0)), gate(ZP_GATE_RET), gate(ZP_GATE_RWKV),
                  pl.BlockSpec((tm, RET_WIDTH), lambda i: (i, 0)),
                  pl.BlockSpec((tm, RWKV_WIDTH), lambda i: (i, 0)),
                  const((RET_WIDTH, d)), const((RWKV_WIDTH, d)), const((d, d)),
                  const((1, d)), const((d, PEER_HEADS * PEER_DQ)),
                  const((PEER_NKEYS, PEER_DQ // 2)), const((PEER_NKEYS, PEER_DQ // 2))],
        out_specs=[
            pl.BlockSpec((tm, d), lambda i: (i, 0)),
            pl.BlockSpec((2, PEER_HEADS, PEER_NKEYS, tm), lambda i: (0, 0, 0, i)),
            pl.BlockSpec((d, tm), lambda i: (0, i)),
        ],
        out_shape=[
            jax.ShapeDtypeStruct((n, d), F32),
            jax.ShapeDtypeStruct((2, PEER_HEADS, PEER_NKEYS, n), F32),
            jax.ShapeDtypeStruct((d, n), BF16),
        ],
        compiler_params=_cparams(("arbitrary",)),
        name="merge_query",
    )(x, zp, zp, o_ret, o_rwkv, w_ret.astype(BF16), w_rwkv.astype(BF16), w_out.astype(BF16),
      norm_w.reshape(1, d), w_q.astype(BF16), keys_1.astype(BF16), keys_2.astype(BF16))


_CAND_GROUPS = [(0, 0), (0, 8)] + [(a, 0) for a in range(1, 8)]


def _peer_topk_kernel(s_ref, lim_ref, rank2_ref, e1_ref, e2_ref, v1_s, r1_s, v2_s, r2_s, cnt_s, z_s):
    tn = s_ref.shape[-1]
    K = PEER_TOPK
    neg = -jnp.inf
    rowid = lax.broadcasted_iota(jnp.int32, (PEER_NKEYS, tn), 0)
    nrow = 8 * (len(_CAND_GROUPS) + 1)
    r = lax.broadcasted_iota(jnp.int32, (nrow, tn), 0)
    grp, sub = r // 8, r % 8
    ca = jnp.where(grp < 2, 0, jnp.where(grp < 9, grp - 1, 8 + sub))
    cb = jnp.where(grp == 1, 8 + sub, jnp.where(grp < 9, sub, 0))
    flat = ca * K + cb
    valid = (ca + 1) * (cb + 1) <= K

    def top16(s, break_ties):
        rank = jnp.full((PEER_NKEYS, tn), K, jnp.int32)
        vals = []
        for a in range(K):
            m = jnp.max(s, axis=0, keepdims=True)
            hit = s == m
            if break_ties:
                idx = jnp.min(jnp.where(hit, rowid, PEER_NKEYS), axis=0, keepdims=True)
                hit = rowid == idx
            rank = jnp.where(hit, a, rank)
            s = jnp.where(hit, neg, s)
            vals.append(m)
        return jnp.concatenate(vals, axis=0), rank

    def head(h, carry):
        s1 = s_ref[0, h]
        s2 = s_ref[1, h]
        v1_s[...], r1_s[...] = top16(s1, False)
        v2_s[...], r2_s[...] = top16(s2, False)
        ranked = (jnp.sum((r1_s[...] < K).astype(jnp.int32), axis=0, keepdims=True)
                  + jnp.sum((r2_s[...] < K).astype(jnp.int32), axis=0, keepdims=True))
        tied = jnp.max(jnp.abs(ranked - 2 * K)) > 0

        @pl.when(tied)
        def _():
            v1_s[...], r1_s[...] = top16(s1, True)
            v2_s[...], r2_s[...] = top16(s2, True)

        v1, rank1 = v1_s[...], r1_s[...]
        v2, rank2 = v2_s[...], r2_s[...]
        pieces = [v1[a:a + 1] + v2[b0:b0 + 8] for a, b0 in _CAND_GROUPS] + [v1[8:16] + v2[0:1]]
        cand = jnp.where(valid, jnp.concatenate(pieces, axis=0), neg)
        m0 = v1[0:1] + v2[0:1]

        def best16(cand, break_ties):
            z = jnp.zeros((1, tn), F32)
            for _ in range(K):
                m = jnp.max(cand, axis=0, keepdims=True)
                hit = cand == m
                if break_ties:
                    f = jnp.min(jnp.where(hit, flat, K * K), axis=0, keepdims=True)
                    hit = flat == f
                cand = jnp.where(hit, neg, cand)
                z = z + jnp.exp(m - m0)
            chosen = jnp.where(jnp.logical_and(valid, cand == neg), 1.0, 0.0)
            per_a = [jnp.sum(chosen[0:16], axis=0, keepdims=True)]
            per_a += [jnp.sum(chosen[8 * (a + 1):8 * (a + 2)], axis=0, keepdims=True) for a in range(1, 8)]
            return jnp.concatenate(per_a + [chosen[8 * 9:8 * 10]], axis=0), z

        cnt_s[...], z_s[...] = best16(cand, False)
        tied2 = jnp.max(jnp.abs(jnp.sum(cnt_s[...], axis=0, keepdims=True) - K)) > 0

        @pl.when(tied2)
        def _():
            cnt_s[...], z_s[...] = best16(cand, True)

        count = cnt_s[...]
        inv_z = 1.0 / z_s[...]
        e1_ref[h] = jnp.where(rank1 < K, jnp.exp(s1 - v1[0:1]) * inv_z, 0.0)
        e2_ref[h] = jnp.where(rank2 < K, jnp.exp(s2 - v2[0:1]), 0.0).astype(BF16)
        rank2_ref[h] = rank2.astype(F32).astype(BF16)
        lim = jnp.zeros((PEER_NKEYS, tn), F32)
        for a in range(K):
            lim = jnp.where(rank1 == a, count[a:a + 1], lim)
        lim_ref[h] = lim
        return carry

    lax.fori_loop(0, PEER_HEADS, head, 0)


def peer_topk(s_t, tn=256):
    n = s_t.shape[-1]
    spec = pl.BlockSpec((PEER_HEADS, PEER_NKEYS, tn), lambda i: (0, 0, i))
    shp = lambda dt: jax.ShapeDtypeStruct((PEER_HEADS, PEER_NKEYS, n), dt)
    return pl.pallas_call(
        _peer_topk_kernel,
        grid=(n // tn,),
        in_specs=[pl.BlockSpec((2, PEER_HEADS, PEER_NKEYS, tn), lambda i: (0, 0, 0, i))],
        out_specs=[spec, spec, spec, spec],
        out_shape=[shp(F32), shp(BF16), shp(F32), shp(BF16)],
        scratch_shapes=[pltpu.VMEM((PEER_TOPK, tn), F32), pltpu.VMEM((PEER_NKEYS, tn), jnp.int32),
                        pltpu.VMEM((PEER_TOPK, tn), F32), pltpu.VMEM((PEER_NKEYS, tn), jnp.int32),
                        pltpu.VMEM((PEER_TOPK, tn), F32), pltpu.VMEM((1, tn), F32)],
        compiler_params=_cparams(("arbitrary",)),
        name="peer_topk",
    )(s_t)


def _peer_ffn_kernel(x_ref, ht_ref, lim_ref, rank2_ref, e1_ref, e2_ref, u_ref, vt_ref, fnw_ref, o_ref,
                     acc_ref, *scratch):
    j = pl.program_id(1)
    eb = u_ref.shape[0]
    nsub = eb // PEER_NKEYS
    pre_refs, act_refs = scratch[:PEER_PRE_PIECES], scratch[PEER_PRE_PIECES:]
    piece = eb // len(pre_refs)
    per_chunk = nsub // len(act_refs)
    ck = per_chunk * PEER_NKEYS
    zero = jnp.zeros((), BF16)

    @pl.when(j == 0)
    def _():
        acc_ref[...] = jnp.zeros_like(acc_ref)

    for k, pre_k in enumerate(pre_refs):
        pre_k[...] = jnp.dot(u_ref[pl.ds(k * piece, piece), :], ht_ref[...], preferred_element_type=F32)
    for c, act_c in enumerate(act_refs):
        for jc in range(per_chunk):
            jj = c * per_chunk + jc
            gate = None
            for h in range(PEER_HEADS):
                lrow = lim_ref[h, jj:jj + 1, :].astype(BF16)
                erow = e1_ref[h, jj:jj + 1, :].astype(BF16)
                term = jnp.where(rank2_ref[h] < lrow, e2_ref[h], zero) * erow
                gate = term if gate is None else gate + term
            p = pre_refs[jj * PEER_NKEYS // piece][pl.ds(jj * PEER_NKEYS % piece, PEER_NKEYS), :]
            gelu = 0.5 * p * (1.0 + lax.erf(p * (2.0 ** -0.5)))
            act_c[pl.ds(jc * PEER_NKEYS, PEER_NKEYS), :] = gelu.astype(BF16) * gate
        acc_ref[...] += jnp.dot(vt_ref[:, pl.ds(c * ck, ck)], act_c[...], preferred_element_type=F32)

    @pl.when(j == pl.num_programs(1) - 1)
    def _():
        y = x_ref[...] + acc_ref[...].T
        ms = jnp.mean(y * y, axis=-1, keepdims=True)
        o_ref[...] = y * lax.rsqrt(ms + NORM_EPS) * fnw_ref[...]


def peer_ffn(x1, h2t, lim, rank2, e1, e2, expert_u, expert_v, final_norm_w, tn=512, eb=1024):
    n, d = x1.shape
    ne = expert_u.shape[0]
    u = expert_u.astype(BF16)
    vt = expert_v.T.astype(BF16)
    nsub = eb // PEER_NKEYS
    assert nsub == 8
    tok = pl.BlockSpec((PEER_HEADS, PEER_NKEYS, tn), lambda i, j: (0, 0, i))
    row = pl.BlockSpec((PEER_HEADS, nsub, tn), lambda i, j: (0, j, i))
    return pl.pallas_call(
        _peer_ffn_kernel,
        grid=(n // tn, ne // eb),
        in_specs=[
            pl.BlockSpec((tn, d), lambda i, j: (i, 0)),
            pl.BlockSpec((d, tn), lambda i, j: (0, i)),
            row, tok, row, tok,
            pl.BlockSpec((eb, d), lambda i, j: (j, 0)),
            pl.BlockSpec((d, eb), lambda i, j: (0, j)),
            pl.BlockSpec((1, d), lambda i, j: (0, 0)),
        ],
        out_specs=pl.BlockSpec((tn, d), lambda i, j: (i, 0)),
        out_shape=jax.ShapeDtypeStruct((n, d), F32),
        scratch_shapes=[pltpu.VMEM((d, tn), F32)]
        + [pltpu.VMEM((eb // PEER_PRE_PIECES, tn), F32)] * PEER_PRE_PIECES
        + [pltpu.VMEM((eb // PEER_ACT_CHUNKS, tn), BF16)] * PEER_ACT_CHUNKS,
        compiler_params=_cparams(("arbitrary", "arbitrary")),
        name="peer_ffn",
    )(x1, h2t, lim, rank2, e1, e2, u, vt, final_norm_w.reshape(1, d))


def kernel(x, norm1_w, w_in, ret_gn_w, ret_gn_b, rwkv_mix, rwkv_w0, rwkv_w_up, rwkv_a0, rwkv_a_up, rwkv_g_up, rwkv_k_k, rwkv_k_a, rwkv_r_k, rwkv_ln_w, rwkv_ln_b, w_ret_branch, w_rwkv_branch, w_out, norm2_w, peer_w_q, peer_keys_1, peer_keys_2, peer_u, peer_v, final_norm_w):
    b, t, d = x.shape
    n = b * t
    assert w_in.shape[0] == 1, "single-layer block: the final norm is fused into the PEER kernel"
    l = 0
    xf = x.reshape(n, d)
    zp = in_proj(xf, norm1_w[l], w_in[l].astype(BF16))
    o_ret = retention(zp, ret_gn_w[l], ret_gn_b[l], b, t)
    pre = rwkv_pre(zp, rwkv_mix[l], rwkv_w0[l], rwkv_w_up[l], rwkv_a0[l], rwkv_a_up[l], rwkv_g_up[l],
                   rwkv_k_k[l], rwkv_k_a[l], rwkv_r_k[l], b, t)
    o_rwkv = rwkv_scan(*pre, rwkv_ln_w[l], rwkv_ln_b[l], b, t)
    x1, s_t, h2t = merge_query(xf, zp, o_ret, o_rwkv, w_ret_branch[l], w_rwkv_branch[l], w_out[l],
                               norm2_w[l], peer_w_q[l], peer_keys_1[l], peer_keys_2[l])
    lim, rank2, e1, e2 = peer_topk(s_t)
    out = peer_ffn(x1, h2t, lim, rank2, e1, e2, peer_u[l], peer_v[l], final_norm_w)
    return out.reshape(b, t, d)
```

```python
import functools

import numpy as np
import jax
import jax.numpy as jnp
from jax import lax
from jax.experimental import pallas as pl
from jax.experimental.pallas import tpu as pltpu

F32 = jnp.float32
BF16 = jnp.bfloat16

D_MODEL = 2048
CHUNK = 64
NORM_EPS = 1e-6
RET_HEADS = 4
RET_DK = 256
RET_DV = 256
RET_QK = RET_HEADS * RET_DK
RET_WIDTH = RET_HEADS * RET_DV
ROPE_BASE = 10000.0
RWKV_HEAD = 64
RWKV_WIDTH = D_MODEL // 2
RWKV_HEADS = RWKV_WIDTH // RWKV_HEAD
DECAY_LORA = 64
AAA_LORA = 64
GATE_LORA = 128
LORA_COLS = DECAY_LORA + AAA_LORA + GATE_LORA
RWKV_COLS = 3 * RWKV_WIDTH + LORA_COLS
RWKV_GN_EPS = 64e-5
IN_COLS = 2 * RET_QK + 2 * RET_WIDTH + RWKV_COLS + 2 * D_MODEL

ZP_RWKV = 2 * RET_QK + 2 * RET_WIDTH
ZP_LORA = ZP_RWKV + 3 * RWKV_WIDTH
ZP_GATE_RET = ZP_RWKV + RWKV_COLS
ZP_GATE_RWKV = ZP_GATE_RET + D_MODEL
assert ZP_LORA % LORA_COLS == 0

VMEM_LIMIT = 56 * 1024 * 1024


def _cparams(sem):
    return pltpu.CompilerParams(dimension_semantics=sem, vmem_limit_bytes=VMEM_LIMIT)


def _in_proj_kernel(x_ref, nw_ref, w_ref, o_ref, h_ref):
    @pl.when(pl.program_id(1) == 0)
    def _():
        x = x_ref[...]
        ms = jnp.mean(x * x, axis=-1, keepdims=True)
        h_ref[...] = (x * lax.rsqrt(ms + NORM_EPS) * nw_ref[...]).astype(BF16)

    o_ref[...] = jnp.dot(h_ref[...], w_ref[...], preferred_element_type=F32)


def in_proj(x, norm_w, w_bf16, tm=1024, tn=1280):
    n, d = x.shape
    cols = w_bf16.shape[1]
    return pl.pallas_call(
        _in_proj_kernel,
        grid=(n // tm, cols // tn),
        in_specs=[
            pl.BlockSpec((tm, d), lambda i, j: (i, 0)),
            pl.BlockSpec((1, d), lambda i, j: (0, 0)),
            pl.BlockSpec((d, tn), lambda i, j: (0, j)),
        ],
        out_specs=pl.BlockSpec((tm, tn), lambda i, j: (i, j)),
        out_shape=jax.ShapeDtypeStruct((n, cols), F32),
        scratch_shapes=[pltpu.VMEM((tm, d), BF16)],
        compiler_params=_cparams(("arbitrary", "arbitrary")),
        name="in_proj",
    )(x, norm_w.reshape(1, d), w_bf16)


def _retention_kernel(q_ref, k_ref, v_ref, g_ref, cos_ref, sin_ref, dintra_ref, qdec_ref, kdec_ref,
                      cdec_ref, gnw_ref, gnb_ref, o_ref, s_ref, qr_ref, kr_ref, acc_ref):
    tb = q_ref.shape[0]
    half = RET_DK // 2

    @pl.when(pl.program_id(1) == 0)
    def _():
        s_ref[...] = jnp.zeros_like(s_ref)

    cos = cos_ref[...]
    sin = sin_ref[...]

    def rot(z_ref, hd):
        z1 = z_ref[:, hd * RET_DK:hd * RET_DK + half]
        z2 = z_ref[:, hd * RET_DK + half:(hd + 1) * RET_DK]
        return jnp.concatenate([z1 * cos - z2 * sin, z2 * cos + z1 * sin], axis=-1)

    for hd in range(RET_HEADS):
        cols = pl.ds(hd * RET_DK, RET_DK)
        qr_ref[:, cols] = rot(q_ref, hd)
        kr_ref[:, cols] = rot(k_ref, hd) * (RET_DK ** -0.5)
    for c in range(tb // CHUNK):
        sl = pl.ds(c * CHUNK, CHUNK)
        for hd in range(RET_HEADS):
            cols = pl.ds(hd * RET_DK, RET_DK)
            qc = qr_ref[sl, cols]
            kc = kr_ref[sl, cols]
            vc = v_ref[sl, cols].astype(BF16)
            scores = lax.dot_general(qc.astype(BF16), kc.astype(BF16), (((1,), (1,)), ((), ())),
                                     preferred_element_type=F32) * dintra_ref[hd]
            s_prev = s_ref[hd]
            o = jnp.dot(scores.astype(BF16), vc, preferred_element_type=F32)
            o = o + jnp.dot((qc * qdec_ref[hd]).astype(BF16), s_prev.astype(BF16), preferred_element_type=F32)
            kd_t = (kc * kdec_ref[hd]).T.astype(BF16)
            s_ref[hd] = s_prev * cdec_ref[hd] + jnp.dot(kd_t, vc, preferred_element_type=F32)
            acc_ref[sl, cols] = o
    for hd in range(RET_HEADS):
        cols = pl.ds(hd * RET_DK, RET_DK)
        o = acc_ref[:, cols]
        mu = jnp.mean(o, axis=-1, keepdims=True)
        oc = o - mu
        var = jnp.mean(oc * oc, axis=-1, keepdims=True)
        y = oc * lax.rsqrt(var + NORM_EPS) * gnw_ref[:, cols] + gnb_ref[:, cols]
        g = g_ref[:, cols]
        o_ref[:, cols] = y * (g * jax.nn.sigmoid(g))


def _retention_tables(t):
    h = RET_HEADS
    log_g = jnp.log(1.0 - jnp.exp2(-5.0 - jnp.arange(h, dtype=F32)))
    n = jnp.arange(CHUNK, dtype=F32)
    d_intra = jnp.exp(log_g[:, None, None] * jnp.abs(n[:, None] - n[None, :]))
    q_decay = jnp.exp(log_g[:, None] * (n[None, :] + 1.0))
    k_decay = jnp.exp(log_g[:, None] * (CHUNK - 1.0 - n[None, :]))
    chunk_decay = jnp.exp(log_g * CHUNK)
    q_decay = jnp.broadcast_to(q_decay[:, :, None], (h, CHUNK, RET_DK))
    k_decay = jnp.broadcast_to(k_decay[:, :, None], (h, CHUNK, RET_DK))
    chunk_decay = jnp.broadcast_to(chunk_decay[:, None, None], (h, 1, RET_DV))
    half = RET_DK // 2
    inv_freq = ROPE_BASE ** (-jnp.arange(half, dtype=F32) * 2.0 / RET_DK)
    ang = jnp.arange(t, dtype=jnp.int32).astype(F32)[:, None] * inv_freq[None, :]
    return d_intra, q_decay, k_decay, chunk_decay, jnp.cos(ang), jnp.sin(ang)


def retention(zp, gn_w, gn_b, batch, t, tb=512):
    n = zp.shape[0]
    h = RET_HEADS
    nt = t // tb
    d_intra, q_decay, k_decay, chunk_decay, cos, sin = _retention_tables(t)

    def zspec(col_block):
        return pl.BlockSpec((tb, RET_WIDTH), lambda b, i: (b * nt + i, col_block))

    tab = lambda shape: pl.BlockSpec((h,) + shape, lambda b, i: (0, 0, 0))
    return pl.pallas_call(
        _retention_kernel,
        grid=(batch, nt),
        in_specs=[
            zspec(0), zspec(1), zspec(2), zspec(3),
            pl.BlockSpec((tb, RET_DK // 2), lambda b, i: (i, 0)),
            pl.BlockSpec((tb, RET_DK // 2), lambda b, i: (i, 0)),
            tab((CHUNK, CHUNK)), tab((CHUNK, RET_DK)), tab((CHUNK, RET_DK)), tab((1, RET_DV)),
            pl.BlockSpec((1, RET_WIDTH), lambda b, i: (0, 0)),
            pl.BlockSpec((1, RET_WIDTH), lambda b, i: (0, 0)),
        ],
        out_specs=pl.BlockSpec((tb, RET_WIDTH), lambda b, i: (b * nt + i, 0)),
        out_shape=jax.ShapeDtypeStruct((n, RET_WIDTH), F32),
        scratch_shapes=[
            pltpu.VMEM((h, RET_DK, RET_DV), F32),
            pltpu.VMEM((tb, RET_QK), F32),
            pltpu.VMEM((tb, RET_QK), F32),
            pltpu.VMEM((tb, RET_WIDTH), F32),
        ],
        compiler_params=_cparams(("arbitrary", "arbitrary")),
        name="retention",
    )(zp, zp, zp, zp, cos, sin, d_intra, q_decay, k_decay, chunk_decay,
      gn_w.reshape(1, RET_WIDTH), gn_b.reshape(1, RET_WIDTH))


def _split2(x):
    hi = x.astype(BF16)
    lo = (x - hi.astype(F32)).astype(BF16)
    return hi, lo


def _split3(x):
    hi = x.astype(BF16)
    r = x - hi.astype(F32)
    mid = r.astype(BF16)
    lo = (r - mid.astype(F32)).astype(BF16)
    return hi, mid, lo


def _group_sum(x, gmat):
    hi, lo = _split2(x)
    lanes = gmat.shape[0]
    outs = []
    for b in range(x.shape[1] // lanes):
        sl = slice(b * lanes, (b + 1) * lanes)
        outs.append(jnp.dot(hi[:, sl], gmat, preferred_element_type=F32)
                    + jnp.dot(lo[:, sl], gmat, preferred_element_type=F32))
    return jnp.concatenate(outs, axis=-1)


def _bmm(a, b):
    return lax.dot_general(a.astype(BF16), b.astype(BF16), (((2,), (1,)), ((0,), (0,))),
                           preferred_element_type=F32)


def _bmm_nt(a, b):
    return lax.dot_general(a.astype(BF16), b.astype(BF16), (((2,), (2,)), ((0,), (0,))),
                           preferred_element_type=F32)


def _bmm_tn(a, b):
    return lax.dot_general(a.astype(BF16), b.astype(BF16), (((1,), (1,)), ((0,), (0,))),
                           preferred_element_type=F32)


RWKV_HEAD_GROUP = 16
GROUP_LANES = 128
RWKV_TILE = 128


def _rwkv_pre_kernel(r_ref, k_ref, v_ref, l_ref, pr_ref, pk_ref, pv_ref, plr_ref,
                     mixr_ref, mixk_ref, mixv_ref, mixl_ref, w0_ref, a0_ref, kk_ref, ka_ref, rk_ref,
                     wup_ref, aup_ref, gup_ref, gmat_ref, ltri_ref,
                     m_out, n_out, q_out, y_out, g_out, bv_out,
                     at_s, rt_s, bh_s, kh_s, bt_s, kt_s, v_s, gc_s, *, tiles_per_batch):
    tb = r_ref.shape[0]
    nchunk = tb // CHUNK
    first = (pl.program_id(0) % tiles_per_batch) == 0
    rowid = lax.broadcasted_iota(jnp.int32, (tb, 1), 0)

    def shift_lerp(cur_ref, prev_ref, mix_ref):
        cur = cur_ref[...]
        prev_row = jnp.where(first, 0.0, prev_ref[7:8, :])
        prev = jnp.where(rowid == 0, prev_row, pltpu.roll(cur, 1, axis=0))
        return cur + (prev - cur) * mix_ref[...]

    r = shift_lerp(r_ref, pr_ref, mixr_ref)
    k = shift_lerp(k_ref, pk_ref, mixk_ref)
    v = shift_lerp(v_ref, pv_ref, mixv_ref)
    lo = shift_lerp(l_ref, plr_ref, mixl_ref)
    xw = lo[:, :DECAY_LORA]
    xa = lo[:, DECAY_LORA:DECAY_LORA + AAA_LORA]
    xg = lo[:, DECAY_LORA + AAA_LORA:]
    wl = jnp.dot(jnp.tanh(xw).astype(BF16), wup_ref[...], preferred_element_type=F32)
    u = -(w0_ref[...] + wl)
    softplus = jnp.maximum(u, 0.0) + jnp.log(1.0 + jnp.exp(-jnp.abs(u)))
    logw = -jnp.exp(-softplus - 0.5)
    a = jax.nn.sigmoid(a0_ref[...] + jnp.dot(xa.astype(BF16), aup_ref[...], preferred_element_type=F32))
    g_out[...] = jnp.dot(jax.nn.sigmoid(xg).astype(BF16), gup_ref[...], preferred_element_type=F32)
    gmat = gmat_ref[...]
    kk = k * kk_ref[...]
    kk = kk / jnp.maximum(jnp.sqrt(_group_sum(kk * kk, gmat)), 1e-12)
    k2 = k * (1.0 + (a - 1.0) * ka_ref[...])
    bv_out[...] = _group_sum(r * k2 * rk_ref[...], gmat) * v
    beta = kk * a
    ltri = ltri_ref[...]

    def to_heads(dst, c, val):
        for h in range(RWKV_HEADS):
            dst[h, pl.ds(c * CHUNK, CHUNK), :] = val[:, h * RWKV_HEAD:(h + 1) * RWKV_HEAD]

    for c in range(nchunk):
        sl = slice(c * CHUNK, (c + 1) * CHUNK)
        lw = logw[sl]
        h1, h2, h3 = _split3(lw)
        cs = (jnp.dot(ltri, h1, preferred_element_type=F32) + jnp.dot(ltri, h2, preferred_element_type=F32)
              + jnp.dot(ltri, h3, preferred_element_type=F32))
        cs_last = cs[CHUNK - 1:CHUNK, :]
        g_t = jnp.exp(cs)
        g_tm1 = jnp.exp(cs - lw)
        inv_g = jnp.exp(-cs)
        g_end = jnp.exp(cs_last - cs)
        to_heads(at_s, c, -kk[sl] * g_tm1)
        to_heads(rt_s, c, r[sl] * g_t)
        to_heads(bh_s, c, beta[sl] * inv_g)
        to_heads(kh_s, c, k2[sl] * inv_g)
        to_heads(bt_s, c, beta[sl] * g_end)
        to_heads(kt_s, c, k2[sl] * g_end)
        to_heads(v_s, c, v[sl])
        gc = jnp.exp(cs_last)
        for h in range(RWKV_HEADS):
            gc_s[h, pl.ds(c * 8, 8), :] = jnp.broadcast_to(gc[:, h * RWKV_HEAD:(h + 1) * RWKV_HEAD], (8, RWKV_HEAD))

    G = RWKV_HEAD_GROUP
    ti = lax.broadcasted_iota(jnp.int32, (CHUNK, CHUNK), 0)
    si = lax.broadcasted_iota(jnp.int32, (CHUNK, CHUNK), 1)
    strict = (si < ti)[None]
    incl = (si <= ti)[None]
    eye = (si == ti).astype(F32)[None]

    def body(idx, carry):
        c = idx // (RWKV_HEADS // G)
        hg = idx % (RWKV_HEADS // G)
        hs = pl.ds(pl.multiple_of(hg * G, G), G)
        ts = pl.ds(pl.multiple_of(c * CHUNK, CHUNK), CHUNK)
        at = at_s[hs, ts, :]
        rt = rt_s[hs, ts, :]
        bh = bh_s[hs, ts, :]
        kh = kh_s[hs, ts, :]
        bt = bt_s[hs, ts, :]
        kt = kt_s[hs, ts, :]
        vv = v_s[hs, ts, :]
        gc = gc_s[hs, pl.ds(pl.multiple_of(c * 8, 8), 1), :]
        aab = jnp.where(strict, _bmm_nt(at, bh), 0.0)
        aak = jnp.where(strict, _bmm_nt(at, kh), 0.0)
        aqb = jnp.where(incl, _bmm_nt(rt, bh), 0.0)
        aqk = jnp.where(incl, _bmm_nt(rt, kh), 0.0)
        p = aab
        tm = eye + aab
        for _ in range(5):
            p = _bmm(p, p)
            tm = tm + _bmm(p, tm)
        akv = _bmm(aak, vv)
        wt = _bmm(tm, at)
        u0 = _bmm(tm, akv)
        q_out[hs, ts, :] = (rt + _bmm(aqb, wt)).astype(BF16)
        y_out[hs, ts, :] = _bmm(aqb, u0) + _bmm(aqk, vv)
        m_out[hs, ts, :] = (_bmm_tn(bt, wt) + eye * gc).astype(BF16)
        n_out[hs, ts, :] = _bmm_tn(bt, u0) + _bmm_tn(kt, vv)
        return carry

    lax.fori_loop(0, nchunk * (RWKV_HEADS // G), body, 0)


def _rwkv_consts():
    lane = np.arange(GROUP_LANES) // RWKV_HEAD
    gmat = (lane[:, None] == lane[None, :]).astype(np.float32)
    ltri = np.tril(np.ones((CHUNK, CHUNK), np.float32))
    return jnp.asarray(gmat, BF16), jnp.asarray(ltri, BF16)


def rwkv_pre(zp, mix, w0, w_up, a0, a_up, g_up, k_k, k_a, r_k, batch, t, tb=RWKV_TILE):
    n = zp.shape[0]
    W = RWKV_WIDTH
    nt = t // tb
    gmat, ltri = _rwkv_consts()
    row = lambda p: p.reshape(1, -1)
    mix_r, mix_k, mix_v, mix_l = mix[:W], mix[W:2 * W], mix[2 * W:3 * W], mix[3 * W:]
    cb = ZP_RWKV // W
    cur = lambda j: pl.BlockSpec((tb, W), lambda i: (i, cb + j))
    prev = lambda j: pl.BlockSpec((8, W), lambda i: (jnp.maximum(i * (tb // 8) - 1, 0), cb + j))
    full = lambda shape: pl.BlockSpec(shape, lambda i: (0,) * len(shape))
    assert tb == RWKV_TILE
    hm = pl.BlockSpec((None, RWKV_HEADS, tb, RWKV_HEAD), lambda i: (i, 0, 0, 0))
    nat = pl.BlockSpec((tb, W), lambda i: (i, 0))
    hm_shape = jax.ShapeDtypeStruct((n // tb, RWKV_HEADS, tb, RWKV_HEAD), F32)
    hm_bf16 = jax.ShapeDtypeStruct((n // tb, RWKV_HEADS, tb, RWKV_HEAD), BF16)
    nat_shape = jax.ShapeDtypeStruct((n, W), F32)
    hscr = pltpu.VMEM((RWKV_HEADS, tb, RWKV_HEAD), F32)
    return pl.pallas_call(
        functools.partial(_rwkv_pre_kernel, tiles_per_batch=nt),
        grid=(n // tb,),
        in_specs=[
            cur(0), cur(1), cur(2),
            pl.BlockSpec((tb, LORA_COLS), lambda i: (i, ZP_LORA // LORA_COLS)),
            prev(0), prev(1), prev(2),
            pl.BlockSpec((8, LORA_COLS), lambda i: (jnp.maximum(i * (tb // 8) - 1, 0), ZP_LORA // LORA_COLS)),
            full((1, W)), full((1, W)), full((1, W)), full((1, LORA_COLS)),
            full((1, W)), full((1, W)), full((1, W)), full((1, W)), full((1, W)),
            full((DECAY_LORA, W)), full((AAA_LORA, W)), full((GATE_LORA, W)),
            full((GROUP_LANES, GROUP_LANES)), full((CHUNK, CHUNK)),
        ],
        out_specs=[hm, hm, hm, hm, nat, nat],
        out_shape=[hm_bf16, hm_shape, hm_bf16, hm_shape, nat_shape, nat_shape],
        scratch_shapes=[hscr] * 7 + [pltpu.VMEM((RWKV_HEADS, 8 * (tb // CHUNK), RWKV_HEAD), F32)],
        compiler_params=_cparams(("arbitrary",)),
        name="rwkv_pre",
    )(zp, zp, zp, zp, zp, zp, zp, zp,
      row(mix_r), row(mix_k), row(mix_v), row(mix_l), row(w0), row(a0), row(k_k), row(k_a), row(r_k),
      w_up.astype(BF16), a_up.astype(BF16), g_up.astype(BF16), gmat, ltri)


def _rwkv_scan_kernel(m_ref, n_ref, q_ref, y_ref, g_ref, bv_ref, lnw_ref, lnb_ref, gmat_ref, o_ref, s_ref, ys_ref):
    tb = g_ref.shape[0]

    @pl.when(pl.program_id(1) == 0)
    def _():
        s_ref[...] = jnp.zeros_like(s_ref)

    per_tile = RWKV_TILE // CHUNK
    for c in range(tb // CHUNK):
        tile, ts = c // per_tile, pl.ds((c % per_tile) * CHUNK, CHUNK)
        s = s_ref[...]
        ys_ref[:, pl.ds(c * CHUNK, CHUNK), :] = _bmm(q_ref[tile, :, ts, :], s) + y_ref[tile, :, ts, :]
        s_ref[...] = _bmm(m_ref[tile, :, ts, :], s) + n_ref[tile, :, ts, :]
    y = jnp.concatenate([ys_ref[h] for h in range(RWKV_HEADS)], axis=-1)
    gmat = gmat_ref[...]
    mu = _group_sum(y, gmat) * (1.0 / RWKV_HEAD)
    yc = y - mu
    var = _group_sum(yc * yc, gmat) * (1.0 / RWKV_HEAD)
    yn = yc * lax.rsqrt(var + RWKV_GN_EPS) * lnw_ref[...] + lnb_ref[...]
    o_ref[...] = (yn + bv_ref[...]) * g_ref[...]


def rwkv_scan(m, nn, qt, y1, g, bv, ln_w, ln_b, batch, t, tb=256):
    n = g.shape[0]
    W = RWKV_WIDTH
    nt = t // tb
    gmat, _ = _rwkv_consts()
    hm = pl.BlockSpec((tb // RWKV_TILE, RWKV_HEADS, RWKV_TILE, RWKV_HEAD), lambda b, i: (b * nt + i, 0, 0, 0))
    nat = pl.BlockSpec((tb, W), lambda b, i: (b * nt + i, 0))
    full = lambda shape: pl.BlockSpec(shape, lambda b, i: (0,) * len(shape))
    return pl.pallas_call(
        _rwkv_scan_kernel,
        grid=(batch, nt),
        in_specs=[hm, hm, hm, hm, nat, nat, full((1, W)), full((1, W)), full((GROUP_LANES, GROUP_LANES))],
        out_specs=nat,
        out_shape=jax.ShapeDtypeStruct((n, W), F32),
        scratch_shapes=[pltpu.VMEM((RWKV_HEADS, RWKV_HEAD, RWKV_HEAD), F32),
                        pltpu.VMEM((RWKV_HEADS, tb, RWKV_HEAD), F32)],
        compiler_params=_cparams(("arbitrary", "arbitrary")),
        name="rwkv_scan",
    )(m, nn, qt, y1, g, bv, ln_w.reshape(1, W), ln_b.reshape(1, W), gmat)


PEER_HEADS = 8
PEER_NKEYS = 128
PEER_NEXPERTS = PEER_NKEYS * PEER_NKEYS
PEER_DQ = 256
PEER_TOPK = 16
PEER_ACT_CHUNKS = 4
PEER_PRE_PIECES = 2
PEER_TOKEN_TILE = 256


def _merge_query_kernel(x_ref, gr_ref, gw_ref, oret_ref, orwkv_ref, wr_ref, ww_ref, wo_ref, nw_ref, wq_ref,
                        k1_ref, k2_ref, x1_ref, s_ref, ht_ref):
    pr = jnp.dot(oret_ref[...].astype(BF16), wr_ref[...], preferred_element_type=F32)
    pw = jnp.dot(orwkv_ref[...].astype(BF16), ww_ref[...], preferred_element_type=F32)
    merged = jax.nn.sigmoid(gr_ref[...]) * pr + jax.nn.sigmoid(gw_ref[...]) * pw
    x1 = x_ref[...] + jnp.dot(merged.astype(BF16), wo_ref[...], preferred_element_type=F32)
    x1_ref[...] = x1
    ms = jnp.mean(x1 * x1, axis=-1, keepdims=True)
    h2 = x1 * lax.rsqrt(ms + NORM_EPS) * nw_ref[...]
    ht_ref[...] = h2.T.astype(BF16)
    q = jnp.dot(h2.astype(BF16), wq_ref[...], preferred_element_type=F32).astype(BF16)
    half = PEER_DQ // 2
    for h in range(PEER_HEADS):
        for p, kref in enumerate((k1_ref, k2_ref)):
            qh = q[:, h * PEER_DQ + p * half: h * PEER_DQ + (p + 1) * half]
            s_ref[p, h] = lax.dot_general(kref[...], qh, (((1,), (1,)), ((), ())), preferred_element_type=F32)


def merge_query(x, zp, o_ret, o_rwkv, w_ret, w_rwkv, w_out, norm_w, w_q, keys_1, keys_2, tm=PEER_TOKEN_TILE):
    n, d = x.shape
    const = lambda shape: pl.BlockSpec(shape, lambda i: (0, 0), pipeline_mode=pl.Buffered(1))

    def gate(col0):
        return pl.BlockSpec((pl.Element(tm), pl.Element(d)), lambda i: (i * tm, col0))

    return pl.pallas_call(
        _merge_query_kernel,
        grid=(n // tm,),
        in_specs=[pl.BlockSpec((tm, d), lambda i: (i, 0)), gate(ZP_GATE_RET), gate(ZP_GATE_RWKV),
                  pl.BlockSpec((tm, RET_WIDTH), lambda i: (i, 0)),
                  pl.BlockSpec((tm, RWKV_WIDTH), lambda i: (i, 0)),
                  const((RET_WIDTH, d)), const((RWKV_WIDTH, d)), const((d, d)),
                  const((1, d)), const((d, PEER_HEADS * PEER_DQ)),
                  const((PEER_NKEYS, PEER_DQ // 2)), const((PEER_NKEYS, PEER_DQ // 2))],
        out_specs=[
            pl.BlockSpec((tm, d), lambda i: (i, 0)),
            pl.BlockSpec((None, 2, PEER_HEADS, PEER_NKEYS, tm), lambda i: (i, 0, 0, 0, 0)),
            pl.BlockSpec((None, d, tm), lambda i: (i, 0, 0)),
        ],
        out_shape=[
            jax.ShapeDtypeStruct((n, d), F32),
            jax.ShapeDtypeStruct((n // tm, 2, PEER_HEADS, PEER_NKEYS, tm), F32),
            jax.ShapeDtypeStruct((n // tm, d, tm), BF16),
        ],
        compiler_params=_cparams(("arbitrary",)),
        name="merge_query",
    )(x, zp, zp, o_ret, o_rwkv, w_ret.astype(BF16), w_rwkv.astype(BF16), w_out.astype(BF16),
      norm_w.reshape(1, d), w_q.astype(BF16), keys_1.astype(BF16), keys_2.astype(BF16))


_CAND_GROUPS = [(0, 0), (0, 8)] + [(a, 0) for a in range(1, 8)]


def _peer_topk_kernel(s_ref, lim_ref, rank2_ref, e1_ref, e2_ref, v1_s, r1_s, v2_s, r2_s, cnt_s, z_s):
    tn = s_ref.shape[-1]
    K = PEER_TOPK
    neg = -jnp.inf
    rowid = lax.broadcasted_iota(jnp.int32, (PEER_NKEYS, tn), 0)
    nrow = 8 * (len(_CAND_GROUPS) + 1)
    r = lax.broadcasted_iota(jnp.int32, (nrow, tn), 0)
    grp, sub = r // 8, r % 8
    ca = jnp.where(grp < 2, 0, jnp.where(grp < 9, grp - 1, 8 + sub))
    cb = jnp.where(grp == 1, 8 + sub, jnp.where(grp < 9, sub, 0))
    flat = ca * K + cb
    valid = (ca + 1) * (cb + 1) <= K

    def top16(s, break_ties):
        rank = jnp.full((PEER_NKEYS, tn), K, jnp.int32)
        vals = []
        for a in range(K):
            m = jnp.max(s, axis=0, keepdims=True)
            hit = s == m
            if break_ties:
                idx = jnp.min(jnp.where(hit, rowid, PEER_NKEYS), axis=0, keepdims=True)
                hit = rowid == idx
            rank = jnp.where(hit, a, rank)
            s = jnp.where(hit, neg, s)
            vals.append(m)
        return jnp.concatenate(vals, axis=0), rank

    def head(h, carry):
        s1 = s_ref[0, h]
        s2 = s_ref[1, h]
        v1_s[...], r1_s[...] = top16(s1, False)
        v2_s[...], r2_s[...] = top16(s2, False)
        ranked = (jnp.sum((r1_s[...] < K).astype(jnp.int32), axis=0, keepdims=True)
                  + jnp.sum((r2_s[...] < K).astype(jnp.int32), axis=0, keepdims=True))
        tied = jnp.max(jnp.abs(ranked - 2 * K)) > 0

        @pl.when(tied)
        def _():
            v1_s[...], r1_s[...] = top16(s1, True)
            v2_s[...], r2_s[...] = top16(s2, True)

        v1, rank1 = v1_s[...], r1_s[...]
        v2, rank2 = v2_s[...], r2_s[...]
        pieces = [v1[a:a + 1] + v2[b0:b0 + 8] for a, b0 in _CAND_GROUPS] + [v1[8:16] + v2[0:1]]
        cand = jnp.where(valid, jnp.concatenate(pieces, axis=0), neg)
        m0 = v1[0:1] + v2[0:1]

        def best16(cand, break_ties):
            z = jnp.zeros((1, tn), F32)
            for _ in range(K):
                m = jnp.max(cand, axis=0, keepdims=True)
                hit = cand == m
                if break_ties:
                    f = jnp.min(jnp.where(hit, flat, K * K), axis=0, keepdims=True)
                    hit = flat == f
                cand = jnp.where(hit, neg, cand)
                z = z + jnp.exp(m - m0)
            chosen = jnp.where(jnp.logical_and(valid, cand == neg), 1.0, 0.0)
            per_a = [jnp.sum(chosen[0:16], axis=0, keepdims=True)]
            per_a += [jnp.sum(chosen[8 * (a + 1):8 * (a + 2)], axis=0, keepdims=True) for a in range(1, 8)]
            return jnp.concatenate(per_a + [chosen[8 * 9:8 * 10]], axis=0), z

        cnt_s[...], z_s[...] = best16(cand, False)
        tied2 = jnp.max(jnp.abs(jnp.sum(cnt_s[...], axis=0, keepdims=True) - K)) > 0

        @pl.when(tied2)
        def _():
            cnt_s[...], z_s[...] = best16(cand, True)

        count = cnt_s[...]
        inv_z = 1.0 / z_s[...]
        e1_ref[h] = jnp.where(rank1 < K, jnp.exp(s1 - v1[0:1]) * inv_z, 0.0)
        e2_ref[h] = jnp.where(rank2 < K, jnp.exp(s2 - v2[0:1]), 0.0).astype(BF16)
        rank2_ref[h] = rank2.astype(F32).astype(BF16)
        lim = jnp.zeros((PEER_NKEYS, tn), F32)
        for a in range(K):
            lim = jnp.where(rank1 == a, count[a:a + 1], lim)
        lim_ref[h] = lim
        return carry

    lax.fori_loop(0, PEER_HEADS, head, 0)


def peer_topk(s_t):
    ntile, _, _, _, tn = s_t.shape
    n = ntile * tn
    spec = pl.BlockSpec((PEER_HEADS, PEER_NKEYS, tn), lambda i: (0, 0, i))
    shp = lambda dt: jax.ShapeDtypeStruct((PEER_HEADS, PEER_NKEYS, n), dt)
    return pl.pallas_call(
        _peer_topk_kernel,
        grid=(ntile,),
        in_specs=[pl.BlockSpec((None, 2, PEER_HEADS, PEER_NKEYS, tn), lambda i: (i, 0, 0, 0, 0))],
        out_specs=[spec, spec, spec, spec],
        out_shape=[shp(F32), shp(BF16), shp(F32), shp(BF16)],
        scratch_shapes=[pltpu.VMEM((PEER_TOPK, tn), F32), pltpu.VMEM((PEER_NKEYS, tn), jnp.int32),
                        pltpu.VMEM((PEER_TOPK, tn), F32), pltpu.VMEM((PEER_NKEYS, tn), jnp.int32),
                        pltpu.VMEM((PEER_TOPK, tn), F32), pltpu.VMEM((1, tn), F32)],
        compiler_params=_cparams(("arbitrary",)),
        name="peer_topk",
    )(s_t)


def _peer_ffn_kernel(x_ref, ht_ref, lim_ref, rank2_ref, e1_ref, e2_ref, u_ref, vt_ref, fnw_ref, o_ref,
                     acc_ref, *scratch):
    j = pl.program_id(1)
    eb = u_ref.shape[0]
    nsub = eb // PEER_NKEYS
    pre_refs, act_refs = scratch[:PEER_PRE_PIECES], scratch[PEER_PRE_PIECES:]
    piece = eb // len(pre_refs)
    per_chunk = nsub // len(act_refs)
    ck = per_chunk * PEER_NKEYS
    zero = jnp.zeros((), BF16)

    @pl.when(j == 0)
    def _():
        acc_ref[...] = jnp.zeros_like(acc_ref)

    tt = ht_ref.shape[-1]
    for k, pre_k in enumerate(pre_refs):
        for sub in range(ht_ref.shape[0]):
            pre_k[:, pl.ds(sub * tt, tt)] = jnp.dot(u_ref[pl.ds(k * piece, piece), :], ht_ref[sub],
                                                    preferred_element_type=F32)
    for c, act_c in enumerate(act_refs):
        for jc in range(per_chunk):
            jj = c * per_chunk + jc
            gate = None
            for h in range(PEER_HEADS):
                lrow = lim_ref[h, jj:jj + 1, :].astype(BF16)
                erow = e1_ref[h, jj:jj + 1, :].astype(BF16)
                term = jnp.where(rank2_ref[h] < lrow, e2_ref[h], zero) * erow
                gate = term if gate is None else gate + term
            p = pre_refs[jj * PEER_NKEYS // piece][pl.ds(jj * PEER_NKEYS % piece, PEER_NKEYS), :]
            gelu = 0.5 * p * (1.0 + lax.erf(p * (2.0 ** -0.5)))
            act_c[pl.ds(jc * PEER_NKEYS, PEER_NKEYS), :] = gelu.astype(BF16) * gate
        acc_ref[...] += jnp.dot(vt_ref[:, pl.ds(c * ck, ck)], act_c[...], preferred_element_type=F32)

    @pl.when(j == pl.num_programs(1) - 1)
    def _():
        y = x_ref[...] + acc_ref[...].T
        ms = jnp.mean(y * y, axis=-1, keepdims=True)
        o_ref[...] = y * lax.rsqrt(ms + NORM_EPS) * fnw_ref[...]


def peer_ffn(x1, h2t, lim, rank2, e1, e2, expert_u, expert_v, final_norm_w, tn=512, eb=1024):
    n, d = x1.shape
    ne = expert_u.shape[0]
    u = expert_u.astype(BF16)
    vt = expert_v.T.astype(BF16)
    nsub = eb // PEER_NKEYS
    assert nsub == 8
    tok = pl.BlockSpec((PEER_HEADS, PEER_NKEYS, tn), lambda i, j: (0, 0, i))
    row = pl.BlockSpec((PEER_HEADS, nsub, tn), lambda i, j: (0, j, i))
    return pl.pallas_call(
        _peer_ffn_kernel,
        grid=(n // tn, ne // eb),
        in_specs=[
            pl.BlockSpec((tn, d), lambda i, j: (i, 0)),
            pl.BlockSpec((tn // h2t.shape[-1], d, h2t.shape[-1]), lambda i, j: (i, 0, 0)),
            row, tok, row, tok,
            pl.BlockSpec((eb, d), lambda i, j: (j, 0)),
            pl.BlockSpec((d, eb), lambda i, j: (0, j)),
            pl.BlockSpec((1, d), lambda i, j: (0, 0)),
        ],
        out_specs=pl.BlockSpec((tn, d), lambda i, j: (i, 0)),
        out_shape=jax.ShapeDtypeStruct((n, d), F32),
        scratch_shapes=[pltpu.VMEM((d, tn), F32)]
        + [pltpu.VMEM((eb // PEER_PRE_PIECES, tn), F32)] * PEER_PRE_PIECES
        + [pltpu.VMEM((eb // PEER_ACT_CHUNKS, tn), BF16)] * PEER_ACT_CHUNKS,
        compiler_params=_cparams(("arbitrary", "arbitrary")),
        name="peer_ffn",
    )(x1, h2t, lim, rank2, e1, e2, u, vt, final_norm_w.reshape(1, d))


def kernel(x, norm1_w, w_in, ret_gn_w, ret_gn_b, rwkv_mix, rwkv_w0, rwkv_w_up, rwkv_a0, rwkv_a_up, rwkv_g_up, rwkv_k_k, rwkv_k_a, rwkv_r_k, rwkv_ln_w, rwkv_ln_b, w_ret_branch, w_rwkv_branch, w_out, norm2_w, peer_w_q, peer_keys_1, peer_keys_2, peer_u, peer_v, final_norm_w):
    b, t, d = x.shape
    n = b * t
    assert w_in.shape[0] == 1, "single-layer block: the final norm is fused into the PEER kernel"
    l = 0
    xf = x.reshape(n, d)
    zp = in_proj(xf, norm1_w[l], w_in[l].astype(BF16))
    o_ret = retention(zp, ret_gn_w[l], ret_gn_b[l], b, t)
    pre = rwkv_pre(zp, rwkv_mix[l], rwkv_w0[l], rwkv_w_up[l], rwkv_a0[l], rwkv_a_up[l], rwkv_g_up[l],
                   rwkv_k_k[l], rwkv_k_a[l], rwkv_r_k[l], b, t)
    o_rwkv = rwkv_scan(*pre, rwkv_ln_w[l], rwkv_ln_b[l], b, t)
    x1, s_t, h2t = merge_query(xf, zp, o_ret, o_rwkv, w_ret_branch[l], w_rwkv_branch[l], w_out[l],
                               norm2_w[l], peer_w_q[l], peer_keys_1[l], peer_keys_2[l])
    lim, rank2, e1, e2 = peer_topk(s_t)
    out = peer_ffn(x1, h2t, lim, rank2, e1, e2, peer_u[l], peer_v[l], final_norm_w)
    return out.reshape(b, t, d)
```

```python
import functools

import numpy as np
import jax
import jax.numpy as jnp
from jax import lax
from jax.experimental import pallas as pl
from jax.experimental.pallas import tpu as pltpu

F32 = jnp.float32
BF16 = jnp.bfloat16

D_MODEL = 2048
CHUNK = 64
NORM_EPS = 1e-6
RET_HEADS = 4
RET_DK = 256
RET_DV = 256
RET_QK = RET_HEADS * RET_DK
RET_WIDTH = RET_HEADS * RET_DV
ROPE_BASE = 10000.0
RWKV_HEAD = 64
RWKV_WIDTH = D_MODEL // 2
RWKV_HEADS = RWKV_WIDTH // RWKV_HEAD
DECAY_LORA = 64
AAA_LORA = 64
GATE_LORA = 128
LORA_COLS = DECAY_LORA + AAA_LORA + GATE_LORA
RWKV_COLS = 3 * RWKV_WIDTH + LORA_COLS
RWKV_GN_EPS = 64e-5
IN_COLS = 2 * RET_QK + 2 * RET_WIDTH + RWKV_COLS + 2 * D_MODEL

ZP_RWKV = 2 * RET_QK + 2 * RET_WIDTH
ZP_LORA = ZP_RWKV + 3 * RWKV_WIDTH
ZP_GATE_RET = ZP_RWKV + RWKV_COLS
ZP_GATE_RWKV = ZP_GATE_RET + D_MODEL
assert ZP_LORA % LORA_COLS == 0

VMEM_LIMIT = 56 * 1024 * 1024


def _cparams(sem):
    return pltpu.CompilerParams(dimension_semantics=sem, vmem_limit_bytes=VMEM_LIMIT)


def _in_proj_kernel(x_ref, nw_ref, w_ref, o_ref, h_ref):
    @pl.when(pl.program_id(1) == 0)
    def _():
        x = x_ref[...]
        ms = jnp.mean(x * x, axis=-1, keepdims=True)
        h_ref[...] = (x * lax.rsqrt(ms + NORM_EPS) * nw_ref[...]).astype(BF16)

    o_ref[...] = jnp.dot(h_ref[...], w_ref[...], preferred_element_type=F32)


def in_proj(x, norm_w, w_bf16, tm=1024, tn=1280):
    n, d = x.shape
    cols = w_bf16.shape[1]
    return pl.pallas_call(
        _in_proj_kernel,
        grid=(n // tm, cols // tn),
        in_specs=[
            pl.BlockSpec((tm, d), lambda i, j: (i, 0)),
            pl.BlockSpec((1, d), lambda i, j: (0, 0)),
            pl.BlockSpec((d, tn), lambda i, j: (0, j)),
        ],
        out_specs=pl.BlockSpec((tm, tn), lambda i, j: (i, j)),
        out_shape=jax.ShapeDtypeStruct((n, cols), F32),
        scratch_shapes=[pltpu.VMEM((tm, d), BF16)],
        compiler_params=_cparams(("arbitrary", "arbitrary")),
        name="in_proj",
    )(x, norm_w.reshape(1, d), w_bf16)


def _retention_kernel(q_ref, k_ref, v_ref, g_ref, cos_ref, sin_ref, dintra_ref, qdec_ref, kdec_ref,
                      cdec_ref, gnw_ref, gnb_ref, o_ref, s_ref, qr_ref, kr_ref, acc_ref):
    tb = q_ref.shape[0]
    half = RET_DK // 2

    @pl.when(pl.program_id(1) == 0)
    def _():
        s_ref[...] = jnp.zeros_like(s_ref)

    cos = cos_ref[...]
    sin = sin_ref[...]

    def rot(z_ref, hd):
        z1 = z_ref[:, hd * RET_DK:hd * RET_DK + half]
        z2 = z_ref[:, hd * RET_DK + half:(hd + 1) * RET_DK]
        return jnp.concatenate([z1 * cos - z2 * sin, z2 * cos + z1 * sin], axis=-1)

    for hd in range(RET_HEADS):
        cols = pl.ds(hd * RET_DK, RET_DK)
        qr_ref[:, cols] = rot(q_ref, hd)
        kr_ref[:, cols] = rot(k_ref, hd) * (RET_DK ** -0.5)
    for c in range(tb // CHUNK):
        sl = pl.ds(c * CHUNK, CHUNK)
        for hd in range(RET_HEADS):
            cols = pl.ds(hd * RET_DK, RET_DK)
            qc = qr_ref[sl, cols]
            kc = kr_ref[sl, cols]
            vc = v_ref[sl, cols].astype(BF16)
            scores = lax.dot_general(qc.astype(BF16), kc.astype(BF16), (((1,), (1,)), ((), ())),
                                     preferred_element_type=F32) * dintra_ref[hd]
            s_prev = s_ref[hd]
            o = jnp.dot(scores.astype(BF16), vc, preferred_element_type=F32)
            o = o + jnp.dot((qc * qdec_ref[hd]).astype(BF16), s_prev.astype(BF16), preferred_element_type=F32)
            kd_t = (kc * kdec_ref[hd]).T.astype(BF16)
            s_ref[hd] = s_prev * cdec_ref[hd] + jnp.dot(kd_t, vc, preferred_element_type=F32)
            acc_ref[sl, cols] = o
    for hd in range(RET_HEADS):
        cols = pl.ds(hd * RET_DK, RET_DK)
        o = acc_ref[:, cols]
        mu = jnp.mean(o, axis=-1, keepdims=True)
        oc = o - mu
        var = jnp.mean(oc * oc, axis=-1, keepdims=True)
        y = oc * lax.rsqrt(var + NORM_EPS) * gnw_ref[:, cols] + gnb_ref[:, cols]
        g = g_ref[:, cols]
        o_ref[:, cols] = y * (g * jax.nn.sigmoid(g))


def _retention_tables(t):
    h = RET_HEADS
    log_g = jnp.log(1.0 - jnp.exp2(-5.0 - jnp.arange(h, dtype=F32)))
    n = jnp.arange(CHUNK, dtype=F32)
    d_intra = jnp.exp(log_g[:, None, None] * jnp.abs(n[:, None] - n[None, :]))
    q_decay = jnp.exp(log_g[:, None] * (n[None, :] + 1.0))
    k_decay = jnp.exp(log_g[:, None] * (CHUNK - 1.0 - n[None, :]))
    chunk_decay = jnp.exp(log_g * CHUNK)
    q_decay = jnp.broadcast_to(q_decay[:, :, None], (h, CHUNK, RET_DK))
    k_decay = jnp.broadcast_to(k_decay[:, :, None], (h, CHUNK, RET_DK))
    chunk_decay = jnp.broadcast_to(chunk_decay[:, None, None], (h, 1, RET_DV))
    half = RET_DK // 2
    inv_freq = ROPE_BASE ** (-jnp.arange(half, dtype=F32) * 2.0 / RET_DK)
    ang = jnp.arange(t, dtype=jnp.int32).astype(F32)[:, None] * inv_freq[None, :]
    return d_intra, q_decay, k_decay, chunk_decay, jnp.cos(ang), jnp.sin(ang)


def retention(zp, gn_w, gn_b, batch, t, tb=512):
    n = zp.shape[0]
    h = RET_HEADS
    nt = t // tb
    d_intra, q_decay, k_decay, chunk_decay, cos, sin = _retention_tables(t)

    def zspec(col_block):
        return pl.BlockSpec((tb, RET_WIDTH), lambda b, i: (b * nt + i, col_block))

    tab = lambda shape: pl.BlockSpec((h,) + shape, lambda b, i: (0, 0, 0))
    return pl.pallas_call(
        _retention_kernel,
        grid=(batch, nt),
        in_specs=[
            zspec(0), zspec(1), zspec(2), zspec(3),
            pl.BlockSpec((tb, RET_DK // 2), lambda b, i: (i, 0)),
            pl.BlockSpec((tb, RET_DK // 2), lambda b, i: (i, 0)),
            tab((CHUNK, CHUNK)), tab((CHUNK, RET_DK)), tab((CHUNK, RET_DK)), tab((1, RET_DV)),
            pl.BlockSpec((1, RET_WIDTH), lambda b, i: (0, 0)),
            pl.BlockSpec((1, RET_WIDTH), lambda b, i: (0, 0)),
        ],
        out_specs=pl.BlockSpec((tb, RET_WIDTH), lambda b, i: (b * nt + i, 0)),
        out_shape=jax.ShapeDtypeStruct((n, RET_WIDTH), F32),
        scratch_shapes=[
            pltpu.VMEM((h, RET_DK, RET_DV), F32),
            pltpu.VMEM((tb, RET_QK), F32),
            pltpu.VMEM((tb, RET_QK), F32),
            pltpu.VMEM((tb, RET_WIDTH), F32),
        ],
        compiler_params=_cparams(("arbitrary", "arbitrary")),
        name="retention",
    )(zp, zp, zp, zp, cos, sin, d_intra, q_decay, k_decay, chunk_decay,
      gn_w.reshape(1, RET_WIDTH), gn_b.reshape(1, RET_WIDTH))


def _split2(x):
    hi = x.astype(BF16)
    lo = (x - hi.astype(F32)).astype(BF16)
    return hi, lo


def _split3(x):
    hi = x.astype(BF16)
    r = x - hi.astype(F32)
    mid = r.astype(BF16)
    lo = (r - mid.astype(F32)).astype(BF16)
    return hi, mid, lo


def _group_sum(x, gmat):
    hi, lo = _split2(x)
    lanes = gmat.shape[0]
    outs = []
    for b in range(x.shape[1] // lanes):
        sl = slice(b * lanes, (b + 1) * lanes)
        outs.append(jnp.dot(hi[:, sl], gmat, preferred_element_type=F32)
                    + jnp.dot(lo[:, sl], gmat, preferred_element_type=F32))
    return jnp.concatenate(outs, axis=-1)


def _bmm(a, b):
    return lax.dot_general(a.astype(BF16), b.astype(BF16), (((2,), (1,)), ((0,), (0,))),
                           preferred_element_type=F32)


def _bmm_nt(a, b):
    return lax.dot_general(a.astype(BF16), b.astype(BF16), (((2,), (2,)), ((0,), (0,))),
                           preferred_element_type=F32)


def _bmm_tn(a, b):
    return lax.dot_general(a.astype(BF16), b.astype(BF16), (((1,), (1,)), ((0,), (0,))),
                           preferred_element_type=F32)


RWKV_HEAD_GROUP = 16
GROUP_LANES = 128
RWKV_TILE = 128


def _rwkv_kernel(r_ref, k_ref, v_ref, l_ref, pr_ref, pk_ref, pv_ref, plr_ref,
                     mixr_ref, mixk_ref, mixv_ref, mixl_ref, w0_ref, a0_ref, kk_ref, ka_ref, rk_ref,
                     wup_ref, aup_ref, gup_ref, gmat_ref, ltri_ref, lnw_ref, lnb_ref,
                     o_ref,
                     at_s, rt_s, bh_s, kh_s, bt_s, kt_s, v_s, gc_s, g_s, bv_s, s_ref, ys_ref, *, tiles_per_batch):
    tb = r_ref.shape[0]
    nchunk = tb // CHUNK
    first = (pl.program_id(0) % tiles_per_batch) == 0
    rowid = lax.broadcasted_iota(jnp.int32, (tb, 1), 0)

    @pl.when(first)
    def _():
        s_ref[...] = jnp.zeros_like(s_ref)

    def shift_lerp(cur_ref, prev_ref, mix_ref):
        cur = cur_ref[...]
        prev_row = jnp.where(first, 0.0, prev_ref[7:8, :])
        prev = jnp.where(rowid == 0, prev_row, pltpu.roll(cur, 1, axis=0))
        return cur + (prev - cur) * mix_ref[...]

    r = shift_lerp(r_ref, pr_ref, mixr_ref)
    k = shift_lerp(k_ref, pk_ref, mixk_ref)
    v = shift_lerp(v_ref, pv_ref, mixv_ref)
    lo = shift_lerp(l_ref, plr_ref, mixl_ref)
    xw = lo[:, :DECAY_LORA]
    xa = lo[:, DECAY_LORA:DECAY_LORA + AAA_LORA]
    xg = lo[:, DECAY_LORA + AAA_LORA:]
    wl = jnp.dot(jnp.tanh(xw).astype(BF16), wup_ref[...], preferred_element_type=F32)
    u = -(w0_ref[...] + wl)
    softplus = jnp.maximum(u, 0.0) + jnp.log(1.0 + jnp.exp(-jnp.abs(u)))
    logw = -jnp.exp(-softplus - 0.5)
    a = jax.nn.sigmoid(a0_ref[...] + jnp.dot(xa.astype(BF16), aup_ref[...], preferred_element_type=F32))
    g_s[...] = jnp.dot(jax.nn.sigmoid(xg).astype(BF16), gup_ref[...], preferred_element_type=F32)
    gmat = gmat_ref[...]
    kk = k * kk_ref[...]
    kk = kk / jnp.maximum(jnp.sqrt(_group_sum(kk * kk, gmat)), 1e-12)
    k2 = k * (1.0 + (a - 1.0) * ka_ref[...])
    bv_s[...] = _group_sum(r * k2 * rk_ref[...], gmat) * v
    beta = kk * a
    ltri = ltri_ref[...]

    def to_heads(dst, c, val):
        for h in range(RWKV_HEADS):
            dst[h, pl.ds(c * CHUNK, CHUNK), :] = val[:, h * RWKV_HEAD:(h + 1) * RWKV_HEAD]

    for c in range(nchunk):
        sl = slice(c * CHUNK, (c + 1) * CHUNK)
        lw = logw[sl]
        h1, h2, h3 = _split3(lw)
        cs = (jnp.dot(ltri, h1, preferred_element_type=F32) + jnp.dot(ltri, h2, preferred_element_type=F32)
              + jnp.dot(ltri, h3, preferred_element_type=F32))
        cs_last = cs[CHUNK - 1:CHUNK, :]
        g_t = jnp.exp(cs)
        g_tm1 = jnp.exp(cs - lw)
        inv_g = jnp.exp(-cs)
        g_end = jnp.exp(cs_last - cs)
        to_heads(at_s, c, -kk[sl] * g_tm1)
        to_heads(rt_s, c, r[sl] * g_t)
        to_heads(bh_s, c, beta[sl] * inv_g)
        to_heads(kh_s, c, k2[sl] * inv_g)
        to_heads(bt_s, c, beta[sl] * g_end)
        to_heads(kt_s, c, k2[sl] * g_end)
        to_heads(v_s, c, v[sl])
        gc = jnp.exp(cs_last)
        for h in range(RWKV_HEADS):
            gc_s[h, pl.ds(c * 8, 8), :] = jnp.broadcast_to(gc[:, h * RWKV_HEAD:(h + 1) * RWKV_HEAD], (8, RWKV_HEAD))

    G = RWKV_HEAD_GROUP
    ti = lax.broadcasted_iota(jnp.int32, (CHUNK, CHUNK), 0)
    si = lax.broadcasted_iota(jnp.int32, (CHUNK, CHUNK), 1)
    strict = (si < ti)[None]
    incl = (si <= ti)[None]
    eye = (si == ti).astype(F32)[None]

    def body(idx, carry):
        c = idx // (RWKV_HEADS // G)
        hg = idx % (RWKV_HEADS // G)
        hs = pl.ds(pl.multiple_of(hg * G, G), G)
        ts = pl.ds(pl.multiple_of(c * CHUNK, CHUNK), CHUNK)
        at = at_s[hs, ts, :]
        rt = rt_s[hs, ts, :]
        bh = bh_s[hs, ts, :]
        kh = kh_s[hs, ts, :]
        bt = bt_s[hs, ts, :]
        kt = kt_s[hs, ts, :]
        vv = v_s[hs, ts, :]
        gc = gc_s[hs, pl.ds(pl.multiple_of(c * 8, 8), 1), :]
        aab = jnp.where(strict, _bmm_nt(at, bh), 0.0)
        aak = jnp.where(strict, _bmm_nt(at, kh), 0.0)
        aqb = jnp.where(incl, _bmm_nt(rt, bh), 0.0)
        aqk = jnp.where(incl, _bmm_nt(rt, kh), 0.0)
        p = aab
        tm = eye + aab
        for _ in range(5):
            p = _bmm(p, p)
            tm = tm + _bmm(p, tm)
        akv = _bmm(aak, vv)
        wt = _bmm(tm, at)
        u0 = _bmm(tm, akv)
        qt = rt + _bmm(aqb, wt)
        y1 = _bmm(aqb, u0) + _bmm(aqk, vv)
        m = _bmm_tn(bt, wt) + eye * gc
        nn = _bmm_tn(bt, u0) + _bmm_tn(kt, vv)
        s = s_ref[hs]
        ys_ref[hs, ts, :] = _bmm(qt, s) + y1
        s_ref[hs] = _bmm(m, s) + nn
        return carry

    lax.fori_loop(0, nchunk * (RWKV_HEADS // G), body, 0)

    y = jnp.concatenate([ys_ref[h] for h in range(RWKV_HEADS)], axis=-1)
    mu = _group_sum(y, gmat) * (1.0 / RWKV_HEAD)
    yc = y - mu
    var = _group_sum(yc * yc, gmat) * (1.0 / RWKV_HEAD)
    yn = yc * lax.rsqrt(var + RWKV_GN_EPS) * lnw_ref[...] + lnb_ref[...]
    o_ref[...] = (yn + bv_s[...]) * g_s[...]


def _rwkv_consts():
    lane = np.arange(GROUP_LANES) // RWKV_HEAD
    gmat = (lane[:, None] == lane[None, :]).astype(np.float32)
    ltri = np.tril(np.ones((CHUNK, CHUNK), np.float32))
    return jnp.asarray(gmat, BF16), jnp.asarray(ltri, BF16)


def rwkv_time_mix(zp, mix, w0, w_up, a0, a_up, g_up, k_k, k_a, r_k, ln_w, ln_b, batch, t, tb=RWKV_TILE):
    n = zp.shape[0]
    W = RWKV_WIDTH
    nt = t // tb
    gmat, ltri = _rwkv_consts()
    row = lambda p: p.reshape(1, -1)
    mix_r, mix_k, mix_v, mix_l = mix[:W], mix[W:2 * W], mix[2 * W:3 * W], mix[3 * W:]
    cb = ZP_RWKV // W
    cur = lambda j: pl.BlockSpec((tb, W), lambda i: (i, cb + j))
    prev = lambda j: pl.BlockSpec((8, W), lambda i: (jnp.maximum(i * (tb // 8) - 1, 0), cb + j))
    full = lambda shape: pl.BlockSpec(shape, lambda i: (0,) * len(shape))
    hscr = pltpu.VMEM((RWKV_HEADS, tb, RWKV_HEAD), F32)
    return pl.pallas_call(
        functools.partial(_rwkv_kernel, tiles_per_batch=nt),
        grid=(n // tb,),
        in_specs=[
            cur(0), cur(1), cur(2),
            pl.BlockSpec((tb, LORA_COLS), lambda i: (i, ZP_LORA // LORA_COLS)),
            prev(0), prev(1), prev(2),
            pl.BlockSpec((8, LORA_COLS), lambda i: (jnp.maximum(i * (tb // 8) - 1, 0), ZP_LORA // LORA_COLS)),
            full((1, W)), full((1, W)), full((1, W)), full((1, LORA_COLS)),
            full((1, W)), full((1, W)), full((1, W)), full((1, W)), full((1, W)),
            full((DECAY_LORA, W)), full((AAA_LORA, W)), full((GATE_LORA, W)),
            full((GROUP_LANES, GROUP_LANES)), full((CHUNK, CHUNK)), full((1, W)), full((1, W)),
        ],
        out_specs=pl.BlockSpec((tb, W), lambda i: (i, 0)),
        out_shape=jax.ShapeDtypeStruct((n, W), F32),
        scratch_shapes=[hscr] * 7 + [pltpu.VMEM((RWKV_HEADS, 8 * (tb // CHUNK), RWKV_HEAD), F32),
                                     pltpu.VMEM((tb, W), F32), pltpu.VMEM((tb, W), F32),
                                     pltpu.VMEM((RWKV_HEADS, RWKV_HEAD, RWKV_HEAD), F32), hscr],
        compiler_params=_cparams(("arbitrary",)),
        name="rwkv_time_mix",
    )(zp, zp, zp, zp, zp, zp, zp, zp,
      row(mix_r), row(mix_k), row(mix_v), row(mix_l), row(w0), row(a0), row(k_k), row(k_a), row(r_k),
      w_up.astype(BF16), a_up.astype(BF16), g_up.astype(BF16), gmat, ltri, row(ln_w), row(ln_b))


PEER_HEADS = 8
PEER_NKEYS = 128
PEER_NEXPERTS = PEER_NKEYS * PEER_NKEYS
PEER_DQ = 256
PEER_TOPK = 16
PEER_ACT_CHUNKS = 4
PEER_PRE_PIECES = 2
PEER_TOKEN_TILE = 256


def _merge_query_kernel(x_ref, gr_ref, gw_ref, oret_ref, orwkv_ref, wr_ref, ww_ref, wo_ref, nw_ref, wq_ref,
                        k1_ref, k2_ref, x1_ref, s_ref, ht_ref):
    pr = jnp.dot(oret_ref[...].astype(BF16), wr_ref[...], preferred_element_type=F32)
    pw = jnp.dot(orwkv_ref[...].astype(BF16), ww_ref[...], preferred_element_type=F32)
    merged = jax.nn.sigmoid(gr_ref[...]) * pr + jax.nn.sigmoid(gw_ref[...]) * pw
    x1 = x_ref[...] + jnp.dot(merged.astype(BF16), wo_ref[...], preferred_element_type=F32)
    x1_ref[...] = x1
    ms = jnp.mean(x1 * x1, axis=-1, keepdims=True)
    h2 = x1 * lax.rsqrt(ms + NORM_EPS) * nw_ref[...]
    ht_ref[...] = h2.T.astype(BF16)
    q = jnp.dot(h2.astype(BF16), wq_ref[...], preferred_element_type=F32).astype(BF16)
    half = PEER_DQ // 2
    for h in range(PEER_HEADS):
        for p, kref in enumerate((k1_ref, k2_ref)):
            qh = q[:, h * PEER_DQ + p * half: h * PEER_DQ + (p + 1) * half]
            s_ref[p, h] = lax.dot_general(kref[...], qh, (((1,), (1,)), ((), ())), preferred_element_type=F32)


def merge_query(x, zp, o_ret, o_rwkv, w_ret, w_rwkv, w_out, norm_w, w_q, keys_1, keys_2, tm=PEER_TOKEN_TILE):
    n, d = x.shape
    const = lambda shape: pl.BlockSpec(shape, lambda i: (0, 0), pipeline_mode=pl.Buffered(1))

    def gate(col0):
        return pl.BlockSpec((pl.Element(tm), pl.Element(d)), lambda i: (i * tm, col0))

    return pl.pallas_call(
        _merge_query_kernel,
        grid=(n // tm,),
        in_specs=[pl.BlockSpec((tm, d), lambda i: (i, 0)), gate(ZP_GATE_RET), gate(ZP_GATE_RWKV),
                  pl.BlockSpec((tm, RET_WIDTH), lambda i: (i, 0)),
                  pl.BlockSpec((tm, RWKV_WIDTH), lambda i: (i, 0)),
                  const((RET_WIDTH, d)), const((RWKV_WIDTH, d)), const((d, d)),
                  const((1, d)), const((d, PEER_HEADS * PEER_DQ)),
                  const((PEER_NKEYS, PEER_DQ // 2)), const((PEER_NKEYS, PEER_DQ // 2))],
        out_specs=[
            pl.BlockSpec((tm, d), lambda i: (i, 0)),
            pl.BlockSpec((None, 2, PEER_HEADS, PEER_NKEYS, tm), lambda i: (i, 0, 0, 0, 0)),
            pl.BlockSpec((None, d, tm), lambda i: (i, 0, 0)),
        ],
        out_shape=[
            jax.ShapeDtypeStruct((n, d), F32),
            jax.ShapeDtypeStruct((n // tm, 2, PEER_HEADS, PEER_NKEYS, tm), F32),
            jax.ShapeDtypeStruct((n // tm, d, tm), BF16),
        ],
        compiler_params=_cparams(("arbitrary",)),
        name="merge_query",
    )(x, zp, zp, o_ret, o_rwkv, w_ret.astype(BF16), w_rwkv.astype(BF16), w_out.astype(BF16),
      norm_w.reshape(1, d), w_q.astype(BF16), keys_1.astype(BF16), keys_2.astype(BF16))


_CAND_GROUPS = [(0, 0), (0, 8)] + [(a, 0) for a in range(1, 8)]


def _peer_topk_kernel(s_ref, lim_ref, rank2_ref, e1_ref, e2_ref, v1_s, r1_s, v2_s, r2_s, cnt_s, z_s):
    tn = s_ref.shape[-1]
    K = PEER_TOPK
    neg = -jnp.inf
    rowid = lax.broadcasted_iota(jnp.int32, (PEER_NKEYS, tn), 0)
    nrow = 8 * (len(_CAND_GROUPS) + 1)
    r = lax.broadcasted_iota(jnp.int32, (nrow, tn), 0)
    grp, sub = r // 8, r % 8
    ca = jnp.where(grp < 2, 0, jnp.where(grp < 9, grp - 1, 8 + sub))
    cb = jnp.where(grp == 1, 8 + sub, jnp.where(grp < 9, sub, 0))
    flat = ca * K + cb
    valid = (ca + 1) * (cb + 1) <= K

    def top16(s, break_ties):
        rank = jnp.full((PEER_NKEYS, tn), K, jnp.int32)
        vals = []
        for a in range(K):
            m = jnp.max(s, axis=0, keepdims=True)
            hit = s == m
            if break_ties:
                idx = jnp.min(jnp.where(hit, rowid, PEER_NKEYS), axis=0, keepdims=True)
                hit = rowid == idx
            rank = jnp.where(hit, a, rank)
            s = jnp.where(hit, neg, s)
            vals.append(m)
        return jnp.concatenate(vals, axis=0), rank

    def head(h, carry):
        s1 = s_ref[0, h]
        s2 = s_ref[1, h]
        v1_s[...], r1_s[...] = top16(s1, False)
        v2_s[...], r2_s[...] = top16(s2, False)
        ranked = (jnp.sum((r1_s[...] < K).astype(jnp.int32), axis=0, keepdims=True)
                  + jnp.sum((r2_s[...] < K).astype(jnp.int32), axis=0, keepdims=True))
        tied = jnp.max(jnp.abs(ranked - 2 * K)) > 0

        @pl.when(tied)
        def _():
            v1_s[...], r1_s[...] = top16(s1, True)
            v2_s[...], r2_s[...] = top16(s2, True)

        v1, rank1 = v1_s[...], r1_s[...]
        v2, rank2 = v2_s[...], r2_s[...]
        pieces = [v1[a:a + 1] + v2[b0:b0 + 8] for a, b0 in _CAND_GROUPS] + [v1[8:16] + v2[0:1]]
        cand = jnp.where(valid, jnp.concatenate(pieces, axis=0), neg)
        m0 = v1[0:1] + v2[0:1]

        def best16(cand, break_ties):
            z = jnp.zeros((1, tn), F32)
            for _ in range(K):
                m = jnp.max(cand, axis=0, keepdims=True)
                hit = cand == m
                if break_ties:
                    f = jnp.min(jnp.where(hit, flat, K * K), axis=0, keepdims=True)
                    hit = flat == f
                cand = jnp.where(hit, neg, cand)
                z = z + jnp.exp(m - m0)
            chosen = jnp.where(jnp.logical_and(valid, cand == neg), 1.0, 0.0)
            per_a = [jnp.sum(chosen[0:16], axis=0, keepdims=True)]
            per_a += [jnp.sum(chosen[8 * (a + 1):8 * (a + 2)], axis=0, keepdims=True) for a in range(1, 8)]
            return jnp.concatenate(per_a + [chosen[8 * 9:8 * 10]], axis=0), z

        cnt_s[...], z_s[...] = best16(cand, False)
        tied2 = jnp.max(jnp.abs(jnp.sum(cnt_s[...], axis=0, keepdims=True) - K)) > 0

        @pl.when(tied2)
        def _():
            cnt_s[...], z_s[...] = best16(cand, True)

        count = cnt_s[...]
        inv_z = 1.0 / z_s[...]
        e1_ref[h] = jnp.where(rank1 < K, jnp.exp(s1 - v1[0:1]) * inv_z, 0.0)
        e2_ref[h] = jnp.where(rank2 < K, jnp.exp(s2 - v2[0:1]), 0.0).astype(BF16)
        rank2_ref[h] = rank2.astype(F32).astype(BF16)
        lim = jnp.zeros((PEER_NKEYS, tn), F32)
        for a in range(K):
            lim = jnp.where(rank1 == a, count[a:a + 1], lim)
        lim_ref[h] = lim
        return carry

    lax.fori_loop(0, PEER_HEADS, head, 0)


def peer_topk(s_t):
    ntile, _, _, _, tn = s_t.shape
    n = ntile * tn
    spec = pl.BlockSpec((PEER_HEADS, PEER_NKEYS, tn), lambda i: (0, 0, i))
    shp = lambda dt: jax.ShapeDtypeStruct((PEER_HEADS, PEER_NKEYS, n), dt)
    return pl.pallas_call(
        _peer_topk_kernel,
        grid=(ntile,),
        in_specs=[pl.BlockSpec((None, 2, PEER_HEADS, PEER_NKEYS, tn), lambda i: (i, 0, 0, 0, 0))],
        out_specs=[spec, spec, spec, spec],
        out_shape=[shp(F32), shp(BF16), shp(F32), shp(BF16)],
        scratch_shapes=[pltpu.VMEM((PEER_TOPK, tn), F32), pltpu.VMEM((PEER_NKEYS, tn), jnp.int32),
                        pltpu.VMEM((PEER_TOPK, tn), F32), pltpu.VMEM((PEER_NKEYS, tn), jnp.int32),
                        pltpu.VMEM((PEER_TOPK, tn), F32), pltpu.VMEM((1, tn), F32)],
        compiler_params=_cparams(("arbitrary",)),
        name="peer_topk",
    )(s_t)


def _peer_ffn_kernel(x_ref, ht_ref, lim_ref, rank2_ref, e1_ref, e2_ref, u_ref, vt_ref, fnw_ref, o_ref,
                     acc_ref, *scratch):
    j = pl.program_id(1)
    eb = u_ref.shape[0]
    nsub = eb // PEER_NKEYS
    pre_refs, act_refs = scratch[:PEER_PRE_PIECES], scratch[PEER_PRE_PIECES:]
    piece = eb // len(pre_refs)
    per_chunk = nsub // len(act_refs)
    ck = per_chunk * PEER_NKEYS
    zero = jnp.zeros((), BF16)

    @pl.when(j == 0)
    def _():
        acc_ref[...] = jnp.zeros_like(acc_ref)

    tt = ht_ref.shape[-1]
    for k, pre_k in enumerate(pre_refs):
        for sub in range(ht_ref.shape[0]):
            pre_k[:, pl.ds(sub * tt, tt)] = jnp.dot(u_ref[pl.ds(k * piece, piece), :], ht_ref[sub],
                                                    preferred_element_type=F32)
    for c, act_c in enumerate(act_refs):
        for jc in range(per_chunk):
            jj = c * per_chunk + jc
            gate = None
            for h in range(PEER_HEADS):
                lrow = lim_ref[h, jj:jj + 1, :].astype(BF16)
                erow = e1_ref[h, jj:jj + 1, :].astype(BF16)
                term = jnp.where(rank2_ref[h] < lrow, e2_ref[h], zero) * erow
                gate = term if gate is None else gate + term
            p = pre_refs[jj * PEER_NKEYS // piece][pl.ds(jj * PEER_NKEYS % piece, PEER_NKEYS), :]
            gelu = 0.5 * p * (1.0 + lax.erf(p * (2.0 ** -0.5)))
            act_c[pl.ds(jc * PEER_NKEYS, PEER_NKEYS), :] = gelu.astype(BF16) * gate
        acc_ref[...] += jnp.dot(vt_ref[:, pl.ds(c * ck, ck)], act_c[...], preferred_element_type=F32)

    @pl.when(j == pl.num_programs(1) - 1)
    def _():
        y = x_ref[...] + acc_ref[...].T
        ms = jnp.mean(y * y, axis=-1, keepdims=True)
        o_ref[...] = y * lax.rsqrt(ms + NORM_EPS) * fnw_ref[...]


def peer_ffn(x1, h2t, lim, rank2, e1, e2, expert_u, expert_v, final_norm_w, tn=512, eb=1024):
    n, d = x1.shape
    ne = expert_u.shape[0]
    u = expert_u.astype(BF16)
    vt = expert_v.T.astype(BF16)
    nsub = eb // PEER_NKEYS
    assert nsub == 8
    tok = pl.BlockSpec((PEER_HEADS, PEER_NKEYS, tn), lambda i, j: (0, 0, i))
    row = pl.BlockSpec((PEER_HEADS, nsub, tn), lambda i, j: (0, j, i))
    return pl.pallas_call(
        _peer_ffn_kernel,
        grid=(n // tn, ne // eb),
        in_specs=[
            pl.BlockSpec((tn, d), lambda i, j: (i, 0)),
            pl.BlockSpec((tn // h2t.shape[-1], d, h2t.shape[-1]), lambda i, j: (i, 0, 0)),
            row, tok, row, tok,
            pl.BlockSpec((eb, d), lambda i, j: (j, 0)),
            pl.BlockSpec((d, eb), lambda i, j: (0, j)),
            pl.BlockSpec((1, d), lambda i, j: (0, 0)),
        ],
        out_specs=pl.BlockSpec((tn, d), lambda i, j: (i, 0)),
        out_shape=jax.ShapeDtypeStruct((n, d), F32),
        scratch_shapes=[pltpu.VMEM((d, tn), F32)]
        + [pltpu.VMEM((eb // PEER_PRE_PIECES, tn), F32)] * PEER_PRE_PIECES
        + [pltpu.VMEM((eb // PEER_ACT_CHUNKS, tn), BF16)] * PEER_ACT_CHUNKS,
        compiler_params=_cparams(("arbitrary", "arbitrary")),
        name="peer_ffn",
    )(x1, h2t, lim, rank2, e1, e2, u, vt, final_norm_w.reshape(1, d))


def kernel(x, norm1_w, w_in, ret_gn_w, ret_gn_b, rwkv_mix, rwkv_w0, rwkv_w_up, rwkv_a0, rwkv_a_up, rwkv_g_up, rwkv_k_k, rwkv_k_a, rwkv_r_k, rwkv_ln_w, rwkv_ln_b, w_ret_branch, w_rwkv_branch, w_out, norm2_w, peer_w_q, peer_keys_1, peer_keys_2, peer_u, peer_v, final_norm_w):
    b, t, d = x.shape
    n = b * t
    assert w_in.shape[0] == 1, "single-layer block: the final norm is fused into the PEER kernel"
    l = 0
    xf = x.reshape(n, d)
    zp = in_proj(xf, norm1_w[l], w_in[l].astype(BF16))
    o_ret = retention(zp, ret_gn_w[l], ret_gn_b[l], b, t)
    o_rwkv = rwkv_time_mix(zp, rwkv_mix[l], rwkv_w0[l], rwkv_w_up[l], rwkv_a0[l], rwkv_a_up[l], rwkv_g_up[l],
                           rwkv_k_k[l], rwkv_k_a[l], rwkv_r_k[l], rwkv_ln_w[l], rwkv_ln_b[l], b, t)
    x1, s_t, h2t = merge_query(xf, zp, o_ret, o_rwkv, w_ret_branch[l], w_rwkv_branch[l], w_out[l],
                               norm2_w[l], peer_w_q[l], peer_keys_1[l], peer_keys_2[l])
    lim, rank2, e1, e2 = peer_topk(s_t)
    out = peer_ffn(x1, h2t, lim, rank2, e1, e2, peer_u[l], peer_v[l], final_norm_w)
    return out.reshape(b, t, d)
```

```python
import functools

import numpy as np
import jax
import jax.numpy as jnp
from jax import lax
from jax.experimental import pallas as pl
from jax.experimental.pallas import tpu as pltpu

F32 = jnp.float32
BF16 = jnp.bfloat16

D_MODEL = 2048
CHUNK = 64
NORM_EPS = 1e-6
RET_HEADS = 4
RET_DK = 256
RET_DV = 256
RET_QK = RET_HEADS * RET_DK
RET_WIDTH = RET_HEADS * RET_DV
ROPE_BASE = 10000.0
RWKV_HEAD = 64
RWKV_WIDTH = D_MODEL // 2
RWKV_HEADS = RWKV_WIDTH // RWKV_HEAD
DECAY_LORA = 64
AAA_LORA = 64
GATE_LORA = 128
LORA_COLS = DECAY_LORA + AAA_LORA + GATE_LORA
RWKV_COLS = 3 * RWKV_WIDTH + LORA_COLS
RWKV_GN_EPS = 64e-5
IN_COLS = 2 * RET_QK + 2 * RET_WIDTH + RWKV_COLS + 2 * D_MODEL

ZP_RWKV = 2 * RET_QK + 2 * RET_WIDTH
ZP_LORA = ZP_RWKV + 3 * RWKV_WIDTH
ZP_GATE_RET = ZP_RWKV + RWKV_COLS
ZP_GATE_RWKV = ZP_GATE_RET + D_MODEL
assert ZP_LORA % LORA_COLS == 0

VMEM_LIMIT = 56 * 1024 * 1024


def _cparams(sem):
    return pltpu.CompilerParams(dimension_semantics=sem, vmem_limit_bytes=VMEM_LIMIT)


def _in_proj_kernel(x_ref, nw_ref, w_ref, o_ref, h_ref):
    @pl.when(pl.program_id(1) == 0)
    def _():
        x = x_ref[...]
        ms = jnp.mean(x * x, axis=-1, keepdims=True)
        h_ref[...] = (x * lax.rsqrt(ms + NORM_EPS) * nw_ref[...]).astype(BF16)

    o_ref[...] = jnp.dot(h_ref[...], w_ref[...], preferred_element_type=F32)


def _cast_kernel(x_ref, o_ref):
    o_ref[...] = x_ref[...].astype(BF16)


def cast_bf16(x, rows=256):
    n, cols = x.shape
    return pl.pallas_call(
        _cast_kernel,
        grid=(n // rows,),
        in_specs=[pl.BlockSpec((rows, cols), lambda i: (i, 0))],
        out_specs=pl.BlockSpec((rows, cols), lambda i: (i, 0)),
        out_shape=jax.ShapeDtypeStruct((n, cols), BF16),
        compiler_params=_cparams(("arbitrary",)),
        name="cast_bf16",
    )(x)


def in_proj(x, norm_w, w_bf16, tm=1024, tn=1280):
    n, d = x.shape
    cols = w_bf16.shape[1]
    return pl.pallas_call(
        _in_proj_kernel,
        grid=(n // tm, cols // tn),
        in_specs=[
            pl.BlockSpec((tm, d), lambda i, j: (i, 0)),
            pl.BlockSpec((1, d), lambda i, j: (0, 0)),
            pl.BlockSpec((d, tn), lambda i, j: (0, j)),
        ],
        out_specs=pl.BlockSpec((tm, tn), lambda i, j: (i, j)),
        out_shape=jax.ShapeDtypeStruct((n, cols), F32),
        scratch_shapes=[pltpu.VMEM((tm, d), BF16)],
        compiler_params=_cparams(("arbitrary", "arbitrary")),
        name="in_proj",
    )(x, norm_w.reshape(1, d), w_bf16)


def _retention_kernel(q_ref, k_ref, v_ref, g_ref, cos_ref, sin_ref, dintra_ref, qdec_ref, kdec_ref,
                      cdec_ref, gnw_ref, gnb_ref, o_ref, s_ref, qr_ref, kr_ref, acc_ref):
    tb = q_ref.shape[0]
    half = RET_DK // 2

    @pl.when(pl.program_id(1) == 0)
    def _():
        s_ref[...] = jnp.zeros_like(s_ref)

    cos = cos_ref[...]
    sin = sin_ref[...]

    def rot(z_ref, hd):
        z1 = z_ref[:, hd * RET_DK:hd * RET_DK + half]
        z2 = z_ref[:, hd * RET_DK + half:(hd + 1) * RET_DK]
        return jnp.concatenate([z1 * cos - z2 * sin, z2 * cos + z1 * sin], axis=-1)

    for hd in range(RET_HEADS):
        cols = pl.ds(hd * RET_DK, RET_DK)
        qr_ref[:, cols] = rot(q_ref, hd)
        kr_ref[:, cols] = rot(k_ref, hd) * (RET_DK ** -0.5)
    for c in range(tb // CHUNK):
        sl = pl.ds(c * CHUNK, CHUNK)
        for hd in range(RET_HEADS):
            cols = pl.ds(hd * RET_DK, RET_DK)
            qc = qr_ref[sl, cols]
            kc = kr_ref[sl, cols]
            vc = v_ref[sl, cols].astype(BF16)
            scores = lax.dot_general(qc.astype(BF16), kc.astype(BF16), (((1,), (1,)), ((), ())),
                                     preferred_element_type=F32) * dintra_ref[hd]
            s_prev = s_ref[hd]
            o = jnp.dot(scores.astype(BF16), vc, preferred_element_type=F32)
            o = o + jnp.dot((qc * qdec_ref[hd]).astype(BF16), s_prev.astype(BF16), preferred_element_type=F32)
            kd_t = (kc * kdec_ref[hd]).T.astype(BF16)
            s_ref[hd] = s_prev * cdec_ref[hd] + jnp.dot(kd_t, vc, preferred_element_type=F32)
            acc_ref[sl, cols] = o
    for hd in range(RET_HEADS):
        cols = pl.ds(hd * RET_DK, RET_DK)
        o = acc_ref[:, cols]
        mu = jnp.mean(o, axis=-1, keepdims=True)
        oc = o - mu
        var = jnp.mean(oc * oc, axis=-1, keepdims=True)
        y = oc * lax.rsqrt(var + NORM_EPS) * gnw_ref[:, cols] + gnb_ref[:, cols]
        g = g_ref[:, cols]
        o_ref[:, cols] = y * (g * jax.nn.sigmoid(g))


def _retention_tables(t):
    h = RET_HEADS
    log_g = jnp.log(1.0 - jnp.exp2(-5.0 - jnp.arange(h, dtype=F32)))
    n = jnp.arange(CHUNK, dtype=F32)
    d_intra = jnp.exp(log_g[:, None, None] * jnp.abs(n[:, None] - n[None, :]))
    q_decay = jnp.exp(log_g[:, None] * (n[None, :] + 1.0))
    k_decay = jnp.exp(log_g[:, None] * (CHUNK - 1.0 - n[None, :]))
    chunk_decay = jnp.exp(log_g * CHUNK)
    q_decay = jnp.broadcast_to(q_decay[:, :, None], (h, CHUNK, RET_DK))
    k_decay = jnp.broadcast_to(k_decay[:, :, None], (h, CHUNK, RET_DK))
    chunk_decay = jnp.broadcast_to(chunk_decay[:, None, None], (h, 1, RET_DV))
    half = RET_DK // 2
    inv_freq = ROPE_BASE ** (-jnp.arange(half, dtype=F32) * 2.0 / RET_DK)
    ang = jnp.arange(t, dtype=jnp.int32).astype(F32)[:, None] * inv_freq[None, :]
    return d_intra, q_decay, k_decay, chunk_decay, jnp.cos(ang), jnp.sin(ang)


def retention(zp, gn_w, gn_b, batch, t, tb=512):
    n = zp.shape[0]
    h = RET_HEADS
    nt = t // tb
    d_intra, q_decay, k_decay, chunk_decay, cos, sin = _retention_tables(t)

    def zspec(col_block):
        return pl.BlockSpec((tb, RET_WIDTH), lambda b, i: (b * nt + i, col_block))

    tab = lambda shape: pl.BlockSpec((h,) + shape, lambda b, i: (0, 0, 0))
    return pl.pallas_call(
        _retention_kernel,
        grid=(batch, nt),
        in_specs=[
            zspec(0), zspec(1), zspec(2), zspec(3),
            pl.BlockSpec((tb, RET_DK // 2), lambda b, i: (i, 0)),
            pl.BlockSpec((tb, RET_DK // 2), lambda b, i: (i, 0)),
            tab((CHUNK, CHUNK)), tab((CHUNK, RET_DK)), tab((CHUNK, RET_DK)), tab((1, RET_DV)),
            pl.BlockSpec((1, RET_WIDTH), lambda b, i: (0, 0)),
            pl.BlockSpec((1, RET_WIDTH), lambda b, i: (0, 0)),
        ],
        out_specs=pl.BlockSpec((tb, RET_WIDTH), lambda b, i: (b * nt + i, 0)),
        out_shape=jax.ShapeDtypeStruct((n, RET_WIDTH), F32),
        scratch_shapes=[
            pltpu.VMEM((h, RET_DK, RET_DV), F32),
            pltpu.VMEM((tb, RET_QK), F32),
            pltpu.VMEM((tb, RET_QK), F32),
            pltpu.VMEM((tb, RET_WIDTH), F32),
        ],
        compiler_params=_cparams(("arbitrary", "arbitrary")),
        name="retention",
    )(zp, zp, zp, zp, cos, sin, d_intra, q_decay, k_decay, chunk_decay,
      gn_w.reshape(1, RET_WIDTH), gn_b.reshape(1, RET_WIDTH))


def _split2(x):
    hi = x.astype(BF16)
    lo = (x - hi.astype(F32)).astype(BF16)
    return hi, lo


def _split3(x):
    hi = x.astype(BF16)
    r = x - hi.astype(F32)
    mid = r.astype(BF16)
    lo = (r - mid.astype(F32)).astype(BF16)
    return hi, mid, lo


def _group_sum(x, gmat):
    hi, lo = _split2(x)
    lanes = gmat.shape[0]
    outs = []
    for b in range(x.shape[1] // lanes):
        sl = slice(b * lanes, (b + 1) * lanes)
        outs.append(jnp.dot(hi[:, sl], gmat, preferred_element_type=F32)
                    + jnp.dot(lo[:, sl], gmat, preferred_element_type=F32))
    return jnp.concatenate(outs, axis=-1)


def _bmm(a, b):
    return lax.dot_general(a.astype(BF16), b.astype(BF16), (((2,), (1,)), ((0,), (0,))),
                           preferred_element_type=F32)


def _bmm_nt(a, b):
    return lax.dot_general(a.astype(BF16), b.astype(BF16), (((2,), (2,)), ((0,), (0,))),
                           preferred_element_type=F32)


def _bmm_tn(a, b):
    return lax.dot_general(a.astype(BF16), b.astype(BF16), (((1,), (1,)), ((0,), (0,))),
                           preferred_element_type=F32)


RWKV_HEAD_GROUP = 16
GROUP_LANES = 128
RWKV_TILE = 128


def _rwkv_kernel(r_ref, k_ref, v_ref, l_ref, pr_ref, pk_ref, pv_ref, plr_ref,
                     mixr_ref, mixk_ref, mixv_ref, mixl_ref, w0_ref, a0_ref, kk_ref, ka_ref, rk_ref,
                     wup_ref, aup_ref, gup_ref, gmat_ref, ltri_ref, lnw_ref, lnb_ref,
                     o_ref,
                     at_s, rt_s, bh_s, kh_s, bt_s, kt_s, v_s, gc_s, g_s, bv_s, s_ref, ys_ref, *, tiles_per_batch):
    tb = r_ref.shape[0]
    nchunk = tb // CHUNK
    first = (pl.program_id(0) % tiles_per_batch) == 0
    rowid = lax.broadcasted_iota(jnp.int32, (tb, 1), 0)

    @pl.when(first)
    def _():
        s_ref[...] = jnp.zeros_like(s_ref)

    def shift_lerp(cur_ref, prev_ref, mix_ref):
        cur = cur_ref[...]
        prev_row = jnp.where(first, 0.0, prev_ref[7:8, :])
        prev = jnp.where(rowid == 0, prev_row, pltpu.roll(cur, 1, axis=0))
        return cur + (prev - cur) * mix_ref[...]

    r = shift_lerp(r_ref, pr_ref, mixr_ref)
    k = shift_lerp(k_ref, pk_ref, mixk_ref)
    v = shift_lerp(v_ref, pv_ref, mixv_ref)
    lo = shift_lerp(l_ref, plr_ref, mixl_ref)
    xw = lo[:, :DECAY_LORA]
    xa = lo[:, DECAY_LORA:DECAY_LORA + AAA_LORA]
    xg = lo[:, DECAY_LORA + AAA_LORA:]
    wl = jnp.dot(jnp.tanh(xw).astype(BF16), wup_ref[...], preferred_element_type=F32)
    u = -(w0_ref[...] + wl)
    softplus = jnp.maximum(u, 0.0) + jnp.log(1.0 + jnp.exp(-jnp.abs(u)))
    logw = -jnp.exp(-softplus - 0.5)
    a = jax.nn.sigmoid(a0_ref[...] + jnp.dot(xa.astype(BF16), aup_ref[...], preferred_element_type=F32))
    g_s[...] = jnp.dot(jax.nn.sigmoid(xg).astype(BF16), gup_ref[...], preferred_element_type=F32)
    gmat = gmat_ref[...]
    kk = k * kk_ref[...]
    kk = kk / jnp.maximum(jnp.sqrt(_group_sum(kk * kk, gmat)), 1e-12)
    k2 = k * (1.0 + (a - 1.0) * ka_ref[...])
    bv_s[...] = _group_sum(r * k2 * rk_ref[...], gmat) * v
    beta = kk * a
    ltri = ltri_ref[...]

    def to_heads(dst, c, val):
        for h in range(RWKV_HEADS):
            dst[h, pl.ds(c * CHUNK, CHUNK), :] = val[:, h * RWKV_HEAD:(h + 1) * RWKV_HEAD]

    for c in range(nchunk):
        sl = slice(c * CHUNK, (c + 1) * CHUNK)
        lw = logw[sl]
        h1, h2, h3 = _split3(lw)
        cs = (jnp.dot(ltri, h1, preferred_element_type=F32) + jnp.dot(ltri, h2, preferred_element_type=F32)
              + jnp.dot(ltri, h3, preferred_element_type=F32))
        cs_last = cs[CHUNK - 1:CHUNK, :]
        g_t = jnp.exp(cs)
        g_tm1 = jnp.exp(cs - lw)
        inv_g = jnp.exp(-cs)
        g_end = jnp.exp(cs_last - cs)
        to_heads(at_s, c, -kk[sl] * g_tm1)
        to_heads(rt_s, c, r[sl] * g_t)
        to_heads(bh_s, c, beta[sl] * inv_g)
        to_heads(kh_s, c, k2[sl] * inv_g)
        to_heads(bt_s, c, beta[sl] * g_end)
        to_heads(kt_s, c, k2[sl] * g_end)
        to_heads(v_s, c, v[sl])
        gc = jnp.exp(cs_last)
        for h in range(RWKV_HEADS):
            gc_s[h, pl.ds(c * 8, 8), :] = jnp.broadcast_to(gc[:, h * RWKV_HEAD:(h + 1) * RWKV_HEAD], (8, RWKV_HEAD))

    G = RWKV_HEAD_GROUP
    ti = lax.broadcasted_iota(jnp.int32, (CHUNK, CHUNK), 0)
    si = lax.broadcasted_iota(jnp.int32, (CHUNK, CHUNK), 1)
    strict = (si < ti)[None]
    incl = (si <= ti)[None]
    eye = (si == ti).astype(F32)[None]

    def body(idx, carry):
        c = idx // (RWKV_HEADS // G)
        hg = idx % (RWKV_HEADS // G)
        hs = pl.ds(pl.multiple_of(hg * G, G), G)
        ts = pl.ds(pl.multiple_of(c * CHUNK, CHUNK), CHUNK)
        at = at_s[hs, ts, :]
        rt = rt_s[hs, ts, :]
        bh = bh_s[hs, ts, :]
        kh = kh_s[hs, ts, :]
        bt = bt_s[hs, ts, :]
        kt = kt_s[hs, ts, :]
        vv = v_s[hs, ts, :]
        gc = gc_s[hs, pl.ds(pl.multiple_of(c * 8, 8), 1), :]
        aab = jnp.where(strict, _bmm_nt(at, bh), 0.0)
        aak = jnp.where(strict, _bmm_nt(at, kh), 0.0)
        aqb = jnp.where(incl, _bmm_nt(rt, bh), 0.0)
        aqk = jnp.where(incl, _bmm_nt(rt, kh), 0.0)
        p = aab
        tm = eye + aab
        for _ in range(5):
            p = _bmm(p, p)
            tm = tm + _bmm(p, tm)
        akv = _bmm(aak, vv)
        wt = _bmm(tm, at)
        u0 = _bmm(tm, akv)
        qt = rt + _bmm(aqb, wt)
        y1 = _bmm(aqb, u0) + _bmm(aqk, vv)
        m = _bmm_tn(bt, wt) + eye * gc
        nn = _bmm_tn(bt, u0) + _bmm_tn(kt, vv)
        s = s_ref[hs]
        ys_ref[hs, ts, :] = _bmm(qt, s) + y1
        s_ref[hs] = _bmm(m, s) + nn
        return carry

    lax.fori_loop(0, nchunk * (RWKV_HEADS // G), body, 0)

    y = jnp.concatenate([ys_ref[h] for h in range(RWKV_HEADS)], axis=-1)
    mu = _group_sum(y, gmat) * (1.0 / RWKV_HEAD)
    yc = y - mu
    var = _group_sum(yc * yc, gmat) * (1.0 / RWKV_HEAD)
    yn = yc * lax.rsqrt(var + RWKV_GN_EPS) * lnw_ref[...] + lnb_ref[...]
    o_ref[...] = (yn + bv_s[...]) * g_s[...]


def _rwkv_consts():
    lane = np.arange(GROUP_LANES) // RWKV_HEAD
    gmat = (lane[:, None] == lane[None, :]).astype(np.float32)
    ltri = np.tril(np.ones((CHUNK, CHUNK), np.float32))
    return jnp.asarray(gmat, BF16), jnp.asarray(ltri, BF16)


def rwkv_time_mix(zp, mix, w0, w_up, a0, a_up, g_up, k_k, k_a, r_k, ln_w, ln_b, batch, t, tb=RWKV_TILE):
    n = zp.shape[0]
    W = RWKV_WIDTH
    nt = t // tb
    gmat, ltri = _rwkv_consts()
    row = lambda p: p.reshape(1, -1)
    mix_r, mix_k, mix_v, mix_l = mix[:W], mix[W:2 * W], mix[2 * W:3 * W], mix[3 * W:]
    cb = ZP_RWKV // W
    cur = lambda j: pl.BlockSpec((tb, W), lambda i: (i, cb + j))
    prev = lambda j: pl.BlockSpec((8, W), lambda i: (jnp.maximum(i * (tb // 8) - 1, 0), cb + j))
    full = lambda shape: pl.BlockSpec(shape, lambda i: (0,) * len(shape))
    hscr = pltpu.VMEM((RWKV_HEADS, tb, RWKV_HEAD), F32)
    return pl.pallas_call(
        functools.partial(_rwkv_kernel, tiles_per_batch=nt),
        grid=(n // tb,),
        in_specs=[
            cur(0), cur(1), cur(2),
            pl.BlockSpec((tb, LORA_COLS), lambda i: (i, ZP_LORA // LORA_COLS)),
            prev(0), prev(1), prev(2),
            pl.BlockSpec((8, LORA_COLS), lambda i: (jnp.maximum(i * (tb // 8) - 1, 0), ZP_LORA // LORA_COLS)),
            full((1, W)), full((1, W)), full((1, W)), full((1, LORA_COLS)),
            full((1, W)), full((1, W)), full((1, W)), full((1, W)), full((1, W)),
            full((DECAY_LORA, W)), full((AAA_LORA, W)), full((GATE_LORA, W)),
            full((GROUP_LANES, GROUP_LANES)), full((CHUNK, CHUNK)), full((1, W)), full((1, W)),
        ],
        out_specs=pl.BlockSpec((tb, W), lambda i: (i, 0)),
        out_shape=jax.ShapeDtypeStruct((n, W), F32),
        scratch_shapes=[hscr] * 7 + [pltpu.VMEM((RWKV_HEADS, 8 * (tb // CHUNK), RWKV_HEAD), F32),
                                     pltpu.VMEM((tb, W), F32), pltpu.VMEM((tb, W), F32),
                                     pltpu.VMEM((RWKV_HEADS, RWKV_HEAD, RWKV_HEAD), F32), hscr],
        compiler_params=_cparams(("arbitrary",)),
        name="rwkv_time_mix",
    )(zp, zp, zp, zp, zp, zp, zp, zp,
      row(mix_r), row(mix_k), row(mix_v), row(mix_l), row(w0), row(a0), row(k_k), row(k_a), row(r_k),
      w_up.astype(BF16), a_up.astype(BF16), g_up.astype(BF16), gmat, ltri, row(ln_w), row(ln_b))


PEER_HEADS = 8
PEER_NKEYS = 128
PEER_NEXPERTS = PEER_NKEYS * PEER_NKEYS
PEER_DQ = 256
PEER_TOPK = 16
PEER_ACT_CHUNKS = 4
PEER_PRE_PIECES = 2
PEER_TOKEN_TILE = 256


def _merge_query_kernel(x_ref, gr_ref, gw_ref, oret_ref, orwkv_ref, wr_ref, ww_ref, wo_ref, nw_ref, wq_ref,
                        k1_ref, k2_ref, x1_ref, s_ref, ht_ref):
    pr = jnp.dot(oret_ref[...].astype(BF16), wr_ref[...], preferred_element_type=F32)
    pw = jnp.dot(orwkv_ref[...].astype(BF16), ww_ref[...], preferred_element_type=F32)
    merged = jax.nn.sigmoid(gr_ref[...]) * pr + jax.nn.sigmoid(gw_ref[...]) * pw
    x1 = x_ref[...] + jnp.dot(merged.astype(BF16), wo_ref[...], preferred_element_type=F32)
    x1_ref[...] = x1
    ms = jnp.mean(x1 * x1, axis=-1, keepdims=True)
    h2 = x1 * lax.rsqrt(ms + NORM_EPS) * nw_ref[...]
    ht_ref[...] = h2.T.astype(BF16)
    q = jnp.dot(h2.astype(BF16), wq_ref[...], preferred_element_type=F32).astype(BF16)
    half = PEER_DQ // 2
    for h in range(PEER_HEADS):
        for p, kref in enumerate((k1_ref, k2_ref)):
            qh = q[:, h * PEER_DQ + p * half: h * PEER_DQ + (p + 1) * half]
            s_ref[p, h] = lax.dot_general(kref[...], qh, (((1,), (1,)), ((), ())), preferred_element_type=F32)


def merge_query(x, zp, o_ret, o_rwkv, w_ret, w_rwkv, w_out, norm_w, w_q, keys_1, keys_2, tm=PEER_TOKEN_TILE):
    n, d = x.shape
    const = lambda shape: pl.BlockSpec(shape, lambda i: (0, 0), pipeline_mode=pl.Buffered(1))

    def gate(col0):
        return pl.BlockSpec((pl.Element(tm), pl.Element(d)), lambda i: (i * tm, col0))

    return pl.pallas_call(
        _merge_query_kernel,
        grid=(n // tm,),
        in_specs=[pl.BlockSpec((tm, d), lambda i: (i, 0)), gate(ZP_GATE_RET), gate(ZP_GATE_RWKV),
                  pl.BlockSpec((tm, RET_WIDTH), lambda i: (i, 0)),
                  pl.BlockSpec((tm, RWKV_WIDTH), lambda i: (i, 0)),
                  const((RET_WIDTH, d)), const((RWKV_WIDTH, d)), const((d, d)),
                  const((1, d)), const((d, PEER_HEADS * PEER_DQ)),
                  const((PEER_NKEYS, PEER_DQ // 2)), const((PEER_NKEYS, PEER_DQ // 2))],
        out_specs=[
            pl.BlockSpec((tm, d), lambda i: (i, 0)),
            pl.BlockSpec((None, 2, PEER_HEADS, PEER_NKEYS, tm), lambda i: (i, 0, 0, 0, 0)),
            pl.BlockSpec((None, d, tm), lambda i: (i, 0, 0)),
        ],
        out_shape=[
            jax.ShapeDtypeStruct((n, d), F32),
            jax.ShapeDtypeStruct((n // tm, 2, PEER_HEADS, PEER_NKEYS, tm), F32),
            jax.ShapeDtypeStruct((n // tm, d, tm), BF16),
        ],
        compiler_params=_cparams(("arbitrary",)),
        name="merge_query",
    )(x, zp, zp, o_ret, o_rwkv, w_ret.astype(BF16), w_rwkv.astype(BF16), w_out.astype(BF16),
      norm_w.reshape(1, d), w_q.astype(BF16), keys_1.astype(BF16), keys_2.astype(BF16))


_CAND_GROUPS = [(0, 0), (0, 8)] + [(a, 0) for a in range(1, 8)]


def _peer_topk_kernel(s_ref, lim_ref, rank2_ref, e1_ref, e2_ref, v1_s, r1_s, v2_s, r2_s, cnt_s, z_s):
    tn = s_ref.shape[-1]
    K = PEER_TOPK
    neg = -jnp.inf
    rowid = lax.broadcasted_iota(jnp.int32, (PEER_NKEYS, tn), 0)
    nrow = 8 * (len(_CAND_GROUPS) + 1)
    r = lax.broadcasted_iota(jnp.int32, (nrow, tn), 0)
    grp, sub = r // 8, r % 8
    ca = jnp.where(grp < 2, 0, jnp.where(grp < 9, grp - 1, 8 + sub))
    cb = jnp.where(grp == 1, 8 + sub, jnp.where(grp < 9, sub, 0))
    flat = ca * K + cb
    valid = (ca + 1) * (cb + 1) <= K

    def top16(s, break_ties):
        rank = jnp.full((PEER_NKEYS, tn), K, jnp.int32)
        vals = []
        for a in range(K):
            m = jnp.max(s, axis=0, keepdims=True)
            hit = s == m
            if break_ties:
                idx = jnp.min(jnp.where(hit, rowid, PEER_NKEYS), axis=0, keepdims=True)
                hit = rowid == idx
            rank = jnp.where(hit, a, rank)
            s = jnp.where(hit, neg, s)
            vals.append(m)
        return jnp.concatenate(vals, axis=0), rank

    def head(h, carry):
        s1 = s_ref[0, h]
        s2 = s_ref[1, h]
        v1_s[...], r1_s[...] = top16(s1, False)
        v2_s[...], r2_s[...] = top16(s2, False)
        ranked = (jnp.sum((r1_s[...] < K).astype(jnp.int32), axis=0, keepdims=True)
                  + jnp.sum((r2_s[...] < K).astype(jnp.int32), axis=0, keepdims=True))
        tied = jnp.max(jnp.abs(ranked - 2 * K)) > 0

        @pl.when(tied)
        def _():
            v1_s[...], r1_s[...] = top16(s1, True)
            v2_s[...], r2_s[...] = top16(s2, True)

        v1, rank1 = v1_s[...], r1_s[...]
        v2, rank2 = v2_s[...], r2_s[...]
        pieces = [v1[a:a + 1] + v2[b0:b0 + 8] for a, b0 in _CAND_GROUPS] + [v1[8:16] + v2[0:1]]
        cand = jnp.where(valid, jnp.concatenate(pieces, axis=0), neg)
        m0 = v1[0:1] + v2[0:1]

        def best16(cand, break_ties):
            z = jnp.zeros((1, tn), F32)
            for _ in range(K):
                m = jnp.max(cand, axis=0, keepdims=True)
                hit = cand == m
                if break_ties:
                    f = jnp.min(jnp.where(hit, flat, K * K), axis=0, keepdims=True)
                    hit = flat == f
                cand = jnp.where(hit, neg, cand)
                z = z + jnp.exp(m - m0)
            chosen = jnp.where(jnp.logical_and(valid, cand == neg), 1.0, 0.0)
            per_a = [jnp.sum(chosen[0:16], axis=0, keepdims=True)]
            per_a += [jnp.sum(chosen[8 * (a + 1):8 * (a + 2)], axis=0, keepdims=True) for a in range(1, 8)]
            return jnp.concatenate(per_a + [chosen[8 * 9:8 * 10]], axis=0), z

        cnt_s[...], z_s[...] = best16(cand, False)
        tied2 = jnp.max(jnp.abs(jnp.sum(cnt_s[...], axis=0, keepdims=True) - K)) > 0

        @pl.when(tied2)
        def _():
            cnt_s[...], z_s[...] = best16(cand, True)

        count = cnt_s[...]
        inv_z = 1.0 / z_s[...]
        e1_ref[h] = jnp.where(rank1 < K, jnp.exp(s1 - v1[0:1]) * inv_z, 0.0)
        e2_ref[h] = jnp.where(rank2 < K, jnp.exp(s2 - v2[0:1]), 0.0).astype(BF16)
        rank2_ref[h] = rank2.astype(F32).astype(BF16)
        lim = jnp.zeros((PEER_NKEYS, tn), F32)
        for a in range(K):
            lim = jnp.where(rank1 == a, count[a:a + 1], lim)
        lim_ref[h] = lim
        return carry

    lax.fori_loop(0, PEER_HEADS, head, 0)


def peer_topk(s_t):
    ntile, _, _, _, tn = s_t.shape
    n = ntile * tn
    spec = pl.BlockSpec((PEER_HEADS, PEER_NKEYS, tn), lambda i: (0, 0, i))
    shp = lambda dt: jax.ShapeDtypeStruct((PEER_HEADS, PEER_NKEYS, n), dt)
    return pl.pallas_call(
        _peer_topk_kernel,
        grid=(ntile,),
        in_specs=[pl.BlockSpec((None, 2, PEER_HEADS, PEER_NKEYS, tn), lambda i: (i, 0, 0, 0, 0))],
        out_specs=[spec, spec, spec, spec],
        out_shape=[shp(F32), shp(BF16), shp(F32), shp(BF16)],
        scratch_shapes=[pltpu.VMEM((PEER_TOPK, tn), F32), pltpu.VMEM((PEER_NKEYS, tn), jnp.int32),
                        pltpu.VMEM((PEER_TOPK, tn), F32), pltpu.VMEM((PEER_NKEYS, tn), jnp.int32),
                        pltpu.VMEM((PEER_TOPK, tn), F32), pltpu.VMEM((1, tn), F32)],
        compiler_params=_cparams(("arbitrary",)),
        name="peer_topk",
    )(s_t)


def _peer_ffn_kernel(x_ref, ht_ref, lim_ref, rank2_ref, e1_ref, e2_ref, u_ref, vt_ref, fnw_ref, o_ref,
                     acc_ref, *scratch):
    j = pl.program_id(1)
    eb = u_ref.shape[0]
    nsub = eb // PEER_NKEYS
    pre_refs, act_refs = scratch[:PEER_PRE_PIECES], scratch[PEER_PRE_PIECES:]
    piece = eb // len(pre_refs)
    per_chunk = nsub // len(act_refs)
    ck = per_chunk * PEER_NKEYS
    zero = jnp.zeros((), BF16)

    @pl.when(j == 0)
    def _():
        acc_ref[...] = jnp.zeros_like(acc_ref)

    tt = ht_ref.shape[-1]
    for k, pre_k in enumerate(pre_refs):
        for sub in range(ht_ref.shape[0]):
            pre_k[:, pl.ds(sub * tt, tt)] = jnp.dot(u_ref[pl.ds(k * piece, piece), :], ht_ref[sub],
                                                    preferred_element_type=F32)
    for c, act_c in enumerate(act_refs):
        for jc in range(per_chunk):
            jj = c * per_chunk + jc
            gate = None
            for h in range(PEER_HEADS):
                lrow = lim_ref[h, jj:jj + 1, :].astype(BF16)
                erow = e1_ref[h, jj:jj + 1, :].astype(BF16)
                term = jnp.where(rank2_ref[h] < lrow, e2_ref[h], zero) * erow
                gate = term if gate is None else gate + term
            p = pre_refs[jj * PEER_NKEYS // piece][pl.ds(jj * PEER_NKEYS % piece, PEER_NKEYS), :]
            gelu = 0.5 * p * (1.0 + lax.erf(p * (2.0 ** -0.5)))
            act_c[pl.ds(jc * PEER_NKEYS, PEER_NKEYS), :] = gelu.astype(BF16) * gate
        acc_ref[...] += jnp.dot(vt_ref[:, pl.ds(c * ck, ck)], act_c[...], preferred_element_type=F32)

    @pl.when(j == pl.num_programs(1) - 1)
    def _():
        y = x_ref[...] + acc_ref[...].T
        ms = jnp.mean(y * y, axis=-1, keepdims=True)
        o_ref[...] = y * lax.rsqrt(ms + NORM_EPS) * fnw_ref[...]


def _transpose_cast_kernel(v_ref, o_ref):
    o_ref[...] = v_ref[...].T.astype(BF16)


def transpose_cast(v, rows=512):
    ne, d = v.shape
    return pl.pallas_call(
        _transpose_cast_kernel,
        grid=(ne // rows,),
        in_specs=[pl.BlockSpec((rows, d), lambda i: (i, 0))],
        out_specs=pl.BlockSpec((d, rows), lambda i: (0, i)),
        out_shape=jax.ShapeDtypeStruct((d, ne), BF16),
        compiler_params=_cparams(("arbitrary",)),
        name="transpose_cast",
    )(v)


def peer_ffn(x1, h2t, lim, rank2, e1, e2, expert_u, expert_v, final_norm_w, tn=512, eb=1024):
    n, d = x1.shape
    ne = expert_u.shape[0]
    u = expert_u.astype(BF16)
    vt = transpose_cast(expert_v)
    nsub = eb // PEER_NKEYS
    assert nsub == 8
    tok = pl.BlockSpec((PEER_HEADS, PEER_NKEYS, tn), lambda i, j: (0, 0, i))
    row = pl.BlockSpec((PEER_HEADS, nsub, tn), lambda i, j: (0, j, i))
    return pl.pallas_call(
        _peer_ffn_kernel,
        grid=(n // tn, ne // eb),
        in_specs=[
            pl.BlockSpec((tn, d), lambda i, j: (i, 0)),
            pl.BlockSpec((tn // h2t.shape[-1], d, h2t.shape[-1]), lambda i, j: (i, 0, 0)),
            row, tok, row, tok,
            pl.BlockSpec((eb, d), lambda i, j: (j, 0)),
            pl.BlockSpec((d, eb), lambda i, j: (0, j)),
            pl.BlockSpec((1, d), lambda i, j: (0, 0)),
        ],
        out_specs=pl.BlockSpec((tn, d), lambda i, j: (i, 0)),
        out_shape=jax.ShapeDtypeStruct((n, d), F32),
        scratch_shapes=[pltpu.VMEM((d, tn), F32)]
        + [pltpu.VMEM((eb // PEER_PRE_PIECES, tn), F32)] * PEER_PRE_PIECES
        + [pltpu.VMEM((eb // PEER_ACT_CHUNKS, tn), BF16)] * PEER_ACT_CHUNKS,
        compiler_params=_cparams(("arbitrary", "arbitrary")),
        name="peer_ffn",
    )(x1, h2t, lim, rank2, e1, e2, u, vt, final_norm_w.reshape(1, d))


def kernel(x, norm1_w, w_in, ret_gn_w, ret_gn_b, rwkv_mix, rwkv_w0, rwkv_w_up, rwkv_a0, rwkv_a_up, rwkv_g_up, rwkv_k_k, rwkv_k_a, rwkv_r_k, rwkv_ln_w, rwkv_ln_b, w_ret_branch, w_rwkv_branch, w_out, norm2_w, peer_w_q, peer_keys_1, peer_keys_2, peer_u, peer_v, final_norm_w):
    b, t, d = x.shape
    n = b * t
    assert w_in.shape[0] == 1, "single-layer block: the final norm is fused into the PEER kernel"
    l = 0
    xf = x.reshape(n, d)
    zp = in_proj(xf, norm1_w[l], cast_bf16(w_in[l]))
    o_ret = retention(zp, ret_gn_w[l], ret_gn_b[l], b, t)
    o_rwkv = rwkv_time_mix(zp, rwkv_mix[l], rwkv_w0[l], rwkv_w_up[l], rwkv_a0[l], rwkv_a_up[l], rwkv_g_up[l],
                           rwkv_k_k[l], rwkv_k_a[l], rwkv_r_k[l], rwkv_ln_w[l], rwkv_ln_b[l], b, t)
    x1, s_t, h2t = merge_query(xf, zp, o_ret, o_rwkv, w_ret_branch[l], w_rwkv_branch[l], w_out[l],
                               norm2_w[l], peer_w_q[l], peer_keys_1[l], peer_keys_2[l])
    lim, rank2, e1, e2 = peer_topk(s_t)
    out = peer_ffn(x1, h2t, lim, rank2, e1, e2, peer_u[l], peer_v[l], final_norm_w)
    return out.reshape(b, t, d)
```

```python
import functools

import numpy as np
import jax
import jax.numpy as jnp
from jax import lax
from jax.experimental import pallas as pl
from jax.experimental.pallas import tpu as pltpu

F32 = jnp.float32
BF16 = jnp.bfloat16

D_MODEL = 2048
CHUNK = 64
NORM_EPS = 1e-6
RET_HEADS = 4
RET_DK = 256
RET_DV = 256
RET_QK = RET_HEADS * RET_DK
RET_WIDTH = RET_HEADS * RET_DV
ROPE_BASE = 10000.0
RWKV_HEAD = 64
RWKV_WIDTH = D_MODEL // 2
RWKV_HEADS = RWKV_WIDTH // RWKV_HEAD
DECAY_LORA = 64
AAA_LORA = 64
GATE_LORA = 128
LORA_COLS = DECAY_LORA + AAA_LORA + GATE_LORA
RWKV_COLS = 3 * RWKV_WIDTH + LORA_COLS
RWKV_GN_EPS = 64e-5
IN_COLS = 2 * RET_QK + 2 * RET_WIDTH + RWKV_COLS + 2 * D_MODEL

ZP_RWKV = 2 * RET_QK + 2 * RET_WIDTH
ZP_LORA = ZP_RWKV + 3 * RWKV_WIDTH
ZP_GATE_RET = ZP_RWKV + RWKV_COLS
ZP_GATE_RWKV = ZP_GATE_RET + D_MODEL
assert ZP_LORA % LORA_COLS == 0

VMEM_LIMIT = 56 * 1024 * 1024


def _cparams(sem):
    return pltpu.CompilerParams(dimension_semantics=sem, vmem_limit_bytes=VMEM_LIMIT)


def _in_proj_kernel(x_ref, nw_ref, w_ref, o_ref, h_ref):
    @pl.when(pl.program_id(1) == 0)
    def _():
        x = x_ref[...]
        ms = jnp.mean(x * x, axis=-1, keepdims=True)
        h_ref[...] = (x * lax.rsqrt(ms + NORM_EPS) * nw_ref[...]).astype(BF16)

    o_ref[...] = jnp.dot(h_ref[...], w_ref[...], preferred_element_type=F32)


def _cast_kernel(x_ref, o_ref):
    o_ref[...] = x_ref[...].astype(BF16)


def cast_bf16(x, rows=256):
    n, cols = x.shape
    return pl.pallas_call(
        _cast_kernel,
        grid=(n // rows,),
        in_specs=[pl.BlockSpec((rows, cols), lambda i: (i, 0))],
        out_specs=pl.BlockSpec((rows, cols), lambda i: (i, 0)),
        out_shape=jax.ShapeDtypeStruct((n, cols), BF16),
        compiler_params=_cparams(("arbitrary",)),
        name="cast_bf16",
    )(x)


def in_proj(x, norm_w, w_bf16, tm=1024, tn=1280):
    n, d = x.shape
    cols = w_bf16.shape[1]
    return pl.pallas_call(
        _in_proj_kernel,
        grid=(n // tm, cols // tn),
        in_specs=[
            pl.BlockSpec((tm, d), lambda i, j: (i, 0)),
            pl.BlockSpec((1, d), lambda i, j: (0, 0)),
            pl.BlockSpec((d, tn), lambda i, j: (0, j)),
        ],
        out_specs=pl.BlockSpec((tm, tn), lambda i, j: (i, j)),
        out_shape=jax.ShapeDtypeStruct((n, cols), F32),
        scratch_shapes=[pltpu.VMEM((tm, d), BF16)],
        compiler_params=_cparams(("arbitrary", "arbitrary")),
        name="in_proj",
    )(x, norm_w.reshape(1, d), w_bf16)


def _retention_kernel(q_ref, k_ref, v_ref, g_ref, cos_ref, sin_ref, dintra_ref, qdec_ref, kdec_ref,
                      cdec_ref, gnw_ref, gnb_ref, o_ref, s_ref, qr_ref, kr_ref, acc_ref):
    tb = q_ref.shape[0]
    half = RET_DK // 2

    @pl.when(pl.program_id(1) == 0)
    def _():
        s_ref[...] = jnp.zeros_like(s_ref)

    cos = cos_ref[...]
    sin = sin_ref[...]

    def rot(z_ref, hd):
        z1 = z_ref[:, hd * RET_DK:hd * RET_DK + half]
        z2 = z_ref[:, hd * RET_DK + half:(hd + 1) * RET_DK]
        return jnp.concatenate([z1 * cos - z2 * sin, z2 * cos + z1 * sin], axis=-1)

    for hd in range(RET_HEADS):
        cols = pl.ds(hd * RET_DK, RET_DK)
        qr_ref[:, cols] = rot(q_ref, hd)
        kr_ref[:, cols] = rot(k_ref, hd) * (RET_DK ** -0.5)
    for c in range(tb // CHUNK):
        sl = pl.ds(c * CHUNK, CHUNK)
        for hd in range(RET_HEADS):
            cols = pl.ds(hd * RET_DK, RET_DK)
            qc = qr_ref[sl, cols]
            kc = kr_ref[sl, cols]
            vc = v_ref[sl, cols].astype(BF16)
            scores = lax.dot_general(qc.astype(BF16), kc.astype(BF16), (((1,), (1,)), ((), ())),
                                     preferred_element_type=F32) * dintra_ref[hd]
            s_prev = s_ref[hd]
            o = jnp.dot(scores.astype(BF16), vc, preferred_element_type=F32)
            o = o + jnp.dot((qc * qdec_ref[hd]).astype(BF16), s_prev.astype(BF16), preferred_element_type=F32)
            kd_t = (kc * kdec_ref[hd]).T.astype(BF16)
            s_ref[hd] = s_prev * cdec_ref[hd] + jnp.dot(kd_t, vc, preferred_element_type=F32)
            acc_ref[sl, cols] = o
    for hd in range(RET_HEADS):
        cols = pl.ds(hd * RET_DK, RET_DK)
        o = acc_ref[:, cols]
        mu = jnp.mean(o, axis=-1, keepdims=True)
        oc = o - mu
        var = jnp.mean(oc * oc, axis=-1, keepdims=True)
        y = oc * lax.rsqrt(var + NORM_EPS) * gnw_ref[:, cols] + gnb_ref[:, cols]
        g = g_ref[:, cols]
        o_ref[:, cols] = y * (g * jax.nn.sigmoid(g))


def _retention_tables(t):
    h = RET_HEADS
    log_g = jnp.log(1.0 - jnp.exp2(-5.0 - jnp.arange(h, dtype=F32)))
    n = jnp.arange(CHUNK, dtype=F32)
    d_intra = jnp.exp(log_g[:, None, None] * jnp.abs(n[:, None] - n[None, :]))
    q_decay = jnp.exp(log_g[:, None] * (n[None, :] + 1.0))
    k_decay = jnp.exp(log_g[:, None] * (CHUNK - 1.0 - n[None, :]))
    chunk_decay = jnp.exp(log_g * CHUNK)
    q_decay = jnp.broadcast_to(q_decay[:, :, None], (h, CHUNK, RET_DK))
    k_decay = jnp.broadcast_to(k_decay[:, :, None], (h, CHUNK, RET_DK))
    chunk_decay = jnp.broadcast_to(chunk_decay[:, None, None], (h, 1, RET_DV))
    half = RET_DK // 2
    inv_freq = ROPE_BASE ** (-jnp.arange(half, dtype=F32) * 2.0 / RET_DK)
    ang = jnp.arange(t, dtype=jnp.int32).astype(F32)[:, None] * inv_freq[None, :]
    return d_intra, q_decay, k_decay, chunk_decay, jnp.cos(ang), jnp.sin(ang)


def retention(zp, gn_w, gn_b, batch, t, tb=512):
    n = zp.shape[0]
    h = RET_HEADS
    nt = t // tb
    d_intra, q_decay, k_decay, chunk_decay, cos, sin = _retention_tables(t)

    def zspec(col_block):
        return pl.BlockSpec((tb, RET_WIDTH), lambda b, i: (b * nt + i, col_block))

    tab = lambda shape: pl.BlockSpec((h,) + shape, lambda b, i: (0, 0, 0))
    return pl.pallas_call(
        _retention_kernel,
        grid=(batch, nt),
        in_specs=[
            zspec(0), zspec(1), zspec(2), zspec(3),
            pl.BlockSpec((tb, RET_DK // 2), lambda b, i: (i, 0)),
            pl.BlockSpec((tb, RET_DK // 2), lambda b, i: (i, 0)),
            tab((CHUNK, CHUNK)), tab((CHUNK, RET_DK)), tab((CHUNK, RET_DK)), tab((1, RET_DV)),
            pl.BlockSpec((1, RET_WIDTH), lambda b, i: (0, 0)),
            pl.BlockSpec((1, RET_WIDTH), lambda b, i: (0, 0)),
        ],
        out_specs=pl.BlockSpec((tb, RET_WIDTH), lambda b, i: (b * nt + i, 0)),
        out_shape=jax.ShapeDtypeStruct((n, RET_WIDTH), F32),
        scratch_shapes=[
            pltpu.VMEM((h, RET_DK, RET_DV), F32),
            pltpu.VMEM((tb, RET_QK), F32),
            pltpu.VMEM((tb, RET_QK), F32),
            pltpu.VMEM((tb, RET_WIDTH), F32),
        ],
        compiler_params=_cparams(("arbitrary", "arbitrary")),
        name="retention",
    )(zp, zp, zp, zp, cos, sin, d_intra, q_decay, k_decay, chunk_decay,
      gn_w.reshape(1, RET_WIDTH), gn_b.reshape(1, RET_WIDTH))


def _split2(x):
    hi = x.astype(BF16)
    lo = (x - hi.astype(F32)).astype(BF16)
    return hi, lo


def _split3(x):
    hi = x.astype(BF16)
    r = x - hi.astype(F32)
    mid = r.astype(BF16)
    lo = (r - mid.astype(F32)).astype(BF16)
    return hi, mid, lo


def _group_sum(x, gmat):
    hi, lo = _split2(x)
    lanes = gmat.shape[0]
    outs = []
    for b in range(x.shape[1] // lanes):
        sl = slice(b * lanes, (b + 1) * lanes)
        outs.append(jnp.dot(hi[:, sl], gmat, preferred_element_type=F32)
                    + jnp.dot(lo[:, sl], gmat, preferred_element_type=F32))
    return jnp.concatenate(outs, axis=-1)


def _bmm(a, b):
    return lax.dot_general(a.astype(BF16), b.astype(BF16), (((2,), (1,)), ((0,), (0,))),
                           preferred_element_type=F32)


def _bmm_nt(a, b):
    return lax.dot_general(a.astype(BF16), b.astype(BF16), (((2,), (2,)), ((0,), (0,))),
                           preferred_element_type=F32)


def _bmm_tn(a, b):
    return lax.dot_general(a.astype(BF16), b.astype(BF16), (((1,), (1,)), ((0,), (0,))),
                           preferred_element_type=F32)


RWKV_HEAD_GROUP = 16
GROUP_LANES = 128
RWKV_TILE = 128


def _rwkv_kernel(r_ref, k_ref, v_ref, l_ref, pr_ref, pk_ref, pv_ref, plr_ref,
                     mixr_ref, mixk_ref, mixv_ref, mixl_ref, w0_ref, a0_ref, kk_ref, ka_ref, rk_ref,
                     wup_ref, aup_ref, gup_ref, gmat_ref, ltri_ref, lnw_ref, lnb_ref,
                     o_ref,
                     at_s, rt_s, bh_s, kh_s, bt_s, kt_s, v_s, gc_s, g_s, bv_s, s_ref, ys_ref, *, tiles_per_batch):
    tb = r_ref.shape[0]
    nchunk = tb // CHUNK
    first = (pl.program_id(0) % tiles_per_batch) == 0
    rowid = lax.broadcasted_iota(jnp.int32, (tb, 1), 0)

    @pl.when(first)
    def _():
        s_ref[...] = jnp.zeros_like(s_ref)

    def shift_lerp(cur_ref, prev_ref, mix_ref):
        cur = cur_ref[...]
        prev_row = jnp.where(first, 0.0, prev_ref[7:8, :])
        prev = jnp.where(rowid == 0, prev_row, pltpu.roll(cur, 1, axis=0))
        return cur + (prev - cur) * mix_ref[...]

    r = shift_lerp(r_ref, pr_ref, mixr_ref)
    k = shift_lerp(k_ref, pk_ref, mixk_ref)
    v = shift_lerp(v_ref, pv_ref, mixv_ref)
    lo = shift_lerp(l_ref, plr_ref, mixl_ref)
    xw = lo[:, :DECAY_LORA]
    xa = lo[:, DECAY_LORA:DECAY_LORA + AAA_LORA]
    xg = lo[:, DECAY_LORA + AAA_LORA:]
    wl = jnp.dot(jnp.tanh(xw).astype(BF16), wup_ref[...], preferred_element_type=F32)
    u = -(w0_ref[...] + wl)
    softplus = jnp.maximum(u, 0.0) + jnp.log(1.0 + jnp.exp(-jnp.abs(u)))
    logw = -jnp.exp(-softplus - 0.5)
    a = jax.nn.sigmoid(a0_ref[...] + jnp.dot(xa.astype(BF16), aup_ref[...], preferred_element_type=F32))
    g_s[...] = jnp.dot(jax.nn.sigmoid(xg).astype(BF16), gup_ref[...], preferred_element_type=F32)
    gmat = gmat_ref[...]
    kk = k * kk_ref[...]
    kk = kk / jnp.maximum(jnp.sqrt(_group_sum(kk * kk, gmat)), 1e-12)
    k2 = k * (1.0 + (a - 1.0) * ka_ref[...])
    bv_s[...] = _group_sum(r * k2 * rk_ref[...], gmat) * v
    beta = kk * a
    ltri = ltri_ref[...]

    def to_heads(dst, c, val):
        for h in range(RWKV_HEADS):
            dst[h, pl.ds(c * CHUNK, CHUNK), :] = val[:, h * RWKV_HEAD:(h + 1) * RWKV_HEAD]

    for c in range(nchunk):
        sl = slice(c * CHUNK, (c + 1) * CHUNK)
        lw = logw[sl]
        h1, h2, h3 = _split3(lw)
        cs = (jnp.dot(ltri, h1, preferred_element_type=F32) + jnp.dot(ltri, h2, preferred_element_type=F32)
              + jnp.dot(ltri, h3, preferred_element_type=F32))
        cs_last = cs[CHUNK - 1:CHUNK, :]
        g_t = jnp.exp(cs)
        g_tm1 = jnp.exp(cs - lw)
        inv_g = jnp.exp(-cs)
        g_end = jnp.exp(cs_last - cs)
        to_heads(at_s, c, -kk[sl] * g_tm1)
        to_heads(rt_s, c, r[sl] * g_t)
        to_heads(bh_s, c, beta[sl] * inv_g)
        to_heads(kh_s, c, k2[sl] * inv_g)
        to_heads(bt_s, c, beta[sl] * g_end)
        to_heads(kt_s, c, k2[sl] * g_end)
        to_heads(v_s, c, v[sl])
        gc = jnp.exp(cs_last)
        for h in range(RWKV_HEADS):
            gc_s[h, pl.ds(c * 8, 8), :] = jnp.broadcast_to(gc[:, h * RWKV_HEAD:(h + 1) * RWKV_HEAD], (8, RWKV_HEAD))

    G = RWKV_HEAD_GROUP
    ti = lax.broadcasted_iota(jnp.int32, (CHUNK, CHUNK), 0)
    si = lax.broadcasted_iota(jnp.int32, (CHUNK, CHUNK), 1)
    strict = (si < ti)[None]
    incl = (si <= ti)[None]
    eye = (si == ti).astype(F32)[None]

    def body(idx, carry):
        c = idx // (RWKV_HEADS // G)
        hg = idx % (RWKV_HEADS // G)
        hs = pl.ds(hg * G, G)
        ts = pl.ds(c * CHUNK, CHUNK)
        at = at_s[hs, ts, :]
        rt = rt_s[hs, ts, :]
        bh = bh_s[hs, ts, :]
        kh = kh_s[hs, ts, :]
        bt = bt_s[hs, ts, :]
        kt = kt_s[hs, ts, :]
        vv = v_s[hs, ts, :]
        gc = gc_s[hs, pl.ds(c * 8, 1), :]
        aab = jnp.where(strict, _bmm_nt(at, bh), 0.0)
        aak = jnp.where(strict, _bmm_nt(at, kh), 0.0)
        aqb = jnp.where(incl, _bmm_nt(rt, bh), 0.0)
        aqk = jnp.where(incl, _bmm_nt(rt, kh), 0.0)
        p = aab
        tm = eye + aab
        for _ in range(5):
            p = _bmm(p, p)
            tm = tm + _bmm(p, tm)
        akv = _bmm(aak, vv)
        wt = _bmm(tm, at)
        u0 = _bmm(tm, akv)
        qt = rt + _bmm(aqb, wt)
        y1 = _bmm(aqb, u0) + _bmm(aqk, vv)
        m = _bmm_tn(bt, wt) + eye * gc
        nn = _bmm_tn(bt, u0) + _bmm_tn(kt, vv)
        s = s_ref[hs]
        ys_ref[hs, ts, :] = _bmm(qt, s) + y1
        s_ref[hs] = _bmm(m, s) + nn
        return carry

    for idx in range(nchunk * (RWKV_HEADS // G)):
        body(idx, 0)

    y = jnp.concatenate([ys_ref[h] for h in range(RWKV_HEADS)], axis=-1)
    mu = _group_sum(y, gmat) * (1.0 / RWKV_HEAD)
    yc = y - mu
    var = _group_sum(yc * yc, gmat) * (1.0 / RWKV_HEAD)
    yn = yc * lax.rsqrt(var + RWKV_GN_EPS) * lnw_ref[...] + lnb_ref[...]
    o_ref[...] = (yn + bv_s[...]) * g_s[...]


def _rwkv_consts():
    lane = np.arange(GROUP_LANES) // RWKV_HEAD
    gmat = (lane[:, None] == lane[None, :]).astype(np.float32)
    ltri = np.tril(np.ones((CHUNK, CHUNK), np.float32))
    return jnp.asarray(gmat, BF16), jnp.asarray(ltri, BF16)


def rwkv_time_mix(zp, mix, w0, w_up, a0, a_up, g_up, k_k, k_a, r_k, ln_w, ln_b, batch, t, tb=RWKV_TILE):
    n = zp.shape[0]
    W = RWKV_WIDTH
    nt = t // tb
    gmat, ltri = _rwkv_consts()
    row = lambda p: p.reshape(1, -1)
    mix_r, mix_k, mix_v, mix_l = mix[:W], mix[W:2 * W], mix[2 * W:3 * W], mix[3 * W:]
    cb = ZP_RWKV // W
    cur = lambda j: pl.BlockSpec((tb, W), lambda i: (i, cb + j))
    prev = lambda j: pl.BlockSpec((8, W), lambda i: (jnp.maximum(i * (tb // 8) - 1, 0), cb + j))
    full = lambda shape: pl.BlockSpec(shape, lambda i: (0,) * len(shape))
    hscr = pltpu.VMEM((RWKV_HEADS, tb, RWKV_HEAD), F32)
    return pl.pallas_call(
        functools.partial(_rwkv_kernel, tiles_per_batch=nt),
        grid=(n // tb,),
        in_specs=[
            cur(0), cur(1), cur(2),
            pl.BlockSpec((tb, LORA_COLS), lambda i: (i, ZP_LORA // LORA_COLS)),
            prev(0), prev(1), prev(2),
            pl.BlockSpec((8, LORA_COLS), lambda i: (jnp.maximum(i * (tb // 8) - 1, 0), ZP_LORA // LORA_COLS)),
            full((1, W)), full((1, W)), full((1, W)), full((1, LORA_COLS)),
            full((1, W)), full((1, W)), full((1, W)), full((1, W)), full((1, W)),
            full((DECAY_LORA, W)), full((AAA_LORA, W)), full((GATE_LORA, W)),
            full((GROUP_LANES, GROUP_LANES)), full((CHUNK, CHUNK)), full((1, W)), full((1, W)),
        ],
        out_specs=pl.BlockSpec((tb, W), lambda i: (i, 0)),
        out_shape=jax.ShapeDtypeStruct((n, W), F32),
        scratch_shapes=[hscr] * 7 + [pltpu.VMEM((RWKV_HEADS, 8 * (tb // CHUNK), RWKV_HEAD), F32),
                                     pltpu.VMEM((tb, W), F32), pltpu.VMEM((tb, W), F32),
                                     pltpu.VMEM((RWKV_HEADS, RWKV_HEAD, RWKV_HEAD), F32), hscr],
        compiler_params=_cparams(("arbitrary",)),
        name="rwkv_time_mix",
    )(zp, zp, zp, zp, zp, zp, zp, zp,
      row(mix_r), row(mix_k), row(mix_v), row(mix_l), row(w0), row(a0), row(k_k), row(k_a), row(r_k),
      w_up.astype(BF16), a_up.astype(BF16), g_up.astype(BF16), gmat, ltri, row(ln_w), row(ln_b))


PEER_HEADS = 8
PEER_NKEYS = 128
PEER_NEXPERTS = PEER_NKEYS * PEER_NKEYS
PEER_DQ = 256
PEER_TOPK = 16
PEER_ACT_CHUNKS = 4
PEER_PRE_PIECES = 2
PEER_TOKEN_TILE = 256


def _merge_query_kernel(x_ref, gr_ref, gw_ref, oret_ref, orwkv_ref, wr_ref, ww_ref, wo_ref, nw_ref, wq_ref,
                        k1_ref, k2_ref, x1_ref, s_ref, ht_ref):
    pr = jnp.dot(oret_ref[...].astype(BF16), wr_ref[...], preferred_element_type=F32)
    pw = jnp.dot(orwkv_ref[...].astype(BF16), ww_ref[...], preferred_element_type=F32)
    merged = jax.nn.sigmoid(gr_ref[...]) * pr + jax.nn.sigmoid(gw_ref[...]) * pw
    x1 = x_ref[...] + jnp.dot(merged.astype(BF16), wo_ref[...], preferred_element_type=F32)
    x1_ref[...] = x1
    ms = jnp.mean(x1 * x1, axis=-1, keepdims=True)
    h2 = x1 * lax.rsqrt(ms + NORM_EPS) * nw_ref[...]
    ht_ref[...] = h2.T.astype(BF16)
    q = jnp.dot(h2.astype(BF16), wq_ref[...], preferred_element_type=F32).astype(BF16)
    half = PEER_DQ // 2
    for h in range(PEER_HEADS):
        for p, kref in enumerate((k1_ref, k2_ref)):
            qh = q[:, h * PEER_DQ + p * half: h * PEER_DQ + (p + 1) * half]
            s_ref[p, h] = lax.dot_general(kref[...], qh, (((1,), (1,)), ((), ())), preferred_element_type=F32)


def merge_query(x, zp, o_ret, o_rwkv, w_ret, w_rwkv, w_out, norm_w, w_q, keys_1, keys_2, tm=PEER_TOKEN_TILE):
    n, d = x.shape
    const = lambda shape: pl.BlockSpec(shape, lambda i: (0, 0), pipeline_mode=pl.Buffered(1))

    def gate(col0):
        return pl.BlockSpec((pl.Element(tm), pl.Element(d)), lambda i: (i * tm, col0))

    return pl.pallas_call(
        _merge_query_kernel,
        grid=(n // tm,),
        in_specs=[pl.BlockSpec((tm, d), lambda i: (i, 0)), gate(ZP_GATE_RET), gate(ZP_GATE_RWKV),
                  pl.BlockSpec((tm, RET_WIDTH), lambda i: (i, 0)),
                  pl.BlockSpec((tm, RWKV_WIDTH), lambda i: (i, 0)),
                  const((RET_WIDTH, d)), const((RWKV_WIDTH, d)), const((d, d)),
                  const((1, d)), const((d, PEER_HEADS * PEER_DQ)),
                  const((PEER_NKEYS, PEER_DQ // 2)), const((PEER_NKEYS, PEER_DQ // 2))],
        out_specs=[
            pl.BlockSpec((tm, d), lambda i: (i, 0)),
            pl.BlockSpec((None, 2, PEER_HEADS, PEER_NKEYS, tm), lambda i: (i, 0, 0, 0, 0)),
            pl.BlockSpec((None, d, tm), lambda i: (i, 0, 0)),
        ],
        out_shape=[
            jax.ShapeDtypeStruct((n, d), F32),
            jax.ShapeDtypeStruct((n // tm, 2, PEER_HEADS, PEER_NKEYS, tm), F32),
            jax.ShapeDtypeStruct((n // tm, d, tm), BF16),
        ],
        compiler_params=_cparams(("arbitrary",)),
        name="merge_query",
    )(x, zp, zp, o_ret, o_rwkv, w_ret.astype(BF16), w_rwkv.astype(BF16), w_out.astype(BF16),
      norm_w.reshape(1, d), w_q.astype(BF16), keys_1.astype(BF16), keys_2.astype(BF16))


_CAND_GROUPS = [(0, 0), (0, 8)] + [(a, 0) for a in range(1, 8)]


def _peer_topk_kernel(s_ref, lim_ref, rank2_ref, e1_ref, e2_ref, v1_s, r1_s, v2_s, r2_s, cnt_s, z_s):
    tn = s_ref.shape[-1]
    K = PEER_TOPK
    neg = -jnp.inf
    rowid = lax.broadcasted_iota(jnp.int32, (PEER_NKEYS, tn), 0)
    nrow = 8 * (len(_CAND_GROUPS) + 1)
    r = lax.broadcasted_iota(jnp.int32, (nrow, tn), 0)
    grp, sub = r // 8, r % 8
    ca = jnp.where(grp < 2, 0, jnp.where(grp < 9, grp - 1, 8 + sub))
    cb = jnp.where(grp == 1, 8 + sub, jnp.where(grp < 9, sub, 0))
    flat = ca * K + cb
    valid = (ca + 1) * (cb + 1) <= K

    def top16(s, break_ties):
        rank = jnp.full((PEER_NKEYS, tn), K, jnp.int32)
        vals = []
        for a in range(K):
            m = jnp.max(s, axis=0, keepdims=True)
            hit = s == m
            if break_ties:
                idx = jnp.min(jnp.where(hit, rowid, PEER_NKEYS), axis=0, keepdims=True)
                hit = rowid == idx
            rank = jnp.where(hit, a, rank)
            s = jnp.where(hit, neg, s)
            vals.append(m)
        return jnp.concatenate(vals, axis=0), rank

    def head(h, carry):
        s1 = s_ref[0, h]
        s2 = s_ref[1, h]
        v1_s[...], r1_s[...] = top16(s1, False)
        v2_s[...], r2_s[...] = top16(s2, False)
        ranked = (jnp.sum((r1_s[...] < K).astype(jnp.int32), axis=0, keepdims=True)
                  + jnp.sum((r2_s[...] < K).astype(jnp.int32), axis=0, keepdims=True))
        tied = jnp.max(jnp.abs(ranked - 2 * K)) > 0

        @pl.when(tied)
        def _():
            v1_s[...], r1_s[...] = top16(s1, True)
            v2_s[...], r2_s[...] = top16(s2, True)

        v1, rank1 = v1_s[...], r1_s[...]
        v2, rank2 = v2_s[...], r2_s[...]
        pieces = [v1[a:a + 1] + v2[b0:b0 + 8] for a, b0 in _CAND_GROUPS] + [v1[8:16] + v2[0:1]]
        cand = jnp.where(valid, jnp.concatenate(pieces, axis=0), neg)
        m0 = v1[0:1] + v2[0:1]

        def best16(cand, break_ties):
            z = jnp.zeros((1, tn), F32)
            for _ in range(K):
                m = jnp.max(cand, axis=0, keepdims=True)
                hit = cand == m
                if break_ties:
                    f = jnp.min(jnp.where(hit, flat, K * K), axis=0, keepdims=True)
                    hit = flat == f
                cand = jnp.where(hit, neg, cand)
                z = z + jnp.exp(m - m0)
            chosen = jnp.where(jnp.logical_and(valid, cand == neg), 1.0, 0.0)
            per_a = [jnp.sum(chosen[0:16], axis=0, keepdims=True)]
            per_a += [jnp.sum(chosen[8 * (a + 1):8 * (a + 2)], axis=0, keepdims=True) for a in range(1, 8)]
            return jnp.concatenate(per_a + [chosen[8 * 9:8 * 10]], axis=0), z

        cnt_s[...], z_s[...] = best16(cand, False)
        tied2 = jnp.max(jnp.abs(jnp.sum(cnt_s[...], axis=0, keepdims=True) - K)) > 0

        @pl.when(tied2)
        def _():
            cnt_s[...], z_s[...] = best16(cand, True)

        count = cnt_s[...]
        inv_z = 1.0 / z_s[...]
        e1_ref[h] = jnp.where(rank1 < K, jnp.exp(s1 - v1[0:1]) * inv_z, 0.0)
        e2_ref[h] = jnp.where(rank2 < K, jnp.exp(s2 - v2[0:1]), 0.0).astype(BF16)
        rank2_ref[h] = rank2.astype(F32).astype(BF16)
        lim = jnp.zeros((PEER_NKEYS, tn), F32)
        for a in range(K):
            lim = jnp.where(rank1 == a, count[a:a + 1], lim)
        lim_ref[h] = lim
        return carry

    lax.fori_loop(0, PEER_HEADS, head, 0)


def peer_topk(s_t):
    ntile, _, _, _, tn = s_t.shape
    n = ntile * tn
    spec = pl.BlockSpec((PEER_HEADS, PEER_NKEYS, tn), lambda i: (0, 0, i))
    shp = lambda dt: jax.ShapeDtypeStruct((PEER_HEADS, PEER_NKEYS, n), dt)
    return pl.pallas_call(
        _peer_topk_kernel,
        grid=(ntile,),
        in_specs=[pl.BlockSpec((None, 2, PEER_HEADS, PEER_NKEYS, tn), lambda i: (i, 0, 0, 0, 0))],
        out_specs=[spec, spec, spec, spec],
        out_shape=[shp(F32), shp(BF16), shp(F32), shp(BF16)],
        scratch_shapes=[pltpu.VMEM((PEER_TOPK, tn), F32), pltpu.VMEM((PEER_NKEYS, tn), jnp.int32),
                        pltpu.VMEM((PEER_TOPK, tn), F32), pltpu.VMEM((PEER_NKEYS, tn), jnp.int32),
                        pltpu.VMEM((PEER_TOPK, tn), F32), pltpu.VMEM((1, tn), F32)],
        compiler_params=_cparams(("arbitrary",)),
        name="peer_topk",
    )(s_t)


def _peer_ffn_kernel(x_ref, ht_ref, lim_ref, rank2_ref, e1_ref, e2_ref, u_ref, vt_ref, fnw_ref, o_ref,
                     acc_ref, *scratch):
    j = pl.program_id(1)
    eb = u_ref.shape[0]
    nsub = eb // PEER_NKEYS
    pre_refs, act_refs = scratch[:PEER_PRE_PIECES], scratch[PEER_PRE_PIECES:]
    piece = eb // len(pre_refs)
    per_chunk = nsub // len(act_refs)
    ck = per_chunk * PEER_NKEYS
    zero = jnp.zeros((), BF16)

    @pl.when(j == 0)
    def _():
        acc_ref[...] = jnp.zeros_like(acc_ref)

    tt = ht_ref.shape[-1]
    for k, pre_k in enumerate(pre_refs):
        for sub in range(ht_ref.shape[0]):
            pre_k[:, pl.ds(sub * tt, tt)] = jnp.dot(u_ref[pl.ds(k * piece, piece), :], ht_ref[sub],
                                                    preferred_element_type=F32)
    for c, act_c in enumerate(act_refs):
        for jc in range(per_chunk):
            jj = c * per_chunk + jc
            gate = None
            for h in range(PEER_HEADS):
                lrow = lim_ref[h, jj:jj + 1, :].astype(BF16)
                erow = e1_ref[h, jj:jj + 1, :].astype(BF16)
                term = jnp.where(rank2_ref[h] < lrow, e2_ref[h], zero) * erow
                gate = term if gate is None else gate + term
            p = pre_refs[jj * PEER_NKEYS // piece][pl.ds(jj * PEER_NKEYS % piece, PEER_NKEYS), :]
            gelu = 0.5 * p * (1.0 + lax.erf(p * (2.0 ** -0.5)))
            act_c[pl.ds(jc * PEER_NKEYS, PEER_NKEYS), :] = gelu.astype(BF16) * gate
        acc_ref[...] += jnp.dot(vt_ref[:, pl.ds(c * ck, ck)], act_c[...], preferred_element_type=F32)

    @pl.when(j == pl.num_programs(1) - 1)
    def _():
        y = x_ref[...] + acc_ref[...].T
        ms = jnp.mean(y * y, axis=-1, keepdims=True)
        o_ref[...] = y * lax.rsqrt(ms + NORM_EPS) * fnw_ref[...]


def _transpose_cast_kernel(v_ref, o_ref):
    o_ref[...] = v_ref[...].T.astype(BF16)


def transpose_cast(v, rows=512):
    ne, d = v.shape
    return pl.pallas_call(
        _transpose_cast_kernel,
        grid=(ne // rows,),
        in_specs=[pl.BlockSpec((rows, d), lambda i: (i, 0))],
        out_specs=pl.BlockSpec((d, rows), lambda i: (0, i)),
        out_shape=jax.ShapeDtypeStruct((d, ne), BF16),
        compiler_params=_cparams(("arbitrary",)),
        name="transpose_cast",
    )(v)


def peer_ffn(x1, h2t, lim, rank2, e1, e2, expert_u, expert_v, final_norm_w, tn=512, eb=1024):
    n, d = x1.shape
    ne = expert_u.shape[0]
    u = expert_u.astype(BF16)
    vt = transpose_cast(expert_v)
    nsub = eb // PEER_NKEYS
    assert nsub == 8
    tok = pl.BlockSpec((PEER_HEADS, PEER_NKEYS, tn), lambda i, j: (0, 0, i))
    row = pl.BlockSpec((PEER_HEADS, nsub, tn), lambda i, j: (0, j, i))
    return pl.pallas_call(
        _peer_ffn_kernel,
        grid=(n // tn, ne // eb),
        in_specs=[
            pl.BlockSpec((tn, d), lambda i, j: (i, 0)),
            pl.BlockSpec((tn // h2t.shape[-1], d, h2t.shape[-1]), lambda i, j: (i, 0, 0)),
            row, tok, row, tok,
            pl.BlockSpec((eb, d), lambda i, j: (j, 0)),
            pl.BlockSpec((d, eb), lambda i, j: (0, j)),
            pl.BlockSpec((1, d), lambda i, j: (0, 0)),
        ],
        out_specs=pl.BlockSpec((tn, d), lambda i, j: (i, 0)),
        out_shape=jax.ShapeDtypeStruct((n, d), F32),
        scratch_shapes=[pltpu.VMEM((d, tn), F32)]
        + [pltpu.VMEM((eb // PEER_PRE_PIECES, tn), F32)] * PEER_PRE_PIECES
        + [pltpu.VMEM((eb // PEER_ACT_CHUNKS, tn), BF16)] * PEER_ACT_CHUNKS,
        compiler_params=_cparams(("arbitrary", "arbitrary")),
        name="peer_ffn",
    )(x1, h2t, lim, rank2, e1, e2, u, vt, final_norm_w.reshape(1, d))


def kernel(x, norm1_w, w_in, ret_gn_w, ret_gn_b, rwkv_mix, rwkv_w0, rwkv_w_up, rwkv_a0, rwkv_a_up, rwkv_g_up, rwkv_k_k, rwkv_k_a, rwkv_r_k, rwkv_ln_w, rwkv_ln_b, w_ret_branch, w_rwkv_branch, w_out, norm2_w, peer_w_q, peer_keys_1, peer_keys_2, peer_u, peer_v, final_norm_w):
    b, t, d = x.shape
    n = b * t
    assert w_in.shape[0] == 1, "single-layer block: the final norm is fused into the PEER kernel"
    l = 0
    xf = x.reshape(n, d)
    zp = in_proj(xf, norm1_w[l], cast_bf16(w_in[l]))
    o_ret = retention(zp, ret_gn_w[l], ret_gn_b[l], b, t)
    o_rwkv = rwkv_time_mix(zp, rwkv_mix[l], rwkv_w0[l], rwkv_w_up[l], rwkv_a0[l], rwkv_a_up[l], rwkv_g_up[l],
                           rwkv_k_k[l], rwkv_k_a[l], rwkv_r_k[l], rwkv_ln_w[l], rwkv_ln_b[l], b, t)
    x1, s_t, h2t = merge_query(xf, zp, o_ret, o_rwkv, w_ret_branch[l], w_rwkv_branch[l], w_out[l],
                               norm2_w[l], peer_w_q[l], peer_keys_1[l], peer_keys_2[l])
    lim, rank2, e1, e2 = peer_topk(s_t)
    out = peer_ffn(x1, h2t, lim, rank2, e1, e2, peer_u[l], peer_v[l], final_norm_w)
    return out.reshape(b, t, d)
```
